```python
import jax
import jax.numpy as jnp
from jax import lax
import numpy as np

D_MODEL = 4096
BATCH = 2
SEQ = 8192
DEPTH = 2

HEAD_DIM = 128
ROPE_THETA = 10000.0
LN_EPS = 1e-5
NORM_EPS = 1e-6
NEG_INF = -1e30

NSA_Q_HEADS = D_MODEL // 256
NSA_KV_GROUPS = 4
NSA_HEADS_PER_GROUP = NSA_Q_HEADS // NSA_KV_GROUPS
NSA_WIDTH = NSA_Q_HEADS * HEAD_DIM
NSA_CMP_LEN = 32
NSA_CMP_STRIDE = 16
NSA_SEL_LEN = 64
NSA_N_SEL = 16
NSA_WINDOW = 512
NSA_CMP_HIDDEN = 128
NSA_QBLOCK = 32
NSA_FORCE_BONUS = 1e4

GLA_HEADS = 4
GLA_WIDTH = D_MODEL // 2
GLA_DV = GLA_WIDTH // GLA_HEADS
GLA_DK = GLA_DV // 2
GLA_RANK = 16
GLA_TAU = 16.0
GLA_CHUNK = 64

DIL_GROUPS = ((128, 1), (512, 4), (2048, 16))
DIL_HEADS = 16
DIL_WIDTH = DIL_HEADS * HEAD_DIM
DIL_QBLOCK = 128

DEEPNORM_ALPHA = (2.0 * DEPTH) ** 0.25
DEEPNORM_BETA = (8.0 * DEPTH) ** -0.25

L0_SPLITS = (NSA_WIDTH, 6 * NSA_KV_GROUPS * HEAD_DIM, 3 * NSA_Q_HEADS, NSA_WIDTH,
             GLA_HEADS * GLA_DK, GLA_HEADS * GLA_DK, GLA_WIDTH, GLA_RANK, GLA_WIDTH)
L1_SPLITS = (len(DIL_GROUPS) * DIL_WIDTH, len(DIL_GROUPS) * DIL_WIDTH,
             len(DIL_GROUPS) * DIL_WIDTH, DIL_WIDTH)

kernel_name = 'hybrid_nsa_gla_dilated_deepnorm'


def _split_columns(h, sizes):
    offs = [int(v) for v in np.cumsum(sizes)[:-1]]
    return jnp.split(h, offs, axis=-1)


def _rope(x, pos):
    half = x.shape[-1] // 2
    inv_freq = ROPE_THETA ** (-jnp.arange(half, dtype=jnp.float32) / half)
    ang = pos.astype(jnp.float32)[:, None] * inv_freq[None, :]
    cos = jnp.cos(ang).astype(x.dtype)
    sin = jnp.sin(ang).astype(x.dtype)
    x1, x2 = x[..., :half], x[..., half:]
    return jnp.concatenate([x1 * cos - x2 * sin, x1 * sin + x2 * cos], axis=-1)


def _layer_norm(x, g, b):
    xf = x.astype(jnp.float32)
    mu = jnp.mean(xf, axis=-1, keepdims=True)
    var = jnp.mean(jnp.square(xf - mu), axis=-1, keepdims=True)
    y = (xf - mu) * lax.rsqrt(var + LN_EPS) * g.astype(jnp.float32) + b.astype(jnp.float32)
    return y.astype(x.dtype)


def _masked_softmax(s, mask):
    s = jnp.where(mask, s.astype(jnp.float32), NEG_INF)
    p = jax.nn.softmax(s, axis=-1)
    return jnp.where(jnp.any(mask, axis=-1, keepdims=True), p, 0.0)


def _compress_blocks(z, pe, w1, w2):
    B, G, S, dh = z.shape
    n_sub = NSA_CMP_LEN // NSA_CMP_STRIDE
    n_chunk = S // NSA_CMP_STRIDE
    n_cmp = n_chunk - n_sub + 1
    ch = z.reshape(B, G, n_chunk, NSA_CMP_STRIDE, dh)
    blk = jnp.concatenate([ch[:, :, j:j + n_cmp] for j in range(n_sub)], axis=3)
    blk = (blk + pe).reshape(B, G, n_cmp, NSA_CMP_LEN * dh)
    return jax.nn.silu(blk @ w1) @ w2


def _nsa_mixer(q, kv, gate_logits, pe_k, w1_k, w2_k, pe_v, w1_v, w2_v):
    B, S, _ = q.shape
    G, Hg, dh = NSA_KV_GROUPS, NSA_HEADS_PER_GROUP, HEAD_DIM
    pos = jnp.arange(S)
    q = _rope(q.reshape(B, S, G, Hg, dh).transpose(0, 2, 3, 1, 4) * dh ** -0.5, pos)
    kv = kv.reshape(B, S, 6, G, dh).transpose(2, 0, 3, 1, 4)
    k_c, v_c, k_s, v_s, k_w, v_w = kv[0], kv[1], kv[2], kv[3], kv[4], kv[5]
    k_c = _compress_blocks(k_c, pe_k, w1_k, w2_k)
    v_c = _compress_blocks(v_c, pe_v, w1_v, w2_v)
    n_cmp = k_c.shape[2]
    cmp_end = jnp.arange(n_cmp) * NSA_CMP_STRIDE + NSA_CMP_LEN - 1
    k_c = _rope(k_c, cmp_end)
    cmp_start = cmp_end - (NSA_CMP_LEN - 1)
    n_blk = S // NSA_SEL_LEN
    blk = jnp.arange(n_blk)
    k_s = _rope(k_s, pos).reshape(B, G, n_blk, NSA_SEL_LEN, dh)
    v_s = v_s.reshape(B, G, n_blk, NSA_SEL_LEN, dh)
    overlap = ((cmp_end[:, None] >= blk[None, :] * NSA_SEL_LEN)
               & (cmp_start[:, None] < (blk[None, :] + 1) * NSA_SEL_LEN)).astype(jnp.float32)
    n_sel = min(NSA_N_SEL, n_blk)
    padw = ((0, 0), (0, 0), (NSA_WINDOW, 0), (0, 0))
    k_w = jnp.pad(_rope(k_w, pos), padw)
    v_w = jnp.pad(v_w, padw)
    gates = jax.nn.sigmoid(gate_logits.astype(jnp.float32)).reshape(B, S, G, Hg, 3).transpose(0, 2, 3, 1, 4)
    nqb = S // NSA_QBLOCK
    q_blocks = jnp.moveaxis(q.reshape(B, G, Hg, nqb, NSA_QBLOCK, dh), 3, 0)
    g_blocks = jnp.moveaxis(gates.reshape(B, G, Hg, nqb, NSA_QBLOCK, 3), 3, 0)
    b_idx = jnp.arange(B)[:, None, None, None]
    g_idx = jnp.arange(G)[None, :, None, None]

    def block_fn(args):
        qc, gc, c = args
        t = c * NSA_QBLOCK + jnp.arange(NSA_QBLOCK)
        m_c = cmp_end[None, :] <= t[:, None]
        p_c = _masked_softmax(jnp.einsum('bghqd,bgnd->bghqn', qc, k_c), m_c)
        o_c = jnp.einsum('bghqn,bgnd->bghqd', p_c.astype(v_c.dtype), v_c)
        imp = jnp.einsum('bgqn,nj->bgqj', p_c.sum(axis=2), overlap)
        cur = t // NSA_SEL_LEN
        forced = (blk[None, :] == 0) | (blk[None, :] == cur[:, None]) | (blk[None, :] == cur[:, None] - 1)
        imp = jnp.where(forced, imp + NSA_FORCE_BONUS, imp)
        imp = jnp.where(blk[None, :] <= cur[:, None], imp, NEG_INF)
        _, sel = lax.top_k(imp, n_sel)
        k_g = k_s[b_idx, g_idx, sel].reshape(B, G, NSA_QBLOCK, n_sel * NSA_SEL_LEN, dh)
        v_g = v_s[b_idx, g_idx, sel].reshape(B, G, NSA_QBLOCK, n_sel * NSA_SEL_LEN, dh)
        kpos = (sel[..., None] * NSA_SEL_LEN + jnp.arange(NSA_SEL_LEN)).reshape(B, G, NSA_QBLOCK, n_sel * NSA_SEL_LEN)
        m_s = (kpos <= t[:, None])[:, :, None]
        p_s = _masked_softmax(jnp.einsum('bghqd,bgqkd->bghqk', qc, k_g), m_s)
        o_s = jnp.einsum('bghqk,bgqkd->bghqd', p_s.astype(v_g.dtype), v_g)
        k_wc = lax.dynamic_slice_in_dim(k_w, c * NSA_QBLOCK, NSA_WINDOW + NSA_QBLOCK, axis=2)
        v_wc = lax.dynamic_slice_in_dim(v_w, c * NSA_QBLOCK, NSA_WINDOW + NSA_QBLOCK, axis=2)
        wpos = c * NSA_QBLOCK - NSA_WINDOW + jnp.arange(NSA_WINDOW + NSA_QBLOCK)
        rel = t[:, None] - wpos[None, :]
        m_w = (rel >= 0) & (rel < NSA_WINDOW) & (wpos[None, :] >= 0)
        p_w = _masked_softmax(jnp.einsum('bghqd,bgkd->bghqk', qc, k_wc), m_w)
        o_w = jnp.einsum('bghqk,bgkd->bghqd', p_w.astype(v_wc.dtype), v_wc)
        o = gc[..., 0:1] * o_c + gc[..., 1:2] * o_s + gc[..., 2:3] * o_w
        return o.astype(qc.dtype)

    o = lax.map(block_fn, (q_blocks, g_blocks, jnp.arange(nqb)))
    o = jnp.moveaxis(o, 0, 3).reshape(B, G, Hg, S, dh)
    return o.transpose(0, 3, 1, 2, 4).reshape(B, S, NSA_WIDTH)


def _gla_mixer(q, k, v, a_lr, w_a2, b_a, norm_g):
    B, S, _ = q.shape
    H, C = GLA_HEADS, GLA_CHUNK
    nC = S // C
    f32 = jnp.float32

    def heads(z, d):
        return z.astype(f32).reshape(B, S, H, d).transpose(0, 2, 1, 3).reshape(B, H, nC, C, d)

    q = heads(q, GLA_DK) * GLA_DK ** -0.5
    k = heads(k, GLA_DK)
    v = heads(v, GLA_DV)
    log_a = heads(jax.nn.log_sigmoid((a_lr @ w_a2 + b_a).astype(f32)) / GLA_TAU, GLA_DK)
    bcum = jnp.cumsum(log_a, axis=3)
    qe = q * jnp.exp(bcum)
    ke = k * jnp.exp(-bcum)
    causal = jnp.tril(jnp.ones((C, C), dtype=bool))
    attn = jnp.where(causal, jnp.einsum('bhncd,bhnsd->bhncs', qe, ke), 0.0)
    o_intra = jnp.einsum('bhncs,bhnse->bhnce', attn, v)
    b_last = bcum[:, :, :, -1:, :]
    kd = k * jnp.exp(b_last - bcum)
    decay = jnp.exp(b_last[:, :, :, 0, :])

    def step(state, inp):
        qe_c, kd_c, dec_c, v_c = inp
        o = jnp.einsum('bhcd,bhde->bhce', qe_c, state)
        state = dec_c[..., None] * state + jnp.einsum('bhcd,bhce->bhde', kd_c, v_c)
        return state, o

    xs = (jnp.moveaxis(qe, 2, 0), jnp.moveaxis(kd, 2, 0), jnp.moveaxis(decay, 2, 0), jnp.moveaxis(v, 2, 0))
    _, o_inter = lax.scan(step, jnp.zeros((B, H, GLA_DK, GLA_DV), f32), xs)
    o = (o_intra + jnp.moveaxis(o_inter, 0, 2)).reshape(B, H, S, GLA_DV)
    o = o * lax.rsqrt(jnp.mean(jnp.square(o), axis=-1, keepdims=True) + NORM_EPS) * norm_g.astype(f32)
    return o.transpose(0, 2, 1, 3).reshape(B, S, GLA_WIDTH)


def _banded_attention(q, k, v, n_back, qb):
    lead = q.shape[:-2]
    L, dh = q.shape[-2], q.shape[-1]
    n_prev = -(-n_back // qb)
    nb = -(-L // qb)
    pad = nb * qb - L
    pw = [(0, 0)] * len(lead)
    q = jnp.pad(q, pw + [(0, pad), (0, 0)])
    k = jnp.pad(k, pw + [(n_prev * qb, pad), (0, 0)])
    v = jnp.pad(v, pw + [(n_prev * qb, pad), (0, 0)])
    q_b = q.reshape(*lead, nb, qb, dh)
    k_b = k.reshape(*lead, nb + n_prev, qb, dh)
    v_b = v.reshape(*lead, nb + n_prev, qb, dh)
    k_band = jnp.concatenate([k_b[..., j:j + nb, :, :] for j in range(n_prev + 1)], axis=-2)
    v_band = jnp.concatenate([v_b[..., j:j + nb, :, :] for j in range(n_prev + 1)], axis=-2)
    s = jnp.einsum('...nqd,...nkd->...nqk', q_b, k_band).astype(jnp.float32)
    qpos = (jnp.arange(nb) * qb)[:, None] + jnp.arange(qb)[None, :]
    kpos = (jnp.arange(nb) * qb - n_prev * qb)[:, None] + jnp.arange((n_prev + 1) * qb)[None, :]
    rel = qpos[:, :, None] - kpos[:, None, :]
    mask = (rel >= 0) & (rel <= n_back) & (kpos[:, None, :] >= 0)
    s = jnp.where(mask, s, NEG_INF)
    m = jnp.max(s, axis=-1, keepdims=True)
    e = jnp.exp(s - m)
    den = jnp.sum(e, axis=-1, keepdims=True)
    lse = (m + jnp.log(den))[..., 0]
    o = jnp.einsum('...nqk,...nkd->...nqd', (e / den).astype(v.dtype), v_band)
    o = o.reshape(*lead, nb * qb, dh)[..., :L, :]
    lse = lse.reshape(*lead, nb * qb)[..., :L]
    return o, lse


def _dilated_group(q, k, v, dilation, n_back):
    B, H, S, dh = q.shape
    L = S // dilation

    def to_streams(z):
        return z.reshape(B, H, L, dilation, dh).transpose(0, 1, 3, 2, 4)

    o, lse = _banded_attention(to_streams(q), to_streams(k), to_streams(v), n_back, DIL_QBLOCK)
    o = o.transpose(0, 1, 3, 2, 4).reshape(B, H, S, dh)
    lse = lse.transpose(0, 1, 3, 2).reshape(B, H, S)
    return o, lse


def _dilated_mixer(q, k, v):
    B, S, _ = q.shape
    n_g = len(DIL_GROUPS)
    pos = jnp.arange(S)

    def split(z):
        return z.reshape(B, S, n_g, DIL_HEADS, HEAD_DIM).transpose(2, 0, 3, 1, 4)

    q = _rope(split(q), pos) * HEAD_DIM ** -0.5
    k = _rope(split(k), pos)
    v = split(v)
    outs, lses = [], []
    for g, (window, dilation) in enumerate(DIL_GROUPS):
        o, lse = _dilated_group(q[g], k[g], v[g], dilation, window // dilation)
        outs.append(o)
        lses.append(lse)
    alpha = jax.nn.softmax(jnp.stack(lses, axis=0), axis=0)
    o = sum(alpha[g][..., None].astype(outs[g].dtype) * outs[g] for g in range(n_g))
    return o.transpose(0, 2, 1, 3).reshape(B, S, DIL_WIDTH)


def _even_layer(x, w_in, pe_k, w1_k, w2_k, pe_v, w1_v, w2_v, gla_w_a2, gla_b_a, gla_norm_g, w_out):
    h = x @ w_in
    nsa_q, nsa_kv, nsa_g, nsa_gate, gla_q, gla_k, gla_v, gla_a, gla_gate = _split_columns(h, L0_SPLITS)
    o_nsa = _nsa_mixer(nsa_q, nsa_kv, nsa_g, pe_k, w1_k, w2_k, pe_v, w1_v, w2_v) * jax.nn.silu(nsa_gate)
    o_gla = _gla_mixer(gla_q, gla_k, gla_v, gla_a, gla_w_a2, gla_b_a, gla_norm_g).astype(x.dtype) * jax.nn.silu(gla_gate)
    return jnp.concatenate([o_nsa, o_gla], axis=-1) @ w_out


def _odd_layer(x, w_in, w_out):
    h = x @ w_in
    q, k, v, gate = _split_columns(h, L1_SPLITS)
    return (_dilated_mixer(q, k, v) * jax.nn.silu(gate)) @ w_out


def setup_inputs(seed: int = 0) -> dict:
    key = jax.random.key(seed)
    ks = jax.random.split(key, 18)

    def nrm(k, shape, scale):
        return jax.random.normal(k, shape, jnp.float32) * scale

    cmp_in = NSA_CMP_LEN * HEAD_DIM
    l0_out_in = NSA_WIDTH + GLA_WIDTH
    return {
        'x': nrm(ks[0], (BATCH, SEQ, D_MODEL), 1.0),
        'l0_w_in': nrm(ks[1], (D_MODEL, sum(L0_SPLITS)), D_MODEL ** -0.5),
        'l0_nsa_pe_k': nrm(ks[2], (NSA_CMP_LEN, HEAD_DIM), 0.02),
        'l0_nsa_w1_k': nrm(ks[3], (cmp_in, NSA_CMP_HIDDEN), cmp_in ** -0.5),
        'l0_nsa_w2_k': nrm(ks[4], (NSA_CMP_HIDDEN, HEAD_DIM), NSA_CMP_HIDDEN ** -0.5),
        'l0_nsa_pe_v': nrm(ks[5], (NSA_CMP_LEN, HEAD_DIM), 0.02),
        'l0_nsa_w1_v': nrm(ks[6], (cmp_in, NSA_CMP_HIDDEN), cmp_in ** -0.5),
        'l0_nsa_w2_v': nrm(ks[7], (NSA_CMP_HIDDEN, HEAD_DIM), NSA_CMP_HIDDEN ** -0.5),
        'l0_gla_w_a2': nrm(ks[8], (GLA_RANK, GLA_HEADS * GLA_DK), GLA_RANK ** -0.5),
        'l0_gla_b_a': nrm(ks[9], (GLA_HEADS * GLA_DK,), 0.1),
        'l0_gla_norm_g': 1.0 + nrm(ks[10], (GLA_DV,), 0.02),
        'l0_w_out': nrm(ks[11], (l0_out_in, D_MODEL), l0_out_in ** -0.5 * DEEPNORM_BETA),
        'l0_ln_g': 1.0 + nrm(ks[12], (D_MODEL,), 0.02),
        'l0_ln_b': nrm(ks[13], (D_MODEL,), 0.02),
        'l1_w_in': nrm(ks[14], (D_MODEL, sum(L1_SPLITS)), D_MODEL ** -0.5),
        'l1_w_out': nrm(ks[15], (DIL_WIDTH, D_MODEL), DIL_WIDTH ** -0.5 * DEEPNORM_BETA),
        'l1_ln_g': 1.0 + nrm(ks[16], (D_MODEL,), 0.02),
        'l1_ln_b': nrm(ks[17], (D_MODEL,), 0.02),
    }


def reference(x, l0_w_in, l0_nsa_pe_k, l0_nsa_w1_k, l0_nsa_w2_k, l0_nsa_pe_v, l0_nsa_w1_v, l0_nsa_w2_v,
              l0_gla_w_a2, l0_gla_b_a, l0_gla_norm_g, l0_w_out, l0_ln_g, l0_ln_b,
              l1_w_in, l1_w_out, l1_ln_g, l1_ln_b):
    layer_params = (
        (l0_w_in, l0_nsa_pe_k, l0_nsa_w1_k, l0_nsa_w2_k, l0_nsa_pe_v, l0_nsa_w1_v, l0_nsa_w2_v,
         l0_gla_w_a2, l0_gla_b_a, l0_gla_norm_g, l0_w_out, l0_ln_g, l0_ln_b),
        (l1_w_in, l1_w_out, l1_ln_g, l1_ln_b),
    )
    for layer in range(DEPTH):
        p = layer_params[layer]
        if layer % 2 == 0:
            y = _even_layer(x, *p[:-2])
        else:
            y = _odd_layer(x, *p[:-2])
        x = _layer_norm(DEEPNORM_ALPHA * x + y, p[-2], p[-1])
    return x
```

```python
import functools

import numpy as np
import jax
import jax.numpy as jnp
from jax import lax
from jax.experimental import pallas as pl
from jax.experimental.pallas import tpu as pltpu

F32 = jnp.float32
BF16 = jnp.bfloat16

HEAD_DIM = 128
ROPE_THETA = 10000.0
LN_EPS = 1e-5
NORM_EPS = 1e-6
NEG_INF = -1e30
M_INIT = -1e29

NSA_KV_GROUPS = 4
NSA_HEADS_PER_GROUP = 4
NSA_CMP_LEN = 32
NSA_CMP_STRIDE = 16
NSA_SEL_LEN = 64
NSA_N_SEL = 16
NSA_WINDOW = 512
NSA_FORCE_BONUS = 1e4
NSA_BLK_PAD = 128

GLA_HEADS = 4
GLA_DK = 256
GLA_DV = 512
GLA_RANK = 16
GLA_TAU = 16.0
GLA_CHUNK = 64

DIL_GROUPS = ((128, 1), (512, 4), (2048, 16))
DIL_HEADS = 16
DIL_BACK = 128

DEPTH = 2
DEEPNORM_ALPHA = (2.0 * DEPTH) ** 0.25

VMEM_LIMIT_BYTES = 56 * 1024 * 1024

EPI_PLAIN, EPI_ROPE, EPI_ROPE_QSCALE, EPI_GLA_QSCALE = 0, 1, 2, 3
Q_SCALE = HEAD_DIM ** -0.5
GLA_Q_SCALE = GLA_DK ** -0.5

NT_DIMS = (((1,), (1,)), ((), ()))
TN_DIMS = (((0,), (0,)), ((), ()))


def _params(*sem):
    return pltpu.CompilerParams(dimension_semantics=sem, vmem_limit_bytes=VMEM_LIMIT_BYTES)


def _silu(x):
    return x * (1.0 / (1.0 + jnp.exp(-x)))


def _rope_tables(pos):
    half = HEAD_DIM // 2
    inv_freq = ROPE_THETA ** (-jnp.arange(half, dtype=F32) / half)
    ang = pos.astype(F32)[:, None] * inv_freq[None, :]
    cos = jnp.cos(ang)
    sin = jnp.sin(ang)
    return jnp.concatenate([cos, cos], axis=-1), jnp.concatenate([-sin, sin], axis=-1)


def _rope(a, cos, sin):
    return a * cos + pltpu.roll(a, HEAD_DIM // 2, 1) * sin


def _proj_kernel(kind_ref, a_ref, w_ref, cos_ref, sin_ref, o_ref):
    kind = kind_ref[pl.program_id(1)]
    acc = jnp.dot(a_ref[...], w_ref[...], preferred_element_type=F32)

    @pl.when(kind == EPI_PLAIN)
    def _():
        o_ref[...] = acc.astype(o_ref.dtype)

    @pl.when(kind == EPI_GLA_QSCALE)
    def _():
        o_ref[...] = (acc * GLA_Q_SCALE).astype(o_ref.dtype)

    @pl.when((kind == EPI_ROPE) | (kind == EPI_ROPE_QSCALE))
    def _():
        scale = jnp.where(kind == EPI_ROPE_QSCALE, Q_SCALE, 1.0).astype(F32)
        cos = cos_ref[...]
        sin = sin_ref[...]
        for s in range(acc.shape[1] // HEAD_DIM):
            sl = slice(s * HEAD_DIM, (s + 1) * HEAD_DIM)
            o_ref[:, sl] = (_rope(acc[:, sl], cos, sin) * scale).astype(o_ref.dtype)


def _project(act3, w, kinds, cos, sin, *, d, tm, tn, out_dtype):
    B, L, dK = act3.shape
    K = dK // d
    N = w.shape[1]
    nI, nJ = L // tm, N // tn
    assert L % tm == 0 and N % tn == 0 and len(kinds) == nJ

    def split(m):
        return m // (d * nI), (m // nI) % d, m % nI

    def a_idx(m, j, k):
        b, r, i = split(m)
        return (b, i, r)

    def t_idx(m, j, k):
        b, r, i = split(m)
        return (r, i, 0)

    def o_idx(m, j, k):
        b, r, i = split(m)
        return (b, r, i, j)

    grid_spec = pltpu.PrefetchScalarGridSpec(
        num_scalar_prefetch=1,
        grid=(B * d * nI, nJ),
        in_specs=[
            pl.BlockSpec((None, tm, K), a_idx),
            pl.BlockSpec((K, tn), lambda m, j, k: (0, j)),
            pl.BlockSpec((None, tm, HEAD_DIM), t_idx),
            pl.BlockSpec((None, tm, HEAD_DIM), t_idx),
        ],
        out_specs=pl.BlockSpec((None, None, tm, tn), o_idx),
    )
    return pl.pallas_call(
        _proj_kernel,
        out_shape=jax.ShapeDtypeStruct((B, d, L, N), out_dtype),
        grid_spec=grid_spec,
        compiler_params=_params("parallel", "arbitrary"),
        name=f"proj_d{d}_n{N}",
    )(jnp.asarray(kinds, jnp.int32), act3, w, cos, sin)


def _outproj_kernel(*refs):
    o_ref = refs[-1]
    acc = None
    for a_ref, w_ref in zip(refs[0:-1:2], refs[1:-1:2]):
        part = jnp.dot(a_ref[...], w_ref[...], preferred_element_type=F32)
        acc = part if acc is None else acc + part
    o_ref[...] = acc.astype(o_ref.dtype)


def _out_project(acts, ws, *, tm, tn):
    T = acts[0].shape[0]
    N = ws[0].shape[1]
    in_specs, args = [], []
    for a, w in zip(acts, ws):
        K = a.shape[1]
        in_specs += [pl.BlockSpec((tm, K), lambda i, j: (i, 0)), pl.BlockSpec((K, tn), lambda i, j: (0, j))]
        args += [a, w]
    return pl.pallas_call(
        _outproj_kernel,
        out_shape=jax.ShapeDtypeStruct((T, N), BF16),
        grid=(T // tm, N // tn),
        in_specs=in_specs,
        out_specs=pl.BlockSpec((tm, tn), lambda i, j: (i, j)),
        compiler_params=_params("parallel", "arbitrary"),
        name=f"outproj_{len(acts)}",
    )(*args)


def _ln_kernel(x_ref, y_ref, g_ref, b_ref, o_ref, ob_ref):
    z = DEEPNORM_ALPHA * x_ref[...] + y_ref[...].astype(F32)
    mu = jnp.mean(z, axis=-1, keepdims=True)
    zc = z - mu
    var = jnp.mean(zc * zc, axis=-1, keepdims=True)
    out = zc * lax.rsqrt(var + LN_EPS) * g_ref[...] + b_ref[...]
    o_ref[...] = out
    ob_ref[...] = out.astype(BF16)


def _deepnorm_ln(x2, y2, g, b, *, tr):
    T, D = x2.shape
    row = pl.BlockSpec((tr, D), lambda i: (i, 0))
    vec = pl.BlockSpec((1, D), lambda i: (0, 0))
    return pl.pallas_call(
        _ln_kernel,
        out_shape=(jax.ShapeDtypeStruct((T, D), F32), jax.ShapeDtypeStruct((T, D), BF16)),
        grid=(T // tr,),
        in_specs=[row, row, vec, vec],
        out_specs=(row, row),
        compiler_params=_params("parallel"),
        name="deepnorm_ln",
    )(x2, y2, g.reshape(1, D), b.reshape(1, D))


def _compress_kernel(zk_ref, zv_ref, pe_k_ref, pe_v_ref, w1ka_ref, w1kb_ref, w1va_ref, w1vb_ref,
                     w2k_ref, w2vt_ref, cos_ref, sin_ref, kc_ref, vct_ref, uk_ref, vk_ref, uv_ref, vv_ref):
    p = pl.program_id(1)
    n = uk_ref.shape[0]

    @pl.when(p == 0)
    def _():
        for ref in (uk_ref, vk_ref, uv_ref, vv_ref):
            ref[...] = jnp.zeros_like(ref)

    def accumulate(z_ref, pe_ref, wa_ref, wb_ref, u_ref, v_ref):
        z = z_ref[...].astype(F32)
        za = (z + pe_ref[pl.ds(p, 1), :]).astype(BF16)
        zb = (z + pe_ref[pl.ds(p + NSA_CMP_STRIDE, 1), :]).astype(BF16)
        u_ref[...] += jnp.dot(za, wa_ref[...], preferred_element_type=F32)
        v_ref[...] += jnp.dot(zb, wb_ref[...], preferred_element_type=F32)

    accumulate(zk_ref, pe_k_ref, w1ka_ref, w1kb_ref, uk_ref, vk_ref)
    accumulate(zv_ref, pe_v_ref, w1va_ref, w1vb_ref, uv_ref, vv_ref)

    @pl.when(p == NSA_CMP_STRIDE - 1)
    def _():
        hid_k = _silu(uk_ref[...] + pltpu.roll(vk_ref[...], n - 1, 0)).astype(BF16)
        hid_v = _silu(uv_ref[...] + pltpu.roll(vv_ref[...], n - 1, 0)).astype(BF16)
        kc = jnp.dot(hid_k, w2k_ref[...], preferred_element_type=F32)
        kc = _rope(kc, cos_ref[...], sin_ref[...])
        row = lax.broadcasted_iota(jnp.int32, kc.shape, 0)
        kc_ref[...] = jnp.where(row < n - 1, kc, 0.0).astype(kc_ref.dtype)
        vct = lax.dot_general(w2vt_ref[...], hid_v, NT_DIMS, preferred_element_type=F32)
        col = lax.broadcasted_iota(jnp.int32, vct.shape, 1)
        vct_ref[...] = jnp.where(col < n - 1, vct, 0.0).astype(vct_ref.dtype)


def _nsa_compress(h2, B, S, pe_k, w1_k, w2_k, pe_v, w1_v, w2_v, kc_block, vc_block):
    N = h2.shape[1]
    G = NSA_KV_GROUPS
    n = S // NSA_CMP_STRIDE
    nb = N // HEAD_DIM
    h3 = h2.reshape(B, n, NSA_CMP_STRIDE * N)
    cos, sin = _rope_tables(jnp.arange(n) * NSA_CMP_STRIDE + NSA_CMP_LEN - 1)

    def z_spec(block0):
        return pl.BlockSpec((None, n, HEAD_DIM), lambda bg, p: (bg // G, 0, p * nb + block0 + bg % G))

    full = lambda shape: pl.BlockSpec(shape, lambda bg, p: (0,) * len(shape))
    w1a = pl.BlockSpec((HEAD_DIM, HEAD_DIM), lambda bg, p: (p, 0))
    w1b = pl.BlockSpec((HEAD_DIM, HEAD_DIM), lambda bg, p: (p + NSA_CMP_STRIDE, 0))
    acc = pltpu.VMEM((n, HEAD_DIM), F32)
    return pl.pallas_call(
        _compress_kernel,
        out_shape=(jax.ShapeDtypeStruct((B * G, n, HEAD_DIM), BF16), jax.ShapeDtypeStruct((B * G, HEAD_DIM, n), BF16)),
        grid=(B * G, NSA_CMP_STRIDE),
        in_specs=[z_spec(kc_block), z_spec(vc_block), full((NSA_CMP_LEN, HEAD_DIM)), full((NSA_CMP_LEN, HEAD_DIM)),
                  w1a, w1b, w1a, w1b, full((HEAD_DIM, HEAD_DIM)), full((HEAD_DIM, HEAD_DIM)),
                  full((n, HEAD_DIM)), full((n, HEAD_DIM))],
        out_specs=(pl.BlockSpec((None, n, HEAD_DIM), lambda bg, p: (bg, 0, 0)),
                   pl.BlockSpec((None, HEAD_DIM, n), lambda bg, p: (bg, 0, 0))),
        scratch_shapes=[acc, acc, acc, acc],
        compiler_params=_params("parallel", "arbitrary"),
        name="nsa_compress",
    )(h3, h3, pe_k, pe_v, w1_k.astype(BF16), w1_k.astype(BF16), w1_v.astype(BF16), w1_v.astype(BF16),
      w2_k.astype(BF16), w2_v.T.astype(BF16), cos, sin)


def _split3(x):
    hi = x.astype(BF16)
    r1 = x - hi.astype(F32)
    mid = r1.astype(BF16)
    lo = (r1 - mid.astype(F32)).astype(BF16)
    return hi, mid, lo


def _nsa_kernel(q_ref, kc_ref, vct_ref, ks_ref, vs_ref, kw_ref, vw_ref, gl_ref, gate_ref, ovt_ref, exp_ref,
                o_ref, imp_ref, sel_ref, m_ref, l_ref, acc_ref, *, tq, tk, n_sel):
    g = pl.program_id(1)
    qi = pl.program_id(2)
    t0 = qi * tq
    hg = NSA_HEADS_PER_GROUP
    n_cmp = kc_ref.shape[0]
    nbp = NSA_BLK_PAD
    head = lambda h: slice(h * HEAD_DIM, (h + 1) * HEAD_DIM)

    q4 = q_ref[...]
    qs = jnp.concatenate([q4[:, head(h)] for h in range(hg)], axis=0)

    t_lane = t0 + lax.broadcasted_iota(jnp.int32, (n_cmp, tq), 1)
    cmp_end = lax.broadcasted_iota(jnp.int32, (n_cmp, tq), 0) * NSA_CMP_STRIDE + (NSA_CMP_LEN - 1)
    cmask = cmp_end <= t_lane
    kc = kc_ref[...]
    vct = vct_ref[...]
    psum = jnp.zeros((n_cmp, tq), F32)
    o_c = []
    for h in range(hg):
        s = lax.dot_general(kc, q4[:, head(h)], NT_DIMS, preferred_element_type=F32)
        s = jnp.where(cmask, s, NEG_INF)
        m = jnp.max(s, axis=0, keepdims=True)
        e = jnp.where(cmask, jnp.exp(s - m), 0.0)
        den = jnp.sum(e, axis=0, keepdims=True)
        p = e * jnp.where(den > 0.0, 1.0 / den, 0.0)
        psum = psum + p
        oct_h = jnp.dot(vct, p.astype(BF16), preferred_element_type=F32)
        o_c.append(oct_h.T)

    ovt = ovt_ref[...]
    imp = sum(jnp.dot(ovt, part, preferred_element_type=F32) for part in _split3(psum))
    j_idx = lax.broadcasted_iota(jnp.int32, (nbp, tq), 0)
    cur = (t0 + lax.broadcasted_iota(jnp.int32, (nbp, tq), 1)) >> 6
    forced = (j_idx == 0) | (j_idx == cur) | (j_idx == cur - 1)
    imp = jnp.where(forced, imp + NSA_FORCE_BONUS, imp)
    imp = jnp.where(j_idx <= cur, imp, NEG_INF)
    imp_ref[...] = imp

    def rank_body(i, cnt):
        row = imp_ref[pl.ds(i, 1), :]
        beats = (row > imp) | ((row == imp) & (i < j_idx))
        return cnt + beats.astype(jnp.int32)

    n_live = ((t0 + tq - 1) >> 6) + 1
    rank = lax.fori_loop(0, n_live, rank_body, jnp.zeros((nbp, tq), jnp.int32))
    sel_t = jnp.where((rank < n_sel) & (j_idx <= cur), 1.0, 0.0)
    sel_ref[...] = sel_t.T.astype(BF16)

    m_ref[...] = jnp.full(m_ref.shape, M_INIT, F32)
    l_ref[...] = jnp.zeros(l_ref.shape, F32)
    acc_ref[...] = jnp.zeros(acc_ref.shape, F32)
    tq_rows = t0 + lax.broadcasted_iota(jnp.int32, (tq, tk), 0)
    k_cols = lax.broadcasted_iota(jnp.int32, (tq, tk), 1)

    def sel_body(kt, carry):
        k0 = pl.multiple_of(kt * tk, tk)
        k = ks_ref[pl.ds(k0, tk), :]
        v = vs_ref[pl.ds(k0, tk), :]
        chosen = jnp.dot(sel_ref[...], exp_ref[kt], preferred_element_type=F32)
        allowed = (chosen > 0.5) & (k_cols + k0 <= tq_rows)
        bias = jnp.where(allowed, 0.0, NEG_INF)
        s = lax.dot_general(qs, k, NT_DIMS, preferred_element_type=F32)
        s = s + jnp.concatenate([bias] * hg, axis=0)
        m_old = m_ref[...]
        m_new = jnp.maximum(m_old, jnp.max(s, axis=1, keepdims=True))
        p = jnp.exp(s - m_new)
        alpha = jnp.exp(m_old - m_new)
        l_ref[...] = alpha * l_ref[...] + jnp.sum(p, axis=1, keepdims=True)
        acc_ref[...] = alpha * acc_ref[...] + jnp.dot(p.astype(BF16), v, preferred_element_type=F32)
        m_ref[...] = m_new
        return carry

    lax.fori_loop(0, (t0 + tq + tk - 1) // tk, sel_body, 0)
    l_s = l_ref[...]
    o_s = acc_ref[...] * jnp.where(l_s > 0.0, 1.0 / l_s, 0.0)

    band = NSA_WINDOW + tq
    w0 = pl.multiple_of(jnp.maximum(t0 - NSA_WINDOW, 0), tq)
    kb = kw_ref[pl.ds(w0, band), :]
    vb = vw_ref[pl.ds(w0, band), :]
    rel = (t0 + lax.broadcasted_iota(jnp.int32, (tq, band), 0)) - (w0 + lax.broadcasted_iota(jnp.int32, (tq, band), 1))
    wbias = jnp.where((rel >= 0) & (rel < NSA_WINDOW), 0.0, NEG_INF)
    s = lax.dot_general(qs, kb, NT_DIMS, preferred_element_type=F32) + jnp.concatenate([wbias] * hg, axis=0)
    m = jnp.maximum(jnp.max(s, axis=1, keepdims=True), M_INIT)
    p = jnp.exp(s - m)
    l_w = jnp.sum(p, axis=1, keepdims=True)
    o_w = jnp.dot(p.astype(BF16), vb, preferred_element_type=F32) * jnp.where(l_w > 0.0, 1.0 / l_w, 0.0)

    sig = 1.0 / (1.0 + jnp.exp(-gl_ref[...]))
    lane = lax.broadcasted_iota(jnp.int32, sig.shape, 1)
    gate = gate_ref[...].astype(F32)
    for h in range(hg):
        base = (g * hg + h) * 3
        gc, gs, gw = (jnp.sum(jnp.where(lane == base + c, sig, 0.0), axis=1, keepdims=True) for c in range(3))
        rows = slice(h * tq, (h + 1) * tq)
        o = gc * o_c[h] + gs * o_s[rows] + gw * o_w[rows]
        o_ref[:, head(h)] = (o * _silu(gate[:, head(h)])).astype(o_ref.dtype)


def _nsa_attention(h2, small, kc, vct, B, S, *, q_tile0, kv_block0, gate_tile0, tq, tk):
    T = h2.shape[0]
    G, hg = NSA_KV_GROUPS, NSA_HEADS_PER_GROUP
    n_cmp = S // NSA_CMP_STRIDE
    n_blk = S // NSA_SEL_LEN
    nQ = S // tq
    gw = hg * HEAD_DIM
    assert n_blk <= NSA_BLK_PAD and S % tk == 0 and S >= NSA_WINDOW + tq and tq % NSA_SEL_LEN == 0

    nn = np.arange(n_cmp)[None, :]
    jj = np.arange(NSA_BLK_PAD)[:, None]
    ovt = ((nn * NSA_CMP_STRIDE + NSA_CMP_LEN - 1 >= jj * NSA_SEL_LEN) & (nn * NSA_CMP_STRIDE < (jj + 1) * NSA_SEL_LEN)
           & (nn < n_cmp - 1) & (jj < n_blk))
    keys = np.arange(S).reshape(S // tk, 1, tk)
    expand = (keys // NSA_SEL_LEN) == np.arange(NSA_BLK_PAD)[None, :, None]

    def kv_spec(c):
        return pl.BlockSpec((S, HEAD_DIM), lambda b, g, i: (b, kv_block0 + 4 * c + g))

    rows = lambda width, col: pl.BlockSpec((tq, width), lambda b, g, i: (b * nQ + i, col(g)))
    per_bg = lambda shape: pl.BlockSpec((None,) + shape, lambda b, g, i: (b * G + g, 0, 0))
    const = lambda shape: pl.BlockSpec(shape, lambda b, g, i: (0,) * len(shape))
    return pl.pallas_call(
        functools.partial(_nsa_kernel, tq=tq, tk=tk, n_sel=min(NSA_N_SEL, n_blk)),
        out_shape=jax.ShapeDtypeStruct((T, G * gw), BF16),
        grid=(B, G, nQ),
        in_specs=[rows(gw, lambda g: q_tile0 + g), per_bg((n_cmp, HEAD_DIM)), per_bg((HEAD_DIM, n_cmp)),
                  kv_spec(2), kv_spec(3), kv_spec(4), kv_spec(5),
                  rows(HEAD_DIM, lambda g: 0), rows(gw, lambda g: gate_tile0 + g),
                  const((NSA_BLK_PAD, n_cmp)), const((S // tk, NSA_BLK_PAD, tk))],
        out_specs=rows(gw, lambda g: g),
        scratch_shapes=[pltpu.VMEM((NSA_BLK_PAD, tq), F32), pltpu.VMEM((tq, NSA_BLK_PAD), BF16),
                        pltpu.VMEM((hg * tq, 1), F32), pltpu.VMEM((hg * tq, 1), F32),
                        pltpu.VMEM((hg * tq, HEAD_DIM), F32)],
        compiler_params=_params("parallel", "parallel", "arbitrary"),
        name="nsa_attention",
    )(h2, kc, vct, h2, h2, h2, h2, small, h2, jnp.asarray(ovt, BF16), jnp.asarray(expand, BF16))


def _gla_kernel(q_ref, k_ref, v_ref, a_ref, gate_ref, wa_ref, ba_ref, ng_ref, o_ref, state_ref, *, n_chunks):
    C = GLA_CHUNK

    @pl.when(pl.program_id(2) == 0)
    def _():
        state_ref[...] = jnp.zeros_like(state_ref)

    r_i = lax.broadcasted_iota(jnp.int32, (C, C), 0)
    c_i = lax.broadcasted_iota(jnp.int32, (C, C), 1)
    causal = r_i >= c_i
    tri = jnp.where(causal, 1.0, 0.0).astype(BF16)
    wa = wa_ref[...]
    ba = ba_ref[...]
    ng = ng_ref[...]
    for c in range(n_chunks):
        rows = slice(c * C, (c + 1) * C)
        q = q_ref[rows, :].astype(F32)
        k = k_ref[rows, :].astype(F32)
        v = v_ref[rows, :]
        pre = jnp.dot(a_ref[rows, :], wa, preferred_element_type=F32) + ba
        log_a = (jnp.minimum(pre, 0.0) - jnp.log(1.0 + jnp.exp(-jnp.abs(pre)))) * (1.0 / GLA_TAU)
        bcum = sum(jnp.dot(tri, part, preferred_element_type=F32) for part in _split3(log_a))
        b_last = bcum[C - 1:C, :]
        qe = (q * jnp.exp(bcum)).astype(BF16)
        ke = (k * jnp.exp(-bcum)).astype(BF16)
        kd = (k * jnp.exp(b_last - bcum)).astype(BF16)
        attn = lax.dot_general(qe, ke, NT_DIMS, preferred_element_type=F32)
        attn = jnp.where(causal, attn, 0.0).astype(BF16)
        state_t = state_ref[...]
        o = jnp.dot(attn, v, preferred_element_type=F32)
        o = o + lax.dot_general(qe, state_t.astype(BF16), NT_DIMS, preferred_element_type=F32)
        state_ref[...] = state_t * jnp.exp(b_last) + lax.dot_general(v, kd, TN_DIMS, preferred_element_type=F32)
        o = o * lax.rsqrt(jnp.mean(o * o, axis=-1, keepdims=True) + NORM_EPS) * ng
        o_ref[rows, :] = (o * _silu(gate_ref[rows, :].astype(F32))).astype(o_ref.dtype)


def _gla(h2, small, w_a2, b_a, norm_g, B, S, *, q_blk0, k_blk0, v_blk0, gate_blk0, rows):
    T = h2.shape[0]
    H = GLA_HEADS
    nR = S // rows
    a_lo = 3 * NSA_KV_GROUPS * NSA_HEADS_PER_GROUP
    wa = jnp.zeros((H, HEAD_DIM, GLA_DK), F32).at[:, a_lo:a_lo + GLA_RANK, :].set(
        w_a2.reshape(GLA_RANK, H, GLA_DK).transpose(1, 0, 2))
    blk = lambda width, blk0: pl.BlockSpec((rows, width), lambda b, h, i: (b * nR + i, blk0 + h))
    per_head = lambda shape: pl.BlockSpec((None,) + shape, lambda b, h, i: (h, 0, 0))
    return pl.pallas_call(
        functools.partial(_gla_kernel, n_chunks=rows // GLA_CHUNK),
        out_shape=jax.ShapeDtypeStruct((T, H * GLA_DV), BF16),
        grid=(B, H, nR),
        in_specs=[blk(GLA_DK, q_blk0), blk(GLA_DK, k_blk0), blk(GLA_DV, v_blk0),
                  pl.BlockSpec((rows, HEAD_DIM), lambda b, h, i: (b * nR + i, 0)), blk(GLA_DV, gate_blk0),
                  per_head((HEAD_DIM, GLA_DK)), per_head((1, GLA_DK)),
                  pl.BlockSpec((1, GLA_DV), lambda b, h, i: (0, 0))],
        out_specs=blk(GLA_DV, 0),
        scratch_shapes=[pltpu.VMEM((GLA_DV, GLA_DK), F32)],
        compiler_params=_params("parallel", "parallel", "arbitrary"),
        name="gla",
    )(h2, h2, h2, small, h2, wa, b_a.reshape(H, 1, GLA_DK), norm_g.reshape(1, GLA_DV))


def _dilated_kernel(*refs, has_prev, is_last):
    q_ref, kp_ref, kc_ref, vp_ref, vc_ref = refs[:5]
    rest = list(refs[5:])
    po_ref, pl_ref = (rest.pop(0), rest.pop(0)) if has_prev else (None, None)
    gate_ref = rest.pop(0) if is_last else None
    o_ref = rest.pop(0)
    lse_ref = None if is_last else rest.pop(0)
    tq = q_ref.shape[0]
    i = pl.program_id(2)
    row = lax.broadcasted_iota(jnp.int32, (tq, 2 * tq), 0)
    col = lax.broadcasted_iota(jnp.int32, (tq, 2 * tq), 1)
    valid = (col >= row + (tq - DIL_BACK)) & (col <= row + tq) & ((col >= tq) | (i > 0))
    if lse_ref is not None:
        lse_ref[...] = jnp.zeros(lse_ref.shape, F32)
    for h in range(DIL_HEADS):
        sl = slice(h * HEAD_DIM, (h + 1) * HEAD_DIM)
        k = jnp.concatenate([kp_ref[:, sl], kc_ref[:, sl]], axis=0)
        v = jnp.concatenate([vp_ref[:, sl], vc_ref[:, sl]], axis=0)
        s = lax.dot_general(q_ref[:, sl], k, NT_DIMS, preferred_element_type=F32)
        s = jnp.where(valid, s, NEG_INF)
        m = jnp.max(s, axis=1, keepdims=True)
        e = jnp.exp(s - m)
        den = jnp.sum(e, axis=1, keepdims=True)
        o = jnp.dot(e.astype(BF16), v, preferred_element_type=F32) * (1.0 / den)
        lse = m + jnp.log(den)
        if has_prev:
            lse_p = pl_ref[:, h:h + 1]
            lse_n = jnp.maximum(lse_p, lse) + jnp.log(1.0 + jnp.exp(-jnp.abs(lse_p - lse)))
            o = jnp.exp(lse_p - lse_n) * po_ref[:, sl].astype(F32) + jnp.exp(lse - lse_n) * o
            lse = lse_n
        if is_last:
            o = o * _silu(gate_ref[:, sl].astype(F32))
        else:
            lse_ref[:, h:h + 1] = lse
        o_ref[:, sl] = o.astype(o_ref.dtype)


def _dilated_group(hg, prev_o, prev_lse, gate, *, d, tq, is_last):
    B, _, L, W3 = hg.shape
    W = W3 // 3
    has_prev = prev_o is not None
    assert tq == DIL_BACK and L % tq == 0
    cur = lambda col: pl.BlockSpec((None, None, tq, W), lambda b, r, i: (b, r, i, col))
    prv = lambda col: pl.BlockSpec((None, None, tq, W), lambda b, r, i: (b, r, jnp.maximum(i - 1, 0), col))
    tok = lambda width: pl.BlockSpec((None, tq, width), lambda b, r, i: (b, i, r))
    in_specs = [cur(0), prv(1), cur(1), prv(2), cur(2)]
    args = [hg] * 5
    if has_prev:
        in_specs += [tok(W), tok(HEAD_DIM)]
        args += [prev_o.reshape(B, L, d * W), prev_lse.reshape(B, L, d * HEAD_DIM)]
    if is_last:
        in_specs += [tok(W)]
        args += [gate.reshape(B, L, d * W)]
    out_shape = [jax.ShapeDtypeStruct((B, L, d * W), BF16)]
    out_specs = [tok(W)]
    if not is_last:
        out_shape += [jax.ShapeDtypeStruct((B, L, d * HEAD_DIM), F32)]
        out_specs += [tok(HEAD_DIM)]
    res = pl.pallas_call(
        functools.partial(_dilated_kernel, has_prev=has_prev, is_last=is_last),
        out_shape=tuple(out_shape),
        grid=(B, d, L // tq),
        in_specs=in_specs,
        out_specs=tuple(out_specs),
        compiler_params=_params("parallel", "parallel", "arbitrary"),
        name=f"dilated_d{d}",
    )(*args)
    o = res[0].reshape(B, L * d, W)
    return (o, None) if is_last else (o, res[1].reshape(B, L * d, HEAD_DIM))


def _row_tile(n, target):
    t = min(n, target)
    assert n % t == 0
    return t


def _even_layer(xb3, w_in, pe_k, w1_k, w2_k, pe_v, w1_v, w2_v, gla_w_a2, gla_b_a, gla_norm_g, w_out):
    B, S, D = xb3.shape
    T = B * S
    nq = NSA_KV_GROUPS * NSA_HEADS_PER_GROUP * HEAD_DIM
    nkv = 6 * NSA_KV_GROUPS * HEAD_DIM
    ng = 3 * NSA_KV_GROUPS * NSA_HEADS_PER_GROUP
    gk = GLA_HEADS * GLA_DK
    gv = GLA_HEADS * GLA_DV
    o_g = nq + nkv
    o_a = o_g + ng + nq + 2 * gk + gv
    w_main = jnp.concatenate([w_in[:, :o_g], w_in[:, o_g + ng:o_a], w_in[:, o_a + GLA_RANK:]], axis=1).astype(BF16)
    w_small = jnp.concatenate([w_in[:, o_g:o_g + ng], w_in[:, o_a:o_a + GLA_RANK],
                               jnp.zeros((D, HEAD_DIM - ng - GLA_RANK), w_in.dtype)], axis=1).astype(BF16)
    tn = 512
    kinds = ([EPI_ROPE_QSCALE] * (nq // tn)
             + [EPI_PLAIN, EPI_PLAIN, EPI_ROPE, EPI_PLAIN, EPI_ROPE, EPI_PLAIN]
             + [EPI_PLAIN] * (nq // tn) + [EPI_GLA_QSCALE] * (gk // tn) + [EPI_PLAIN] * (gk // tn)
             + [EPI_PLAIN] * (gv // tn) + [EPI_PLAIN] * (gv // tn))
    cos, sin = _rope_tables(jnp.arange(S))
    cos, sin = cos[None], sin[None]
    tm = _row_tile(S, 1024)
    h2 = _project(xb3, w_main, kinds, cos, sin, d=1, tm=tm, tn=tn, out_dtype=BF16).reshape(T, -1)
    small = _project(xb3, w_small, [EPI_PLAIN], cos, sin, d=1, tm=tm, tn=HEAD_DIM, out_dtype=F32).reshape(T, HEAD_DIM)

    q_tile0 = 0
    kv_block0 = nq // HEAD_DIM
    gate_tile0 = (nq + nkv) // tn
    kc, vct = _nsa_compress(h2, B, S, pe_k, w1_k, w2_k, pe_v, w1_v, w2_v, kv_block0, kv_block0 + NSA_KV_GROUPS)
    o_nsa = _nsa_attention(h2, small, kc, vct, B, S, q_tile0=q_tile0, kv_block0=kv_block0, gate_tile0=gate_tile0,
                           tq=128, tk=_row_tile(S, 512))
    c0 = nq + nkv + nq
    o_gla = _gla(h2, small, gla_w_a2, gla_b_a, gla_norm_g, B, S,
                 q_blk0=c0 // GLA_DK, k_blk0=(c0 + gk) // GLA_DK, v_blk0=(c0 + 2 * gk) // GLA_DV,
                 gate_blk0=(c0 + 2 * gk + gv) // GLA_DV, rows=_row_tile(S, 512))
    w_out = w_out.astype(BF16)
    return _out_project([o_nsa, o_gla], [w_out[:nq], w_out[nq:]], tm=_row_tile(T, 1024), tn=512)


def _odd_layer(xb3, w_in, w_out):
    B, S, D = xb3.shape
    T = B * S
    n_g = len(DIL_GROUPS)
    W = DIL_HEADS * HEAD_DIM
    tn = 512
    cos, sin = _rope_tables(jnp.arange(S))
    gate = _project(xb3, w_in[:, 3 * n_g * W:].astype(BF16), [EPI_PLAIN] * (W // tn), cos[None], sin[None],
                    d=1, tm=_row_tile(S, 1024), tn=tn, out_dtype=BF16).reshape(B, S, W)
    kinds = [EPI_ROPE_QSCALE] * (W // tn) + [EPI_ROPE] * (W // tn) + [EPI_PLAIN] * (W // tn)
    o, lse = None, None
    for gi, (window, d) in enumerate(DIL_GROUPS):
        assert window // d == DIL_BACK
        L = S // d
        w_g = jnp.concatenate([w_in[:, (c * n_g + gi) * W:(c * n_g + gi + 1) * W] for c in range(3)], axis=1).astype(BF16)
        to_streams = lambda t: t.reshape(L, d, HEAD_DIM).transpose(1, 0, 2)
        hg = _project(xb3.reshape(B, L, d * D), w_g, kinds, to_streams(cos), to_streams(sin),
                      d=d, tm=_row_tile(L, 1024), tn=tn, out_dtype=BF16)
        o, lse = _dilated_group(hg, o, lse, gate, d=d, tq=DIL_BACK, is_last=gi == n_g - 1)
    return _out_project([o.reshape(T, W)], [w_out.astype(BF16)], tm=_row_tile(T, 1024), tn=512)


def kernel(x, l0_w_in, l0_nsa_pe_k, l0_nsa_w1_k, l0_nsa_w2_k, l0_nsa_pe_v, l0_nsa_w1_v, l0_nsa_w2_v, l0_gla_w_a2, l0_gla_b_a, l0_gla_norm_g, l0_w_out, l0_ln_g, l0_ln_b, l1_w_in, l1_w_out, l1_ln_g, l1_ln_b):
    B, S, D = x.shape
    T = B * S
    x2 = x.reshape(T, D)
    tr = _row_tile(T, 256)
    y0 = _even_layer(x.astype(BF16), l0_w_in, l0_nsa_pe_k, l0_nsa_w1_k, l0_nsa_w2_k, l0_nsa_pe_v, l0_nsa_w1_v,
                     l0_nsa_w2_v, l0_gla_w_a2, l0_gla_b_a, l0_gla_norm_g, l0_w_out)
    x1, x1b = _deepnorm_ln(x2, y0, l0_ln_g, l0_ln_b, tr=tr)
    y1 = _odd_layer(x1b.reshape(B, S, D), l1_w_in, l1_w_out)
    out, _ = _deepnorm_ln(x1, y1, l1_ln_g, l1_ln_b, tr=tr)
    return out.reshape(B, S, D)
```

```python
import functools

import numpy as np
import jax
import jax.numpy as jnp
from jax import lax
from jax.experimental import pallas as pl
from jax.experimental.pallas import tpu as pltpu

F32 = jnp.float32
BF16 = jnp.bfloat16

HEAD_DIM = 128
ROPE_THETA = 10000.0
LN_EPS = 1e-5
NORM_EPS = 1e-6
NEG_INF = -1e30
M_INIT = -1e29
TAKEN = -3e38
SEL_UNROLL = 2

NSA_KV_GROUPS = 4
NSA_HEADS_PER_GROUP = 4
NSA_CMP_LEN = 32
NSA_CMP_STRIDE = 16
NSA_SEL_LEN = 64
NSA_N_SEL = 16
NSA_WINDOW = 512
NSA_FORCE_BONUS = 1e4
NSA_BLK_PAD = 128

GLA_HEADS = 4
GLA_DK = 256
GLA_DV = 512
GLA_RANK = 16
GLA_TAU = 16.0
GLA_CHUNK = 64

DIL_GROUPS = ((128, 1), (512, 4), (2048, 16))
DIL_HEADS = 16
DIL_BACK = 128

DEPTH = 2
DEEPNORM_ALPHA = (2.0 * DEPTH) ** 0.25

VMEM_LIMIT_BYTES = 56 * 1024 * 1024

EPI_PLAIN, EPI_ROPE, EPI_ROPE_QSCALE, EPI_GLA_QSCALE = 0, 1, 2, 3
Q_SCALE = HEAD_DIM ** -0.5
GLA_Q_SCALE = GLA_DK ** -0.5

NT_DIMS = (((1,), (1,)), ((), ()))
TN_DIMS = (((0,), (0,)), ((), ()))


def _params(*sem):
    return pltpu.CompilerParams(dimension_semantics=sem, vmem_limit_bytes=VMEM_LIMIT_BYTES)


def _silu(x):
    return x * (1.0 / (1.0 + jnp.exp(-x)))


def _rope_tables(pos):
    half = HEAD_DIM // 2
    inv_freq = ROPE_THETA ** (-jnp.arange(half, dtype=F32) / half)
    ang = pos.astype(F32)[:, None] * inv_freq[None, :]
    cos = jnp.cos(ang)
    sin = jnp.sin(ang)
    return jnp.concatenate([cos, cos], axis=-1), jnp.concatenate([-sin, sin], axis=-1)


def _rope(a, cos, sin):
    return a * cos + pltpu.roll(a, HEAD_DIM // 2, 1) * sin


def _proj_kernel(kind_ref, a_ref, w_ref, cos_ref, sin_ref, o_ref):
    kind = kind_ref[pl.program_id(1)]
    acc = jnp.dot(a_ref[...], w_ref[...], preferred_element_type=F32)

    @pl.when(kind == EPI_PLAIN)
    def _():
        o_ref[...] = acc.astype(o_ref.dtype)

    @pl.when(kind == EPI_GLA_QSCALE)
    def _():
        o_ref[...] = (acc * GLA_Q_SCALE).astype(o_ref.dtype)

    @pl.when((kind == EPI_ROPE) | (kind == EPI_ROPE_QSCALE))
    def _():
        scale = jnp.where(kind == EPI_ROPE_QSCALE, Q_SCALE, 1.0).astype(F32)
        cos = cos_ref[...]
        sin = sin_ref[...]
        for s in range(acc.shape[1] // HEAD_DIM):
            sl = slice(s * HEAD_DIM, (s + 1) * HEAD_DIM)
            o_ref[:, sl] = (_rope(acc[:, sl], cos, sin) * scale).astype(o_ref.dtype)


def _project(act3, w, kinds, cos, sin, *, d, tm, tn, out_dtype):
    B, L, dK = act3.shape
    K = dK // d
    N = w.shape[1]
    nI, nJ = L // tm, N // tn
    assert L % tm == 0 and N % tn == 0 and len(kinds) == nJ

    def split(m):
        return m // (d * nI), (m // nI) % d, m % nI

    def a_idx(m, j, k):
        b, r, i = split(m)
        return (b, i, r)

    def t_idx(m, j, k):
        b, r, i = split(m)
        return (r, i, 0)

    def o_idx(m, j, k):
        b, r, i = split(m)
        return (b, r, i, j)

    grid_spec = pltpu.PrefetchScalarGridSpec(
        num_scalar_prefetch=1,
        grid=(B * d * nI, nJ),
        in_specs=[
            pl.BlockSpec((None, tm, K), a_idx),
            pl.BlockSpec((K, tn), lambda m, j, k: (0, j)),
            pl.BlockSpec((None, tm, HEAD_DIM), t_idx),
            pl.BlockSpec((None, tm, HEAD_DIM), t_idx),
        ],
        out_specs=pl.BlockSpec((None, None, tm, tn), o_idx),
    )
    return pl.pallas_call(
        _proj_kernel,
        out_shape=jax.ShapeDtypeStruct((B, d, L, N), out_dtype),
        grid_spec=grid_spec,
        compiler_params=_params("parallel", "arbitrary"),
        name=f"proj_d{d}_n{N}",
    )(jnp.asarray(kinds, jnp.int32), act3, w, cos, sin)


def _outproj_kernel(*refs):
    o_ref = refs[-1]
    acc = None
    for a_ref, w_ref in zip(refs[0:-1:2], refs[1:-1:2]):
        part = jnp.dot(a_ref[...], w_ref[...], preferred_element_type=F32)
        acc = part if acc is None else acc + part
    o_ref[...] = acc.astype(o_ref.dtype)


def _out_project(acts, ws, *, tm, tn):
    T = acts[0].shape[0]
    N = ws[0].shape[1]
    in_specs, args = [], []
    for a, w in zip(acts, ws):
        K = a.shape[1]
        in_specs += [pl.BlockSpec((tm, K), lambda i, j: (i, 0)), pl.BlockSpec((K, tn), lambda i, j: (0, j))]
        args += [a, w]
    return pl.pallas_call(
        _outproj_kernel,
        out_shape=jax.ShapeDtypeStruct((T, N), BF16),
        grid=(T // tm, N // tn),
        in_specs=in_specs,
        out_specs=pl.BlockSpec((tm, tn), lambda i, j: (i, j)),
        compiler_params=_params("parallel", "arbitrary"),
        name=f"outproj_{len(acts)}",
    )(*args)


def _ln_kernel(x_ref, y_ref, g_ref, b_ref, o_ref, ob_ref):
    z = DEEPNORM_ALPHA * x_ref[...] + y_ref[...].astype(F32)
    mu = jnp.mean(z, axis=-1, keepdims=True)
    zc = z - mu
    var = jnp.mean(zc * zc, axis=-1, keepdims=True)
    out = zc * lax.rsqrt(var + LN_EPS) * g_ref[...] + b_ref[...]
    o_ref[...] = out
    ob_ref[...] = out.astype(BF16)


def _deepnorm_ln(x2, y2, g, b, *, tr):
    T, D = x2.shape
    row = pl.BlockSpec((tr, D), lambda i: (i, 0))
    vec = pl.BlockSpec((1, D), lambda i: (0, 0))
    return pl.pallas_call(
        _ln_kernel,
        out_shape=(jax.ShapeDtypeStruct((T, D), F32), jax.ShapeDtypeStruct((T, D), BF16)),
        grid=(T // tr,),
        in_specs=[row, row, vec, vec],
        out_specs=(row, row),
        compiler_params=_params("parallel"),
        name="deepnorm_ln",
    )(x2, y2, g.reshape(1, D), b.reshape(1, D))


def _compress_kernel(zk_ref, zv_ref, pe_k_ref, pe_v_ref, w1ka_ref, w1kb_ref, w1va_ref, w1vb_ref,
                     w2k_ref, w2vt_ref, cos_ref, sin_ref, kc_ref, vct_ref, uk_ref, vk_ref, uv_ref, vv_ref):
    p = pl.program_id(1)
    n = uk_ref.shape[0]

    @pl.when(p == 0)
    def _():
        for ref in (uk_ref, vk_ref, uv_ref, vv_ref):
            ref[...] = jnp.zeros_like(ref)

    def accumulate(z_ref, pe_ref, wa_ref, wb_ref, u_ref, v_ref):
        z = z_ref[...].astype(F32)
        za = (z + pe_ref[pl.ds(p, 1), :]).astype(BF16)
        zb = (z + pe_ref[pl.ds(p + NSA_CMP_STRIDE, 1), :]).astype(BF16)
        u_ref[...] += jnp.dot(za, wa_ref[...], preferred_element_type=F32)
        v_ref[...] += jnp.dot(zb, wb_ref[...], preferred_element_type=F32)

    accumulate(zk_ref, pe_k_ref, w1ka_ref, w1kb_ref, uk_ref, vk_ref)
    accumulate(zv_ref, pe_v_ref, w1va_ref, w1vb_ref, uv_ref, vv_ref)

    @pl.when(p == NSA_CMP_STRIDE - 1)
    def _():
        hid_k = _silu(uk_ref[...] + pltpu.roll(vk_ref[...], n - 1, 0)).astype(BF16)
        hid_v = _silu(uv_ref[...] + pltpu.roll(vv_ref[...], n - 1, 0)).astype(BF16)
        kc = jnp.dot(hid_k, w2k_ref[...], preferred_element_type=F32)
        kc = _rope(kc, cos_ref[...], sin_ref[...])
        row = lax.broadcasted_iota(jnp.int32, kc.shape, 0)
        kc_ref[...] = jnp.where(row < n - 1, kc, 0.0).astype(kc_ref.dtype)
        vct = lax.dot_general(w2vt_ref[...], hid_v, NT_DIMS, preferred_element_type=F32)
        col = lax.broadcasted_iota(jnp.int32, vct.shape, 1)
        vct_ref[...] = jnp.where(col < n - 1, vct, 0.0).astype(vct_ref.dtype)


def _nsa_compress(h2, B, S, pe_k, w1_k, w2_k, pe_v, w1_v, w2_v, kc_block, vc_block):
    G = NSA_KV_GROUPS
    n = S // NSA_CMP_STRIDE
    c0, c1 = kc_block * HEAD_DIM, (vc_block + G) * HEAD_DIM
    nb = (c1 - c0) // HEAD_DIM
    h3 = h2[:, c0:c1].reshape(B, n, NSA_CMP_STRIDE * (c1 - c0))
    kc_block, vc_block = 0, vc_block - kc_block
    cos, sin = _rope_tables(jnp.arange(n) * NSA_CMP_STRIDE + NSA_CMP_LEN - 1)

    def z_spec(block0):
        return pl.BlockSpec((None, n, HEAD_DIM), lambda bg, p: (bg // G, 0, p * nb + block0 + bg % G))

    full = lambda shape: pl.BlockSpec(shape, lambda bg, p: (0,) * len(shape))
    w1a = pl.BlockSpec((HEAD_DIM, HEAD_DIM), lambda bg, p: (p, 0))
    w1b = pl.BlockSpec((HEAD_DIM, HEAD_DIM), lambda bg, p: (p + NSA_CMP_STRIDE, 0))
    acc = pltpu.VMEM((n, HEAD_DIM), F32)
    return pl.pallas_call(
        _compress_kernel,
        out_shape=(jax.ShapeDtypeStruct((B * G, n, HEAD_DIM), BF16), jax.ShapeDtypeStruct((B * G, HEAD_DIM, n), BF16)),
        grid=(B * G, NSA_CMP_STRIDE),
        in_specs=[z_spec(kc_block), z_spec(vc_block), full((NSA_CMP_LEN, HEAD_DIM)), full((NSA_CMP_LEN, HEAD_DIM)),
                  w1a, w1b, w1a, w1b, full((HEAD_DIM, HEAD_DIM)), full((HEAD_DIM, HEAD_DIM)),
                  full((n, HEAD_DIM)), full((n, HEAD_DIM))],
        out_specs=(pl.BlockSpec((None, n, HEAD_DIM), lambda bg, p: (bg, 0, 0)),
                   pl.BlockSpec((None, HEAD_DIM, n), lambda bg, p: (bg, 0, 0))),
        scratch_shapes=[acc, acc, acc, acc],
        compiler_params=_params("parallel", "arbitrary"),
        name="nsa_compress",
    )(h3, h3, pe_k, pe_v, w1_k.astype(BF16), w1_k.astype(BF16), w1_v.astype(BF16), w1_v.astype(BF16),
      w2_k.astype(BF16), w2_v.T.astype(BF16), cos, sin)


def _split3(x):
    hi = x.astype(BF16)
    r1 = x - hi.astype(F32)
    mid = r1.astype(BF16)
    lo = (r1 - mid.astype(F32)).astype(BF16)
    return hi, mid, lo


def _vt_kernel(blk_ref, v_ref, o_ref):
    for c in range(o_ref.shape[0]):
        o_ref[c] = v_ref[c * HEAD_DIM:(c + 1) * HEAD_DIM, :].astype(F32).T.astype(o_ref.dtype)


def _transpose_values(h2, B, S, blocks, *, rows):
    ns = S // rows
    per = rows // HEAD_DIM
    nv = len(blocks)
    grid_spec = pltpu.PrefetchScalarGridSpec(
        num_scalar_prefetch=1,
        grid=(B, nv, ns),
        in_specs=[pl.BlockSpec((rows, HEAD_DIM), lambda b, i, s, blk: (b * ns + s, blk[i]))],
        out_specs=pl.BlockSpec((None, None, per, HEAD_DIM, HEAD_DIM), lambda b, i, s, blk: (b, i, s, 0, 0)),
    )
    return pl.pallas_call(
        _vt_kernel,
        out_shape=jax.ShapeDtypeStruct((B, nv, S // HEAD_DIM, HEAD_DIM, HEAD_DIM), BF16),
        grid_spec=grid_spec,
        compiler_params=_params("parallel", "parallel", "arbitrary"),
        name="transpose_values",
    )(jnp.asarray(blocks, jnp.int32), h2)


def _nsa_kernel(q_ref, kc_ref, vct_ref, ks_ref, vst_ref, kw_ref, vwt_ref, gl_ref, gate_ref, ovt_ref,
                o_ref, sel_ref, sig_ref, m_ref, l_ref, acc_ref, *, tq, tk, n_sel):
    g = pl.program_id(1)
    qi = pl.program_id(2)
    t0 = qi * tq
    hg = NSA_HEADS_PER_GROUP
    n_cmp = kc_ref.shape[0]
    nbp = NSA_BLK_PAD
    head = lambda h: slice(h * HEAD_DIM, (h + 1) * HEAD_DIM)
    lanes = lambda h: slice(h * tq, (h + 1) * tq)
    per_head = lambda a: jnp.concatenate([a] * hg, axis=1)
    chunk = HEAD_DIM

    q4 = q_ref[...]
    qs = jnp.concatenate([q4[:, head(h)] for h in range(hg)], axis=0)

    def softmax_rows(s, m_floor):
        m = jnp.maximum(jnp.max(s, axis=0, keepdims=True), m_floor)
        e = jnp.exp(s - m)
        den = jnp.sum(e, axis=0, keepdims=True)
        return e, jnp.where(den > 0.0, 1.0 / den, 0.0)

    t_lane = t0 + lax.broadcasted_iota(jnp.int32, (n_cmp, tq), 1)
    cmp_end = lax.broadcasted_iota(jnp.int32, (n_cmp, tq), 0) * NSA_CMP_STRIDE + (NSA_CMP_LEN - 1)
    cbias = per_head(jnp.where(cmp_end <= t_lane, 0.0, NEG_INF))
    s = lax.dot_general(kc_ref[...], qs, NT_DIMS, preferred_element_type=F32) + cbias
    e, inv = softmax_rows(s, M_INIT)
    p = e * inv
    psum = sum(p[:, lanes(h)] for h in range(hg))
    o_ct = jnp.dot(vct_ref[...], p.astype(BF16), preferred_element_type=F32)

    ovt = ovt_ref[...]
    imp = sum(jnp.dot(ovt, part, preferred_element_type=F32) for part in _split3(psum))
    j_idx = lax.broadcasted_iota(jnp.int32, (nbp, tq), 0)
    cur = (t0 + lax.broadcasted_iota(jnp.int32, (nbp, tq), 1)) >> 6
    forced = (j_idx == 0) | (j_idx == cur) | (j_idx == cur - 1)
    imp = jnp.where(forced, imp + NSA_FORCE_BONUS, imp)
    imp = jnp.where(j_idx <= cur, imp, NEG_INF)

    j_f = j_idx.astype(F32)
    picked = jnp.zeros((nbp, tq), F32)
    for _ in range(n_sel):
        top = jnp.max(imp, axis=0, keepdims=True)
        first = jnp.min(jnp.where(imp == top, j_f, float(nbp)), axis=0, keepdims=True)
        hit = j_f == first
        picked = jnp.where(hit, 1.0, picked)
        imp = jnp.where(hit, TAKEN, imp)
    sel_ref[...] = jnp.where(j_idx <= cur, picked, 0.0)

    m_ref[...] = jnp.full(m_ref.shape, M_INIT, F32)
    l_ref[...] = jnp.zeros(l_ref.shape, F32)
    acc_ref[...] = jnp.zeros(acc_ref.shape, F32)
    key_i = lax.broadcasted_iota(jnp.int32, (tk, tq), 0)
    t_i = t0 + lax.broadcasted_iota(jnp.int32, (tk, tq), 1)
    blocks_per_tile = tk // NSA_SEL_LEN

    def sel_tile(kt):
        k0 = pl.multiple_of(kt * tk, tk)
        k = ks_ref[pl.ds(k0, tk), :]
        vt = jnp.concatenate([vst_ref[kt * (tk // chunk) + c] for c in range(tk // chunk)], axis=1)
        sel8 = sel_ref[pl.ds(pl.multiple_of(kt * blocks_per_tile, blocks_per_tile), blocks_per_tile), :]
        chosen = jnp.concatenate([jnp.broadcast_to(sel8[b:b + 1, :], (NSA_SEL_LEN, tq))
                                  for b in range(blocks_per_tile)], axis=0)
        bias = jnp.where((chosen > 0.5) & (key_i + k0 <= t_i), 0.0, NEG_INF)
        s = lax.dot_general(k, qs, NT_DIMS, preferred_element_type=F32) + per_head(bias)
        m_old = m_ref[...]
        m_new = jnp.maximum(m_old, jnp.max(s, axis=0, keepdims=True))
        p = jnp.exp(s - m_new)
        alpha = jnp.exp(m_old - m_new)
        l_ref[...] = alpha * l_ref[...] + jnp.sum(p, axis=0, keepdims=True)
        acc_ref[...] = alpha * acc_ref[...] + jnp.dot(vt, p.astype(BF16), preferred_element_type=F32)
        m_ref[...] = m_new

    def sel_body(it, carry):
        for u in range(SEL_UNROLL):
            sel_tile(it * SEL_UNROLL + u)
        return carry

    n_tiles = (t0 + tq + tk - 1) // tk
    lax.fori_loop(0, (n_tiles + SEL_UNROLL - 1) // SEL_UNROLL, sel_body, 0)
    l_s = l_ref[...]
    o_st = acc_ref[...] * jnp.where(l_s > 0.0, 1.0 / l_s, 0.0)

    band = NSA_WINDOW + tq
    wc = jnp.maximum(qi - NSA_WINDOW // tq, 0)
    w0 = pl.multiple_of(wc * tq, tq)
    kb = kw_ref[pl.ds(w0, band), :]
    vbt = jnp.concatenate([vwt_ref[wc + c] for c in range(band // chunk)], axis=1)
    rel = (t0 + lax.broadcasted_iota(jnp.int32, (band, tq), 1)) - (w0 + lax.broadcasted_iota(jnp.int32, (band, tq), 0))
    wbias = per_head(jnp.where((rel >= 0) & (rel < NSA_WINDOW), 0.0, NEG_INF))
    s = lax.dot_general(kb, qs, NT_DIMS, preferred_element_type=F32) + wbias
    e, inv = softmax_rows(s, M_INIT)
    o_wt = jnp.dot(vbt, e.astype(BF16), preferred_element_type=F32) * inv

    sig_ref[...] = (1.0 / (1.0 + jnp.exp(-gl_ref[...]))).T
    gate = gate_ref[...].astype(F32)
    for h in range(hg):
        base = (g * hg + h) * 3
        gc, gs, gw = (sig_ref[pl.ds(base + c, 1), :] for c in range(3))
        o_t = gc * o_ct[:, lanes(h)] + gs * o_st[:, lanes(h)] + gw * o_wt[:, lanes(h)]
        o_ref[:, head(h)] = (o_t.T * _silu(gate[:, head(h)])).astype(o_ref.dtype)


def _nsa_attention(h2, small, kc, vct, vt, B, S, *, q_tile0, kv_block0, gate_tile0, tq, tk):
    T = h2.shape[0]
    G, hg = NSA_KV_GROUPS, NSA_HEADS_PER_GROUP
    n_cmp = S // NSA_CMP_STRIDE
    n_blk = S // NSA_SEL_LEN
    nQ = S // tq
    gw = hg * HEAD_DIM
    assert n_blk <= NSA_BLK_PAD and S % tk == 0 and S >= NSA_WINDOW + tq and tq == HEAD_DIM
    assert tk % HEAD_DIM == 0 and (tk // NSA_SEL_LEN) % 8 == 0 and (S // tk) % SEL_UNROLL == 0

    nn = np.arange(n_cmp)[None, :]
    jj = np.arange(NSA_BLK_PAD)[:, None]
    ovt = ((nn * NSA_CMP_STRIDE + NSA_CMP_LEN - 1 >= jj * NSA_SEL_LEN) & (nn * NSA_CMP_STRIDE < (jj + 1) * NSA_SEL_LEN)
           & (nn < n_cmp - 1) & (jj < n_blk))

    def k_spec(c):
        return pl.BlockSpec((S, HEAD_DIM), lambda b, g, i: (b, kv_block0 + 4 * c + g))

    def vt_spec(which):
        return pl.BlockSpec((None, None, S // HEAD_DIM, HEAD_DIM, HEAD_DIM), lambda b, g, i: (b, which * G + g, 0, 0, 0))

    rows = lambda width, col: pl.BlockSpec((tq, width), lambda b, g, i: (b * nQ + i, col(g)))
    per_bg = lambda shape: pl.BlockSpec((None,) + shape, lambda b, g, i: (b * G + g, 0, 0))
    const = lambda shape: pl.BlockSpec(shape, lambda b, g, i: (0,) * len(shape))
    return pl.pallas_call(
        functools.partial(_nsa_kernel, tq=tq, tk=tk, n_sel=min(NSA_N_SEL, n_blk)),
        out_shape=jax.ShapeDtypeStruct((T, G * gw), BF16),
        grid=(B, G, nQ),
        in_specs=[rows(gw, lambda g: q_tile0 + g), per_bg((n_cmp, HEAD_DIM)), per_bg((HEAD_DIM, n_cmp)),
                  k_spec(2), vt_spec(0), k_spec(4), vt_spec(1),
                  rows(HEAD_DIM, lambda g: 0), rows(gw, lambda g: gate_tile0 + g),
                  const((NSA_BLK_PAD, n_cmp))],
        out_specs=rows(gw, lambda g: g),
        scratch_shapes=[pltpu.VMEM((NSA_BLK_PAD, tq), F32), pltpu.VMEM((HEAD_DIM, tq), F32),
                        pltpu.VMEM((1, hg * tq), F32), pltpu.VMEM((1, hg * tq), F32),
                        pltpu.VMEM((HEAD_DIM, hg * tq), F32)],
        compiler_params=_params("parallel", "parallel", "arbitrary"),
        name="nsa_attention",
    )(h2, kc, vct, h2, vt, h2, vt, small, h2, jnp.asarray(ovt, BF16))


def _gla_kernel(q_ref, k_ref, v_ref, a_ref, gate_ref, wa_ref, ba_ref, ng_ref, o_ref, state_ref, *, n_chunks):
    C = GLA_CHUNK

    @pl.when(pl.program_id(2) == 0)
    def _():
        state_ref[...] = jnp.zeros_like(state_ref)

    r_i = lax.broadcasted_iota(jnp.int32, (C, C), 0)
    c_i = lax.broadcasted_iota(jnp.int32, (C, C), 1)
    causal = r_i >= c_i
    tri = jnp.where(causal, 1.0, 0.0).astype(BF16)
    wa = wa_ref[...]
    ba = ba_ref[...]
    ng = ng_ref[...]
    for c in range(n_chunks):
        rows = slice(c * C, (c + 1) * C)
        q = q_ref[rows, :].astype(F32)
        k = k_ref[rows, :].astype(F32)
        v = v_ref[rows, :]
        pre = jnp.dot(a_ref[rows, :], wa, preferred_element_type=F32) + ba
        log_a = (jnp.minimum(pre, 0.0) - jnp.log(1.0 + jnp.exp(-jnp.abs(pre)))) * (1.0 / GLA_TAU)
        bcum = sum(jnp.dot(tri, part, preferred_element_type=F32) for part in _split3(log_a))
        b_last = bcum[C - 1:C, :]
        qe = (q * jnp.exp(bcum)).astype(BF16)
        ke = (k * jnp.exp(-bcum)).astype(BF16)
        kd = (k * jnp.exp(b_last - bcum)).astype(BF16)
        attn = lax.dot_general(qe, ke, NT_DIMS, preferred_element_type=F32)
        attn = jnp.where(causal, attn, 0.0).astype(BF16)
        state_t = state_ref[...]
        o = jnp.dot(attn, v, preferred_element_type=F32)
        o = o + lax.dot_general(qe, state_t.astype(BF16), NT_DIMS, preferred_element_type=F32)
        state_ref[...] = state_t * jnp.exp(b_last) + lax.dot_general(v, kd, TN_DIMS, preferred_element_type=F32)
        o = o * lax.rsqrt(jnp.mean(o * o, axis=-1, keepdims=True) + NORM_EPS) * ng
        o_ref[rows, :] = (o * _silu(gate_ref[rows, :].astype(F32))).astype(o_ref.dtype)


def _gla(h2, small, w_a2, b_a, norm_g, B, S, *, q_blk0, k_blk0, v_blk0, gate_blk0, rows):
    T = h2.shape[0]
    H = GLA_HEADS
    nR = S // rows
    a_lo = 3 * NSA_KV_GROUPS * NSA_HEADS_PER_GROUP
    wa = jnp.zeros((H, HEAD_DIM, GLA_DK), F32).at[:, a_lo:a_lo + GLA_RANK, :].set(
        w_a2.reshape(GLA_RANK, H, GLA_DK).transpose(1, 0, 2))
    blk = lambda width, blk0: pl.BlockSpec((rows, width), lambda b, h, i: (b * nR + i, blk0 + h))
    per_head = lambda shape: pl.BlockSpec((None,) + shape, lambda b, h, i: (h, 0, 0))
    return pl.pallas_call(
        functools.partial(_gla_kernel, n_chunks=rows // GLA_CHUNK),
        out_shape=jax.ShapeDtypeStruct((T, H * GLA_DV), BF16),
        grid=(B, H, nR),
        in_specs=[blk(GLA_DK, q_blk0), blk(GLA_DK, k_blk0), blk(GLA_DV, v_blk0),
                  pl.BlockSpec((rows, HEAD_DIM), lambda b, h, i: (b * nR + i, 0)), blk(GLA_DV, gate_blk0),
                  per_head((HEAD_DIM, GLA_DK)), per_head((1, GLA_DK)),
                  pl.BlockSpec((1, GLA_DV), lambda b, h, i: (0, 0))],
        out_specs=blk(GLA_DV, 0),
        scratch_shapes=[pltpu.VMEM((GLA_DV, GLA_DK), F32)],
        compiler_params=_params("parallel", "parallel", "arbitrary"),
        name="gla",
    )(h2, h2, h2, small, h2, wa, b_a.reshape(H, 1, GLA_DK), norm_g.reshape(1, GLA_DV))


def _dilated_kernel(*refs, has_prev, is_last):
    q_ref, kp_ref, kc_ref, vp_ref, vc_ref = refs[:5]
    rest = list(refs[5:])
    po_ref, pl_ref = (rest.pop(0), rest.pop(0)) if has_prev else (None, None)
    gate_ref = rest.pop(0) if is_last else None
    o_ref = rest.pop(0)
    lse_ref = None if is_last else rest.pop(0)
    tq = q_ref.shape[0]
    i = pl.program_id(2)
    row = lax.broadcasted_iota(jnp.int32, (tq, 2 * tq), 0)
    col = lax.broadcasted_iota(jnp.int32, (tq, 2 * tq), 1)
    valid = (col >= row + (tq - DIL_BACK)) & (col <= row + tq) & ((col >= tq) | (i > 0))
    if lse_ref is not None:
        lse_ref[...] = jnp.zeros(lse_ref.shape, F32)
    for h in range(DIL_HEADS):
        sl = slice(h * HEAD_DIM, (h + 1) * HEAD_DIM)
        k = jnp.concatenate([kp_ref[:, sl], kc_ref[:, sl]], axis=0)
        v = jnp.concatenate([vp_ref[:, sl], vc_ref[:, sl]], axis=0)
        s = lax.dot_general(q_ref[:, sl], k, NT_DIMS, preferred_element_type=F32)
        s = jnp.where(valid, s, NEG_INF)
        m = jnp.max(s, axis=1, keepdims=True)
        e = jnp.exp(s - m)
        den = jnp.sum(e, axis=1, keepdims=True)
        o = jnp.dot(e.astype(BF16), v, preferred_element_type=F32) * (1.0 / den)
        lse = m + jnp.log(den)
        if has_prev:
            lse_p = pl_ref[:, h:h + 1]
            lse_n = jnp.maximum(lse_p, lse) + jnp.log(1.0 + jnp.exp(-jnp.abs(lse_p - lse)))
            o = jnp.exp(lse_p - lse_n) * po_ref[:, sl].astype(F32) + jnp.exp(lse - lse_n) * o
            lse = lse_n
        if is_last:
            o = o * _silu(gate_ref[:, sl].astype(F32))
        else:
            lse_ref[:, h:h + 1] = lse
        o_ref[:, sl] = o.astype(o_ref.dtype)


def _dilated_group(hg, prev_o, prev_lse, gate, *, d, tq, is_last):
    B, _, L, W3 = hg.shape
    W = W3 // 3
    has_prev = prev_o is not None
    assert tq == DIL_BACK and L % tq == 0
    cur = lambda col: pl.BlockSpec((None, None, tq, W), lambda b, r, i: (b, r, i, col))
    prv = lambda col: pl.BlockSpec((None, None, tq, W), lambda b, r, i: (b, r, jnp.maximum(i - 1, 0), col))
    tok = lambda width: pl.BlockSpec((None, tq, width), lambda b, r, i: (b, i, r))
    in_specs = [cur(0), prv(1), cur(1), prv(2), cur(2)]
    args = [hg] * 5
    if has_prev:
        in_specs += [tok(W), tok(HEAD_DIM)]
        args += [prev_o.reshape(B, L, d * W), prev_lse.reshape(B, L, d * HEAD_DIM)]
    if is_last:
        in_specs += [tok(W)]
        args += [gate.reshape(B, L, d * W)]
    out_shape = [jax.ShapeDtypeStruct((B, L, d * W), BF16)]
    out_specs = [tok(W)]
    if not is_last:
        out_shape += [jax.ShapeDtypeStruct((B, L, d * HEAD_DIM), F32)]
        out_specs += [tok(HEAD_DIM)]
    res = pl.pallas_call(
        functools.partial(_dilated_kernel, has_prev=has_prev, is_last=is_last),
        out_shape=tuple(out_shape),
        grid=(B, d, L // tq),
        in_specs=in_specs,
        out_specs=tuple(out_specs),
        compiler_params=_params("parallel", "parallel", "arbitrary"),
        name=f"dilated_d{d}",
    )(*args)
    o = res[0].reshape(B, L * d, W)
    return (o, None) if is_last else (o, res[1].reshape(B, L * d, HEAD_DIM))


def _row_tile(n, target):
    t = min(n, target)
    assert n % t == 0
    return t


def _even_layer(xb3, w_in, pe_k, w1_k, w2_k, pe_v, w1_v, w2_v, gla_w_a2, gla_b_a, gla_norm_g, w_out):
    B, S, D = xb3.shape
    T = B * S
    nq = NSA_KV_GROUPS * NSA_HEADS_PER_GROUP * HEAD_DIM
    nkv = 6 * NSA_KV_GROUPS * HEAD_DIM
    ng = 3 * NSA_KV_GROUPS * NSA_HEADS_PER_GROUP
    gk = GLA_HEADS * GLA_DK
    gv = GLA_HEADS * GLA_DV
    o_g = nq + nkv
    o_a = o_g + ng + nq + 2 * gk + gv
    w_main = jnp.concatenate([w_in[:, :o_g], w_in[:, o_g + ng:o_a], w_in[:, o_a + GLA_RANK:]], axis=1).astype(BF16)
    w_small = jnp.concatenate([w_in[:, o_g:o_g + ng], w_in[:, o_a:o_a + GLA_RANK],
                               jnp.zeros((D, HEAD_DIM - ng - GLA_RANK), w_in.dtype)], axis=1).astype(BF16)
    tn = 512
    kinds = ([EPI_ROPE_QSCALE] * (nq // tn)
             + [EPI_PLAIN, EPI_PLAIN, EPI_ROPE, EPI_PLAIN, EPI_ROPE, EPI_PLAIN]
             + [EPI_PLAIN] * (nq // tn) + [EPI_GLA_QSCALE] * (gk // tn) + [EPI_PLAIN] * (gk // tn)
             + [EPI_PLAIN] * (gv // tn) + [EPI_PLAIN] * (gv // tn))
    cos, sin = _rope_tables(jnp.arange(S))
    cos, sin = cos[None], sin[None]
    tm = _row_tile(S, 1024)
    h2 = _project(xb3, w_main, kinds, cos, sin, d=1, tm=tm, tn=tn, out_dtype=BF16).reshape(T, -1)
    small = _project(xb3, w_small, [EPI_PLAIN], cos, sin, d=1, tm=tm, tn=HEAD_DIM, out_dtype=F32).reshape(T, HEAD_DIM)

    q_tile0 = 0
    kv_block0 = nq // HEAD_DIM
    gate_tile0 = (nq + nkv) // tn
    kc, vct = _nsa_compress(h2, B, S, pe_k, w1_k, w2_k, pe_v, w1_v, w2_v, kv_block0, kv_block0 + NSA_KV_GROUPS)
    v_blocks = [kv_block0 + 4 * c + g for c in (3, 5) for g in range(NSA_KV_GROUPS)]
    vt = _transpose_values(h2, B, S, v_blocks, rows=_row_tile(S, 1024))
    o_nsa = _nsa_attention(h2, small, kc, vct, vt, B, S, q_tile0=q_tile0, kv_block0=kv_block0,
                           gate_tile0=gate_tile0, tq=128, tk=_row_tile(S, 512))
    c0 = nq + nkv + nq
    o_gla = _gla(h2, small, gla_w_a2, gla_b_a, gla_norm_g, B, S,
                 q_blk0=c0 // GLA_DK, k_blk0=(c0 + gk) // GLA_DK, v_blk0=(c0 + 2 * gk) // GLA_DV,
                 gate_blk0=(c0 + 2 * gk + gv) // GLA_DV, rows=_row_tile(S, 512))
    w_out = w_out.astype(BF16)
    return _out_project([o_nsa, o_gla], [w_out[:nq], w_out[nq:]], tm=_row_tile(T, 1024), tn=512)


def _odd_layer(xb3, w_in, w_out):
    B, S, D = xb3.shape
    T = B * S
    n_g = len(DIL_GROUPS)
    W = DIL_HEADS * HEAD_DIM
    tn = 512
    cos, sin = _rope_tables(jnp.arange(S))
    gate = _project(xb3, w_in[:, 3 * n_g * W:].astype(BF16), [EPI_PLAIN] * (W // tn), cos[None], sin[None],
                    d=1, tm=_row_tile(S, 1024), tn=tn, out_dtype=BF16).reshape(B, S, W)
    kinds = [EPI_ROPE_QSCALE] * (W // tn) + [EPI_ROPE] * (W // tn) + [EPI_PLAIN] * (W // tn)
    o, lse = None, None
    for gi, (window, d) in enumerate(DIL_GROUPS):
        assert window // d == DIL_BACK
        L = S // d
        w_g = jnp.concatenate([w_in[:, (c * n_g + gi) * W:(c * n_g + gi + 1) * W] for c in range(3)], axis=1).astype(BF16)
        to_streams = lambda t: t.reshape(L, d, HEAD_DIM).transpose(1, 0, 2)
        hg = _project(xb3.reshape(B, L, d * D), w_g, kinds, to_streams(cos), to_streams(sin),
                      d=d, tm=_row_tile(L, 1024), tn=tn, out_dtype=BF16)
        o, lse = _dilated_group(hg, o, lse, gate, d=d, tq=DIL_BACK, is_last=gi == n_g - 1)
    return _out_project([o.reshape(T, W)], [w_out.astype(BF16)], tm=_row_tile(T, 1024), tn=512)


def kernel(x, l0_w_in, l0_nsa_pe_k, l0_nsa_w1_k, l0_nsa_w2_k, l0_nsa_pe_v, l0_nsa_w1_v, l0_nsa_w2_v, l0_gla_w_a2, l0_gla_b_a, l0_gla_norm_g, l0_w_out, l0_ln_g, l0_ln_b, l1_w_in, l1_w_out, l1_ln_g, l1_ln_b):
    B, S, D = x.shape
    T = B * S
    x2 = x.reshape(T, D)
    tr = _row_tile(T, 256)
    y0 = _even_layer(x.astype(BF16), l0_w_in, l0_nsa_pe_k, l0_nsa_w1_k, l0_nsa_w2_k, l0_nsa_pe_v, l0_nsa_w1_v,
                     l0_nsa_w2_v, l0_gla_w_a2, l0_gla_b_a, l0_gla_norm_g, l0_w_out)
    x1, x1b = _deepnorm_ln(x2, y0, l0_ln_g, l0_ln_b, tr=tr)
    y1 = _odd_layer(x1b.reshape(B, S, D), l1_w_in, l1_w_out)
    out, _ = _deepnorm_ln(x1, y1, l1_ln_g, l1_ln_b, tr=tr)
    return out.reshape(B, S, D)
```

```python
import functools

import numpy as np
import jax
import jax.numpy as jnp
from jax import lax
from jax.experimental import pallas as pl
from jax.experimental.pallas import tpu as pltpu

F32 = jnp.float32
BF16 = jnp.bfloat16

HEAD_DIM = 128
ROPE_THETA = 10000.0
LN_EPS = 1e-5
NORM_EPS = 1e-6
NEG_INF = -1e30
M_INIT = -1e29
TAKEN = -3e38
SEL_UNROLL = 4
SEL_SUB = 256
VT_ROWS = HEAD_DIM + 16

NSA_KV_GROUPS = 4
NSA_HEADS_PER_GROUP = 4
NSA_CMP_LEN = 32
NSA_CMP_STRIDE = 16
NSA_SEL_LEN = 64
NSA_N_SEL = 16
NSA_WINDOW = 512
NSA_FORCE_BONUS = 1e4
NSA_BLK_PAD = 128

GLA_HEADS = 4
GLA_DK = 256
GLA_DV = 512
GLA_RANK = 16
GLA_TAU = 16.0
GLA_CHUNK = 64

DIL_GROUPS = ((128, 1), (512, 4), (2048, 16))
DIL_HEADS = 16
DIL_BACK = 128

DEPTH = 2
DEEPNORM_ALPHA = (2.0 * DEPTH) ** 0.25

VMEM_LIMIT_BYTES = 56 * 1024 * 1024

EPI_PLAIN, EPI_ROPE, EPI_ROPE_QSCALE, EPI_GLA_QSCALE, EPI_ROPE_QSCALE_LOG2 = 0, 1, 2, 3, 4
Q_SCALE = HEAD_DIM ** -0.5
LOG2E = 1.4426950408889634
GLA_Q_SCALE = GLA_DK ** -0.5

NT_DIMS = (((1,), (1,)), ((), ()))
TN_DIMS = (((0,), (0,)), ((), ()))


def _params(*sem):
    return pltpu.CompilerParams(dimension_semantics=sem, vmem_limit_bytes=VMEM_LIMIT_BYTES)


def _silu(x):
    return x * (1.0 / (1.0 + jnp.exp(-x)))


def _rope_tables(pos):
    half = HEAD_DIM // 2
    inv_freq = ROPE_THETA ** (-jnp.arange(half, dtype=F32) / half)
    ang = pos.astype(F32)[:, None] * inv_freq[None, :]
    cos = jnp.cos(ang)
    sin = jnp.sin(ang)
    return jnp.concatenate([cos, cos], axis=-1), jnp.concatenate([-sin, sin], axis=-1)


def _rope(a, cos, sin):
    return a * cos + pltpu.roll(a, HEAD_DIM // 2, 1) * sin


def _proj_kernel(kind_ref, a_ref, w_ref, cos_ref, sin_ref, o_ref):
    kind = kind_ref[pl.program_id(1)]
    acc = jnp.dot(a_ref[...], w_ref[...], preferred_element_type=F32)

    @pl.when(kind == EPI_PLAIN)
    def _():
        o_ref[...] = acc.astype(o_ref.dtype)

    @pl.when(kind == EPI_GLA_QSCALE)
    def _():
        o_ref[...] = (acc * GLA_Q_SCALE).astype(o_ref.dtype)

    @pl.when((kind == EPI_ROPE) | (kind == EPI_ROPE_QSCALE) | (kind == EPI_ROPE_QSCALE_LOG2))
    def _():
        scale = jnp.where(kind == EPI_ROPE, 1.0,
                          jnp.where(kind == EPI_ROPE_QSCALE, Q_SCALE, Q_SCALE * LOG2E)).astype(F32)
        cos = cos_ref[...]
        sin = sin_ref[...]
        for s in range(acc.shape[1] // HEAD_DIM):
            sl = slice(s * HEAD_DIM, (s + 1) * HEAD_DIM)
            o_ref[:, sl] = (_rope(acc[:, sl], cos, sin) * scale).astype(o_ref.dtype)


def _project(act4, w, kinds, cos, sin, *, tm, tn, out_dtype):
    B, d, L, K = act4.shape
    N = w.shape[1]
    nI, nJ = L // tm, N // tn
    assert L % tm == 0 and N % tn == 0 and len(kinds) == nJ

    def split(m):
        return m // (d * nI), (m // nI) % d, m % nI

    def a_idx(m, j, k):
        b, r, i = split(m)
        return (b, r, i, 0)

    def t_idx(m, j, k):
        b, r, i = split(m)
        return (r, i, 0)

    def o_idx(m, j, k):
        b, r, i = split(m)
        return (b, r, i, j)

    grid_spec = pltpu.PrefetchScalarGridSpec(
        num_scalar_prefetch=1,
        grid=(B * d * nI, nJ),
        in_specs=[
            pl.BlockSpec((None, None, tm, K), a_idx),
            pl.BlockSpec((K, tn), lambda m, j, k: (0, j)),
            pl.BlockSpec((None, tm, HEAD_DIM), t_idx),
            pl.BlockSpec((None, tm, HEAD_DIM), t_idx),
        ],
        out_specs=pl.BlockSpec((None, None, tm, tn), o_idx),
    )
    return pl.pallas_call(
        _proj_kernel,
        out_shape=jax.ShapeDtypeStruct((B, d, L, N), out_dtype),
        grid_spec=grid_spec,
        compiler_params=_params("parallel", "arbitrary"),
        name=f"proj_d{d}_n{N}",
    )(jnp.asarray(kinds, jnp.int32), act4, w, cos, sin)


def _outproj_kernel(*refs):
    o_ref = refs[-1]
    acc = None
    for a_ref, w_ref in zip(refs[0:-1:2], refs[1:-1:2]):
        part = jnp.dot(a_ref[...], w_ref[...], preferred_element_type=F32)
        acc = part if acc is None else acc + part
    o_ref[...] = acc.astype(o_ref.dtype)


def _out_project(acts, ws, *, tm, tn):
    T = acts[0].shape[0]
    N = ws[0].shape[1]
    in_specs, args = [], []
    for a, w in zip(acts, ws):
        K = a.shape[1]
        in_specs += [pl.BlockSpec((tm, K), lambda i, j: (i, 0)), pl.BlockSpec((K, tn), lambda i, j: (0, j))]
        args += [a, w]
    return pl.pallas_call(
        _outproj_kernel,
        out_shape=jax.ShapeDtypeStruct((T, N), BF16),
        grid=(T // tm, N // tn),
        in_specs=in_specs,
        out_specs=pl.BlockSpec((tm, tn), lambda i, j: (i, j)),
        compiler_params=_params("parallel", "arbitrary"),
        name=f"outproj_{len(acts)}",
    )(*args)


def _ln_kernel(x_ref, y_ref, g_ref, b_ref, o_ref, *stream_refs, dilations):
    z = DEEPNORM_ALPHA * x_ref[...] + y_ref[...].astype(F32)
    mu = jnp.mean(z, axis=-1, keepdims=True)
    zc = z - mu
    var = jnp.mean(zc * zc, axis=-1, keepdims=True)
    out = zc * lax.rsqrt(var + LN_EPS) * g_ref[...] + b_ref[...]
    o_ref[...] = out
    tr = out.shape[0]
    ob = out.astype(BF16)
    dst = lax.broadcasted_iota(jnp.int32, (tr, tr), 0)
    src = lax.broadcasted_iota(jnp.int32, (tr, tr), 1)
    for d, s_ref in zip(dilations, stream_refs):
        if d == 1:
            s_ref[0] = ob
            continue
        per = tr // d
        perm = jnp.where(src == (dst & (per - 1)) * d + (dst >> (per.bit_length() - 1)), 1.0, 0.0).astype(BF16)
        sm = jnp.dot(perm, ob, preferred_element_type=F32).astype(BF16)
        for r in range(d):
            s_ref[r] = sm[r * per:(r + 1) * per]


def _deepnorm_ln(x3, y2, g, b, *, tr, dilations=()):
    B, S, D = x3.shape
    nI = S // tr
    assert S % tr == 0 and all(tr % (d * 16) == 0 and d & (d - 1) == 0 for d in dilations) and tr & (tr - 1) == 0
    row3 = pl.BlockSpec((None, tr, D), lambda b, i: (b, i, 0))
    vec = pl.BlockSpec((1, D), lambda b, i: (0, 0))
    out_shape = [jax.ShapeDtypeStruct((B, S, D), F32)]
    out_specs = [row3]
    for d in dilations:
        out_shape.append(jax.ShapeDtypeStruct((B, d, S // d, D), BF16))
        out_specs.append(pl.BlockSpec((None, d, tr // d, D), lambda b, i: (b, 0, i, 0)))
    return pl.pallas_call(
        functools.partial(_ln_kernel, dilations=tuple(dilations)),
        out_shape=tuple(out_shape),
        grid=(B, nI),
        in_specs=[row3, pl.BlockSpec((tr, D), lambda b, i: (b * nI + i, 0)), vec, vec],
        out_specs=tuple(out_specs),
        compiler_params=_params("parallel", "parallel"),
        name="deepnorm_ln",
    )(x3, y2, g.reshape(1, D), b.reshape(1, D))


def _compress_kernel(zk_ref, zv_ref, pe_k_ref, pe_v_ref, w1ka_ref, w1kb_ref, w1va_ref, w1vb_ref,
                     w2k_ref, w2vt_ref, cos_ref, sin_ref, kc_ref, vct_ref, uk_ref, vk_ref, uv_ref, vv_ref):
    p = pl.program_id(1)
    n = uk_ref.shape[0]

    @pl.when(p == 0)
    def _():
        for ref in (uk_ref, vk_ref, uv_ref, vv_ref):
            ref[...] = jnp.zeros_like(ref)

    def accumulate(z_ref, pe_ref, wa_ref, wb_ref, u_ref, v_ref):
        z = z_ref[...].astype(F32)
        za = (z + pe_ref[pl.ds(p, 1), :]).astype(BF16)
        zb = (z + pe_ref[pl.ds(p + NSA_CMP_STRIDE, 1), :]).astype(BF16)
        u_ref[...] += jnp.dot(za, wa_ref[...], preferred_element_type=F32)
        v_ref[...] += jnp.dot(zb, wb_ref[...], preferred_element_type=F32)

    accumulate(zk_ref, pe_k_ref, w1ka_ref, w1kb_ref, uk_ref, vk_ref)
    accumulate(zv_ref, pe_v_ref, w1va_ref, w1vb_ref, uv_ref, vv_ref)

    @pl.when(p == NSA_CMP_STRIDE - 1)
    def _():
        hid_k = _silu(uk_ref[...] + pltpu.roll(vk_ref[...], n - 1, 0)).astype(BF16)
        hid_v = _silu(uv_ref[...] + pltpu.roll(vv_ref[...], n - 1, 0)).astype(BF16)
        kc = jnp.dot(hid_k, w2k_ref[...], preferred_element_type=F32)
        kc = _rope(kc, cos_ref[...], sin_ref[...])
        row = lax.broadcasted_iota(jnp.int32, kc.shape, 0)
        kc_ref[...] = jnp.where(row < n - 1, kc, 0.0).astype(kc_ref.dtype)
        vct = lax.dot_general(w2vt_ref[...], hid_v, NT_DIMS, preferred_element_type=F32)
        col = lax.broadcasted_iota(jnp.int32, vct.shape, 1)
        vct_ref[...] = jnp.where(col < n - 1, vct, 0.0).astype(vct_ref.dtype)


def _nsa_compress(h2, B, S, pe_k, w1_k, w2_k, pe_v, w1_v, w2_v, kc_block, vc_block):
    G = NSA_KV_GROUPS
    n = S // NSA_CMP_STRIDE
    c0, c1 = kc_block * HEAD_DIM, (vc_block + G) * HEAD_DIM
    nb = (c1 - c0) // HEAD_DIM
    h3 = h2[:, c0:c1].reshape(B, n, NSA_CMP_STRIDE * (c1 - c0))
    kc_block, vc_block = 0, vc_block - kc_block
    cos, sin = _rope_tables(jnp.arange(n) * NSA_CMP_STRIDE + NSA_CMP_LEN - 1)

    def z_spec(block0):
        return pl.BlockSpec((None, n, HEAD_DIM), lambda bg, p: (bg // G, 0, p * nb + block0 + bg % G))

    full = lambda shape: pl.BlockSpec(shape, lambda bg, p: (0,) * len(shape))
    w1a = pl.BlockSpec((HEAD_DIM, HEAD_DIM), lambda bg, p: (p, 0))
    w1b = pl.BlockSpec((HEAD_DIM, HEAD_DIM), lambda bg, p: (p + NSA_CMP_STRIDE, 0))
    acc = pltpu.VMEM((n, HEAD_DIM), F32)
    return pl.pallas_call(
        _compress_kernel,
        out_shape=(jax.ShapeDtypeStruct((B * G, n, HEAD_DIM), BF16), jax.ShapeDtypeStruct((B * G, HEAD_DIM, n), BF16)),
        grid=(B * G, NSA_CMP_STRIDE),
        in_specs=[z_spec(kc_block), z_spec(vc_block), full((NSA_CMP_LEN, HEAD_DIM)), full((NSA_CMP_LEN, HEAD_DIM)),
                  w1a, w1b, w1a, w1b, full((HEAD_DIM, HEAD_DIM)), full((HEAD_DIM, HEAD_DIM)),
                  full((n, HEAD_DIM)), full((n, HEAD_DIM))],
        out_specs=(pl.BlockSpec((None, n, HEAD_DIM), lambda bg, p: (bg, 0, 0)),
                   pl.BlockSpec((None, HEAD_DIM, n), lambda bg, p: (bg, 0, 0))),
        scratch_shapes=[acc, acc, acc, acc],
        compiler_params=_params("parallel", "arbitrary"),
        name="nsa_compress",
    )(h3, h3, pe_k, pe_v, w1_k.astype(BF16), w1_k.astype(BF16), w1_v.astype(BF16), w1_v.astype(BF16),
      w2_k.astype(BF16), w2_v.T.astype(BF16), cos, sin)


def _split3(x):
    hi = x.astype(BF16)
    r1 = x - hi.astype(F32)
    mid = r1.astype(BF16)
    lo = (r1 - mid.astype(F32)).astype(BF16)
    return hi, mid, lo


def _vt_kernel(blk_ref, v_ref, o_ref):
    extra = VT_ROWS - HEAD_DIM
    ones_row = jnp.where(lax.broadcasted_iota(jnp.int32, (extra, HEAD_DIM), 0) == 0, 1.0, 0.0).astype(o_ref.dtype)
    for c in range(o_ref.shape[0]):
        o_ref[c, 0:HEAD_DIM, :] = v_ref[c * HEAD_DIM:(c + 1) * HEAD_DIM, :].astype(F32).T.astype(o_ref.dtype)
        o_ref[c, HEAD_DIM:VT_ROWS, :] = ones_row


def _transpose_values(h2, B, S, blocks, *, rows):
    ns = S // rows
    per = rows // HEAD_DIM
    nv = len(blocks)
    grid_spec = pltpu.PrefetchScalarGridSpec(
        num_scalar_prefetch=1,
        grid=(B, nv, ns),
        in_specs=[pl.BlockSpec((rows, HEAD_DIM), lambda b, i, s, blk: (b * ns + s, blk[i]))],
        out_specs=pl.BlockSpec((None, None, per, VT_ROWS, HEAD_DIM), lambda b, i, s, blk: (b, i, s, 0, 0)),
    )
    return pl.pallas_call(
        _vt_kernel,
        out_shape=jax.ShapeDtypeStruct((B, nv, S // HEAD_DIM, VT_ROWS, HEAD_DIM), BF16),
        grid_spec=grid_spec,
        compiler_params=_params("parallel", "parallel", "arbitrary"),
        name="transpose_values",
    )(jnp.asarray(blocks, jnp.int32), h2)


def _nsa_kernel(q_ref, kc_ref, vct_ref, ks_ref, vst_ref, kw_ref, vwt_ref, gl_ref, gate_ref, ovt_ref,
                o_ref, sel_ref, sig_ref, *, tq, tk, n_sel):
    g = pl.program_id(1)
    qi = pl.program_id(2)
    t0 = qi * tq
    hg = NSA_HEADS_PER_GROUP
    n_cmp = kc_ref.shape[0]
    nbp = NSA_BLK_PAD
    head = lambda h: slice(h * HEAD_DIM, (h + 1) * HEAD_DIM)
    lanes = lambda h: slice(h * tq, (h + 1) * tq)
    per_head = lambda a: jnp.concatenate([a] * hg, axis=1)
    chunk = HEAD_DIM

    q4 = q_ref[...]
    qs = jnp.concatenate([q4[:, head(h)] for h in range(hg)], axis=0)

    def softmax_rows(s, m_floor):
        m = jnp.maximum(jnp.max(s, axis=0, keepdims=True), m_floor)
        e = jnp.exp2(s - m)
        den = jnp.sum(e, axis=0, keepdims=True)
        return e, jnp.where(den > 0.0, 1.0 / den, 0.0)

    t_lane = t0 + lax.broadcasted_iota(jnp.int32, (n_cmp, tq), 1)
    cmp_end = lax.broadcasted_iota(jnp.int32, (n_cmp, tq), 0) * NSA_CMP_STRIDE + (NSA_CMP_LEN - 1)
    cbias = per_head(jnp.where(cmp_end <= t_lane, 0.0, NEG_INF))
    s = lax.dot_general(kc_ref[...], qs, NT_DIMS, preferred_element_type=F32) + cbias
    e, inv = softmax_rows(s, M_INIT)
    p = e * inv
    psum = sum(p[:, lanes(h)] for h in range(hg))
    o_ct = jnp.dot(vct_ref[...], p.astype(BF16), preferred_element_type=F32)

    ovt = ovt_ref[...]
    imp = sum(jnp.dot(ovt, part, preferred_element_type=F32) for part in _split3(psum))
    j_idx = lax.broadcasted_iota(jnp.int32, (nbp, tq), 0)
    cur = (t0 + lax.broadcasted_iota(jnp.int32, (nbp, tq), 1)) >> 6
    forced = (j_idx == 0) | (j_idx == cur) | (j_idx == cur - 1)
    imp = jnp.where(forced, imp + NSA_FORCE_BONUS, imp)
    imp = jnp.where(j_idx <= cur, imp, NEG_INF)

    j_f = j_idx.astype(F32)
    picked = jnp.zeros((nbp, tq), F32)
    for _ in range(n_sel):
        top = jnp.max(imp, axis=0, keepdims=True)
        first = jnp.min(jnp.where(imp == top, j_f, float(nbp)), axis=0, keepdims=True)
        hit = j_f == first
        picked = jnp.where(hit, 1.0, picked)
        imp = jnp.where(hit, TAKEN, imp)
    sel_ref[...] = jnp.where(j_idx <= cur, picked, 0.0)

    n_pairs = hg // 2
    key_i = lax.broadcasted_iota(jnp.int32, (SEL_SUB, tq), 0)
    t_i = t0 + lax.broadcasted_iota(jnp.int32, (SEL_SUB, tq), 1)
    blocks_per_tile = tk // NSA_SEL_LEN
    blocks_per_sub = SEL_SUB // NSA_SEL_LEN
    q_pair = [qs[pr * 2 * tq:(pr + 1) * 2 * tq] for pr in range(n_pairs)]

    def sel_scores(kt):
        k0 = pl.multiple_of(kt * tk, tk)
        sel8 = sel_ref[pl.ds(pl.multiple_of(kt * blocks_per_tile, blocks_per_tile), blocks_per_tile), :]
        scores = []
        for sub in range(tk // SEL_SUB):
            ks0 = k0 + sub * SEL_SUB
            k = ks_ref[pl.ds(pl.multiple_of(ks0, SEL_SUB), SEL_SUB), :]
            chosen = jnp.concatenate([jnp.broadcast_to(sel8[sub * blocks_per_sub + b:sub * blocks_per_sub + b + 1, :],
                                                       (NSA_SEL_LEN, tq)) for b in range(blocks_per_sub)], axis=0)
            bias = jnp.where((chosen > 0.5) & (key_i + ks0 <= t_i), 0.0, NEG_INF)
            bias2 = jnp.concatenate([bias, bias], axis=1)
            for pr in range(n_pairs):
                scores.append(lax.dot_general(k, q_pair[pr], NT_DIMS, preferred_element_type=F32) + bias2)
        return tuple(scores)

    def sel_tiles(kt0, count, carry):
        m, acc = list(carry[0]), list(carry[1])
        scores = [sel_scores(kt0 + u) for u in range(count)]
        for u in range(count):
            for sub in range(tk // SEL_SUB):
                c0 = (kt0 + u) * (tk // chunk) + sub * (SEL_SUB // chunk)
                vt = jnp.concatenate([vst_ref[c0 + c] for c in range(SEL_SUB // chunk)], axis=1)
                for pr in range(n_pairs):
                    s = scores[u][sub * n_pairs + pr]
                    m_new = jnp.maximum(m[pr], jnp.max(s, axis=0, keepdims=True))
                    p = jnp.exp2(s - m_new).astype(BF16)
                    acc[pr] = jnp.exp2(m[pr] - m_new) * acc[pr] + jnp.dot(vt, p, preferred_element_type=F32)
                    m[pr] = m_new
        return tuple(m), tuple(acc)

    n_tiles = (t0 + tq + tk - 1) // tk
    n_groups = n_tiles // SEL_UNROLL
    carry = (tuple(jnp.full((1, 2 * tq), M_INIT, F32) for _ in range(n_pairs)),
             tuple(jnp.zeros((VT_ROWS, 2 * tq), F32) for _ in range(n_pairs)))
    carry = lax.fori_loop(0, n_groups, lambda it, c: sel_tiles(it * SEL_UNROLL, SEL_UNROLL, c), carry)
    _, acc_pairs = lax.fori_loop(n_groups * SEL_UNROLL, n_tiles, lambda kt, c: sel_tiles(kt, 1, c), carry)
    acc = jnp.concatenate(acc_pairs, axis=1)
    l_s = acc[HEAD_DIM:HEAD_DIM + 1, :]
    o_st = acc[:HEAD_DIM, :] * jnp.where(l_s > 0.0, 1.0 / l_s, 0.0)

    band = NSA_WINDOW + tq
    wc = jnp.maximum(qi - NSA_WINDOW // tq, 0)
    w0 = pl.multiple_of(wc * tq, tq)
    kb = kw_ref[pl.ds(w0, band), :]
    vbt = jnp.concatenate([vwt_ref[wc + c, 0:HEAD_DIM, :] for c in range(band // chunk)], axis=1)
    rel = (t0 + lax.broadcasted_iota(jnp.int32, (band, tq), 1)) - (w0 + lax.broadcasted_iota(jnp.int32, (band, tq), 0))
    wbias = per_head(jnp.where((rel >= 0) & (rel < NSA_WINDOW), 0.0, NEG_INF))
    s = lax.dot_general(kb, qs, NT_DIMS, preferred_element_type=F32) + wbias
    e, inv = softmax_rows(s, M_INIT)
    o_wt = jnp.dot(vbt, e.astype(BF16), preferred_element_type=F32) * inv

    sig_ref[...] = (1.0 / (1.0 + jnp.exp(-gl_ref[...]))).T
    gate = gate_ref[...].astype(F32)
    for h in range(hg):
        base = (g * hg + h) * 3
        gc, gs, gw = (sig_ref[pl.ds(base + c, 1), :] for c in range(3))
        o_t = gc * o_ct[:, lanes(h)] + gs * o_st[:, lanes(h)] + gw * o_wt[:, lanes(h)]
        o_ref[:, head(h)] = (o_t.T * _silu(gate[:, head(h)])).astype(o_ref.dtype)


def _nsa_attention(h2, small, kc, vct, vt, B, S, *, q_tile0, kv_block0, gate_tile0, tq, tk):
    T = h2.shape[0]
    G, hg = NSA_KV_GROUPS, NSA_HEADS_PER_GROUP
    n_cmp = S // NSA_CMP_STRIDE
    n_blk = S // NSA_SEL_LEN
    nQ = S // tq
    gw = hg * HEAD_DIM
    assert n_blk <= NSA_BLK_PAD and S % tk == 0 and S >= NSA_WINDOW + tq and tq == HEAD_DIM
    assert tk % SEL_SUB == 0 and (tk // NSA_SEL_LEN) % 8 == 0 and hg % 2 == 0

    nn = np.arange(n_cmp)[None, :]
    jj = np.arange(NSA_BLK_PAD)[:, None]
    ovt = ((nn * NSA_CMP_STRIDE + NSA_CMP_LEN - 1 >= jj * NSA_SEL_LEN) & (nn * NSA_CMP_STRIDE < (jj + 1) * NSA_SEL_LEN)
           & (nn < n_cmp - 1) & (jj < n_blk))

    def k_spec(c):
        return pl.BlockSpec((S, HEAD_DIM), lambda b, g, i: (b, kv_block0 + 4 * c + g))

    def vt_spec(which):
        return pl.BlockSpec((None, None, S // HEAD_DIM, VT_ROWS, HEAD_DIM), lambda b, g, i: (b, which * G + g, 0, 0, 0))

    rows = lambda width, col: pl.BlockSpec((tq, width), lambda b, g, i: (b * nQ + i, col(g)))
    per_bg = lambda shape: pl.BlockSpec((None,) + shape, lambda b, g, i: (b * G + g, 0, 0))
    const = lambda shape: pl.BlockSpec(shape, lambda b, g, i: (0,) * len(shape))
    return pl.pallas_call(
        functools.partial(_nsa_kernel, tq=tq, tk=tk, n_sel=min(NSA_N_SEL, n_blk)),
        out_shape=jax.ShapeDtypeStruct((T, G * gw), BF16),
        grid=(B, G, nQ),
        in_specs=[rows(gw, lambda g: q_tile0 + g), per_bg((n_cmp, HEAD_DIM)), per_bg((HEAD_DIM, n_cmp)),
                  k_spec(2), vt_spec(0), k_spec(4), vt_spec(1),
                  rows(HEAD_DIM, lambda g: 0), rows(gw, lambda g: gate_tile0 + g),
                  const((NSA_BLK_PAD, n_cmp))],
        out_specs=rows(gw, lambda g: g),
        scratch_shapes=[pltpu.VMEM((NSA_BLK_PAD, tq), F32), pltpu.VMEM((HEAD_DIM, tq), F32)],
        compiler_params=_params("parallel", "parallel", "arbitrary"),
        name="nsa_attention",
    )(h2, kc, vct, h2, vt, h2, vt, small, h2, jnp.asarray(ovt, BF16))


def _gla_kernel(q_ref, k_ref, v_ref, a_ref, gate_ref, wa_ref, ba_ref, ng_ref, o_ref, state_ref, *, n_chunks):
    C = GLA_CHUNK

    @pl.when(pl.program_id(2) == 0)
    def _():
        state_ref[...] = jnp.zeros_like(state_ref)

    r_i = lax.broadcasted_iota(jnp.int32, (C, C), 0)
    c_i = lax.broadcasted_iota(jnp.int32, (C, C), 1)
    causal = r_i >= c_i
    tri = jnp.where(causal, 1.0, 0.0).astype(BF16)
    wa = wa_ref[...]
    ba = ba_ref[...]
    ng = ng_ref[...]
    for c in range(n_chunks):
        rows = slice(c * C, (c + 1) * C)
        q = q_ref[rows, :].astype(F32)
        k = k_ref[rows, :].astype(F32)
        v = v_ref[rows, :]
        pre = jnp.dot(a_ref[rows, :], wa, preferred_element_type=F32) + ba
        log_a = (jnp.minimum(pre, 0.0) - jnp.log(1.0 + jnp.exp(-jnp.abs(pre)))) * (1.0 / GLA_TAU)
        bcum = sum(jnp.dot(tri, part, preferred_element_type=F32) for part in _split3(log_a))
        b_last = bcum[C - 1:C, :]
        qe = (q * jnp.exp(bcum)).astype(BF16)
        ke = (k * jnp.exp(-bcum)).astype(BF16)
        kd = (k * jnp.exp(b_last - bcum)).astype(BF16)
        attn = lax.dot_general(qe, ke, NT_DIMS, preferred_element_type=F32)
        attn = jnp.where(causal, attn, 0.0).astype(BF16)
        state_t = state_ref[...]
        o = jnp.dot(attn, v, preferred_element_type=F32)
        o = o + lax.dot_general(qe, state_t.astype(BF16), NT_DIMS, preferred_element_type=F32)
        state_ref[...] = state_t * jnp.exp(b_last) + lax.dot_general(v, kd, TN_DIMS, preferred_element_type=F32)
        o = o * lax.rsqrt(jnp.mean(o * o, axis=-1, keepdims=True) + NORM_EPS) * ng
        o_ref[rows, :] = (o * _silu(gate_ref[rows, :].astype(F32))).astype(o_ref.dtype)


def _gla(h2, small, w_a2, b_a, norm_g, B, S, *, q_blk0, k_blk0, v_blk0, gate_blk0, rows):
    T = h2.shape[0]
    H = GLA_HEADS
    nR = S // rows
    a_lo = 3 * NSA_KV_GROUPS * NSA_HEADS_PER_GROUP
    wa = jnp.zeros((H, HEAD_DIM, GLA_DK), F32).at[:, a_lo:a_lo + GLA_RANK, :].set(
        w_a2.reshape(GLA_RANK, H, GLA_DK).transpose(1, 0, 2))
    blk = lambda width, blk0: pl.BlockSpec((rows, width), lambda b, h, i: (b * nR + i, blk0 + h))
    per_head = lambda shape: pl.BlockSpec((None,) + shape, lambda b, h, i: (h, 0, 0))
    return pl.pallas_call(
        functools.partial(_gla_kernel, n_chunks=rows // GLA_CHUNK),
        out_shape=jax.ShapeDtypeStruct((T, H * GLA_DV), BF16),
        grid=(B, H, nR),
        in_specs=[blk(GLA_DK, q_blk0), blk(GLA_DK, k_blk0), blk(GLA_DV, v_blk0),
                  pl.BlockSpec((rows, HEAD_DIM), lambda b, h, i: (b * nR + i, 0)), blk(GLA_DV, gate_blk0),
                  per_head((HEAD_DIM, GLA_DK)), per_head((1, GLA_DK)),
                  pl.BlockSpec((1, GLA_DV), lambda b, h, i: (0, 0))],
        out_specs=blk(GLA_DV, 0),
        scratch_shapes=[pltpu.VMEM((GLA_DV, GLA_DK), F32)],
        compiler_params=_params("parallel", "parallel", "arbitrary"),
        name="gla",
    )(h2, h2, h2, small, h2, wa, b_a.reshape(H, 1, GLA_DK), norm_g.reshape(1, GLA_DV))


def _dilated_stream(i, q, kp, kc, vp, vc, prev, gate, o_scr, m_scr, d_scr):
    tq = o_scr.shape[0]
    head = lambda h: slice(h * HEAD_DIM, (h + 1) * HEAD_DIM)
    row = lax.broadcasted_iota(jnp.int32, (tq, 2 * tq), 0)
    col = lax.broadcasted_iota(jnp.int32, (tq, 2 * tq), 1)
    valid = (col >= row + (tq - DIL_BACK)) & (col <= row + tq) & ((col >= tq) | (i > 0))
    m_scr[...] = jnp.zeros(m_scr.shape, F32)
    d_scr[...] = jnp.ones(d_scr.shape, F32)
    for h in range(DIL_HEADS):
        k = jnp.concatenate([kp[:, head(h)], kc[:, head(h)]], axis=0)
        v = jnp.concatenate([vp[:, head(h)], vc[:, head(h)]], axis=0)
        s = lax.dot_general(q[:, head(h)], k, NT_DIMS, preferred_element_type=F32)
        s = jnp.where(valid, s, NEG_INF)
        m = jnp.max(s, axis=1, keepdims=True)
        e = jnp.exp(s - m)
        o_scr[:, head(h)] = jnp.dot(e.astype(BF16), v, preferred_element_type=F32)
        m_scr[:, h:h + 1] = m
        d_scr[:, h:h + 1] = jnp.sum(e, axis=1, keepdims=True)
    den = d_scr[...]
    lse = m_scr[...] + jnp.log(den)
    w_cur = 1.0 / den
    if prev is not None:
        prev_o, lse_p = prev
        lse_n = jnp.maximum(lse_p, lse) + jnp.log(1.0 + jnp.exp(-jnp.abs(lse_p - lse)))
        w_prev = jnp.exp(lse_p - lse_n)
        w_cur = jnp.exp(lse - lse_n) * w_cur
        lse = lse_n
    outs = []
    for h in range(DIL_HEADS):
        o = o_scr[:, head(h)] * w_cur[:, h:h + 1]
        if prev is not None:
            o = o + prev_o[:, head(h)].astype(F32) * w_prev[:, h:h + 1]
        if gate is not None:
            o = o * _silu(gate[:, head(h)].astype(F32))
        outs.append(o)
    return outs, lse


def _dilated_kernel(*refs, has_prev, is_last):
    q_ref, kp_ref, kc_ref, vp_ref, vc_ref = refs[:5]
    rest = list(refs[5:])
    prev = (rest.pop(0), rest.pop(0)) if has_prev else None
    gate_ref = rest.pop(0) if is_last else None
    o_ref = rest.pop(0)
    lse_ref = None if is_last else rest.pop(0)
    o_scr, m_scr, d_scr = rest
    if prev is not None:
        prev = (prev[0], prev[1][...])
    outs, lse = _dilated_stream(pl.program_id(2), q_ref, kp_ref, kc_ref, vp_ref, vc_ref, prev, gate_ref,
                                o_scr, m_scr, d_scr)
    for h, o in enumerate(outs):
        o_ref[:, h * HEAD_DIM:(h + 1) * HEAD_DIM] = o.astype(o_ref.dtype)
    if lse_ref is not None:
        lse_ref[...] = lse


def _dilated_scratch(tq, W):
    return [pltpu.VMEM((tq, W), F32), pltpu.VMEM((tq, HEAD_DIM), F32), pltpu.VMEM((tq, HEAD_DIM), F32)]


def _dilated_group(hg, prev_o, prev_lse, gate, *, tq):
    B, d, L, W3 = hg.shape
    W = W3 // 3
    has_prev, is_last = prev_o is not None, gate is not None
    assert tq == DIL_BACK and L % tq == 0
    cur = lambda col: pl.BlockSpec((None, None, tq, W), lambda b, r, i: (b, r, i, col))
    prv = lambda col: pl.BlockSpec((None, None, tq, W), lambda b, r, i: (b, r, jnp.maximum(i - 1, 0), col))
    tok = lambda width: pl.BlockSpec((None, tq, width), lambda b, r, i: (b, i, r))
    in_specs = [cur(0), prv(1), cur(1), prv(2), cur(2)]
    args = [hg] * 5
    if has_prev:
        in_specs += [tok(W), tok(HEAD_DIM)]
        args += [prev_o, prev_lse]
    if is_last:
        in_specs += [tok(W)]
        args += [gate]
    out_shape = [jax.ShapeDtypeStruct((B, L, d * W), BF16)]
    out_specs = [tok(W)]
    if not is_last:
        out_shape += [jax.ShapeDtypeStruct((B, L, d * HEAD_DIM), F32)]
        out_specs += [tok(HEAD_DIM)]
    res = pl.pallas_call(
        functools.partial(_dilated_kernel, has_prev=has_prev, is_last=is_last),
        out_shape=tuple(out_shape),
        grid=(B, d, L // tq),
        in_specs=in_specs,
        out_specs=tuple(out_specs),
        scratch_shapes=_dilated_scratch(tq, W),
        compiler_params=_params("parallel", "parallel", "arbitrary"),
        name=f"dilated_d{d}",
    )(*args)
    return res[0] if is_last else (res[0], res[1])


def _regroup_kernel(*refs, d, ratio):
    q_ref, kp_ref, kc_ref, vp_ref, vc_ref = refs[:5]
    po_refs = refs[5:5 + ratio]
    pl_refs = refs[5 + ratio:5 + 2 * ratio]
    o_ref, lse_ref, o_scr, m_scr, d_scr, so_scr, sl_scr = refs[5 + 2 * ratio:]
    i = pl.program_id(1)
    r = pl.program_id(2)
    tq, W = o_scr.shape
    rows_c = tq // ratio
    n = d * tq
    shift = lambda v: v.bit_length() - 1
    dst = lax.broadcasted_iota(jnp.int32, (tq, tq), 0)
    src = lax.broadcasted_iota(jnp.int32, (tq, tq), 1)
    gather = jnp.where(src == (dst & (ratio - 1)) * rows_c + (dst >> shift(ratio)), 1.0, 0.0).astype(BF16)
    prev_o = jnp.dot(gather, jnp.concatenate([p[...] for p in po_refs], axis=0), preferred_element_type=F32)
    prev_l = sum(jnp.dot(gather, part, preferred_element_type=F32)
                 for part in _split3(jnp.concatenate([p[...] for p in pl_refs], axis=0)))
    outs, lse = _dilated_stream(i, q_ref, kp_ref, kc_ref, vp_ref, vc_ref, (prev_o, prev_l), None,
                                o_scr, m_scr, d_scr)
    so_scr[r] = jnp.concatenate(outs, axis=1).astype(BF16)
    sl_scr[r] = lse

    @pl.when(r == d - 1)
    def _():
        dst = lax.broadcasted_iota(jnp.int32, (n, n), 0)
        src = lax.broadcasted_iota(jnp.int32, (n, n), 1)
        scatter = jnp.where(src == (dst & (d - 1)) * tq + (dst >> shift(d)), 1.0, 0.0).astype(BF16)
        o_ref[...] = jnp.dot(scatter, so_scr[...].reshape(n, W), preferred_element_type=F32).astype(o_ref.dtype)
        lse_ref[...] = sum(jnp.dot(scatter, part, preferred_element_type=F32)
                           for part in _split3(sl_scr[...].reshape(n, HEAD_DIM)))


def _dilated_regroup(hg, prev_o, prev_lse, *, fine, tq):
    B, d, L, W3 = hg.shape
    W = W3 // 3
    S = L * d
    ratio = fine // d
    assert tq == DIL_BACK and L % tq == 0 and fine % d == 0 and tq % ratio == 0
    assert d & (d - 1) == 0 and ratio & (ratio - 1) == 0
    rows_c = tq // ratio
    cur = lambda col: pl.BlockSpec((None, None, tq, W), lambda b, i, r: (b, r, i, col))
    prv = lambda col: pl.BlockSpec((None, None, tq, W), lambda b, i, r: (b, r, jnp.maximum(i - 1, 0), col))
    piece = lambda width: [pl.BlockSpec((None, rows_c, width), lambda b, i, r, j=j: (b, i, j * d + r))
                           for j in range(ratio)]
    return pl.pallas_call(
        functools.partial(_regroup_kernel, d=d, ratio=ratio),
        out_shape=(jax.ShapeDtypeStruct((B, S, W), BF16), jax.ShapeDtypeStruct((B, S, HEAD_DIM), F32)),
        grid=(B, L // tq, d),
        in_specs=[cur(0), prv(1), cur(1), prv(2), cur(2)] + piece(W) + piece(HEAD_DIM),
        out_specs=(pl.BlockSpec((None, d * tq, W), lambda b, i, r: (b, i, 0)),
                   pl.BlockSpec((None, d * tq, HEAD_DIM), lambda b, i, r: (b, i, 0))),
        scratch_shapes=_dilated_scratch(tq, W) + [pltpu.VMEM((d, tq, W), BF16), pltpu.VMEM((d, tq, HEAD_DIM), F32)],
        compiler_params=_params("parallel", "arbitrary", "arbitrary"),
        name=f"dilated_regroup_d{d}",
    )(*([hg] * 5 + [prev_o] * ratio + [prev_lse] * ratio))


def _row_tile(n, target):
    t = min(n, target)
    assert n % t == 0
    return t


def _even_layer(xb4, w_in, pe_k, w1_k, w2_k, pe_v, w1_v, w2_v, gla_w_a2, gla_b_a, gla_norm_g, w_out):
    B, _, S, D = xb4.shape
    T = B * S
    nq = NSA_KV_GROUPS * NSA_HEADS_PER_GROUP * HEAD_DIM
    nkv = 6 * NSA_KV_GROUPS * HEAD_DIM
    ng = 3 * NSA_KV_GROUPS * NSA_HEADS_PER_GROUP
    gk = GLA_HEADS * GLA_DK
    gv = GLA_HEADS * GLA_DV
    o_g = nq + nkv
    o_a = o_g + ng + nq + 2 * gk + gv
    w_main = jnp.concatenate([w_in[:, :o_g], w_in[:, o_g + ng:o_a], w_in[:, o_a + GLA_RANK:]], axis=1).astype(BF16)
    w_small = jnp.concatenate([w_in[:, o_g:o_g + ng], w_in[:, o_a:o_a + GLA_RANK],
                               jnp.zeros((D, HEAD_DIM - ng - GLA_RANK), w_in.dtype)], axis=1).astype(BF16)
    tn = 512
    kinds = ([EPI_ROPE_QSCALE_LOG2] * (nq // tn)
             + [EPI_PLAIN, EPI_PLAIN, EPI_ROPE, EPI_PLAIN, EPI_ROPE, EPI_PLAIN]
             + [EPI_PLAIN] * (nq // tn) + [EPI_GLA_QSCALE] * (gk // tn) + [EPI_PLAIN] * (gk // tn)
             + [EPI_PLAIN] * (gv // tn) + [EPI_PLAIN] * (gv // tn))
    cos, sin = _rope_tables(jnp.arange(S))
    cos, sin = cos[None], sin[None]
    tm = _row_tile(S, 1024)
    h2 = _project(xb4, w_main, kinds, cos, sin, tm=tm, tn=tn, out_dtype=BF16).reshape(T, -1)
    small = _project(xb4, w_small, [EPI_PLAIN], cos, sin, tm=tm, tn=HEAD_DIM, out_dtype=F32).reshape(T, HEAD_DIM)

    q_tile0 = 0
    kv_block0 = nq // HEAD_DIM
    gate_tile0 = (nq + nkv) // tn
    kc, vct = _nsa_compress(h2, B, S, pe_k, w1_k, w2_k, pe_v, w1_v, w2_v, kv_block0, kv_block0 + NSA_KV_GROUPS)
    v_blocks = [kv_block0 + 4 * c + g for c in (3, 5) for g in range(NSA_KV_GROUPS)]
    vt = _transpose_values(h2, B, S, v_blocks, rows=_row_tile(S, 1024))
    o_nsa = _nsa_attention(h2, small, kc, vct, vt, B, S, q_tile0=q_tile0, kv_block0=kv_block0,
                           gate_tile0=gate_tile0, tq=128, tk=_row_tile(S, 512))
    c0 = nq + nkv + nq
    o_gla = _gla(h2, small, gla_w_a2, gla_b_a, gla_norm_g, B, S,
                 q_blk0=c0 // GLA_DK, k_blk0=(c0 + gk) // GLA_DK, v_blk0=(c0 + 2 * gk) // GLA_DV,
                 gate_blk0=(c0 + 2 * gk + gv) // GLA_DV, rows=_row_tile(S, 512))
    w_out = w_out.astype(BF16)
    return _out_project([o_nsa, o_gla], [w_out[:nq], w_out[nq:]], tm=_row_tile(T, 1024), tn=512)


def _odd_layer(streams, w_in, w_out):
    B, _, S, D = streams[1].shape
    T = B * S
    n_g = len(DIL_GROUPS)
    W = DIL_HEADS * HEAD_DIM
    tn = 512
    cos, sin = _rope_tables(jnp.arange(S))
    gate = _project(streams[1], w_in[:, 3 * n_g * W:].astype(BF16), [EPI_PLAIN] * (W // tn), cos[None], sin[None],
                    tm=_row_tile(S, 1024), tn=tn, out_dtype=BF16).reshape(B, S, W)
    kinds = [EPI_ROPE_QSCALE] * (W // tn) + [EPI_ROPE] * (W // tn) + [EPI_PLAIN] * (W // tn)

    def group_qkv(gi, d):
        L = S // d
        w_g = jnp.concatenate([w_in[:, (c * n_g + gi) * W:(c * n_g + gi + 1) * W] for c in range(3)], axis=1).astype(BF16)
        to_streams = lambda t: t.reshape(L, d, HEAD_DIM).transpose(1, 0, 2)
        return _project(streams[d], w_g, kinds, to_streams(cos), to_streams(sin),
                        tm=_row_tile(L, 1024), tn=tn, out_dtype=BF16)

    order = sorted(range(n_g), key=lambda gi: -DIL_GROUPS[gi][1])
    assert len(order) == 3 and DIL_GROUPS[order[-1]][1] == 1
    for gi in order:
        assert DIL_GROUPS[gi][0] // DIL_GROUPS[gi][1] == DIL_BACK
    d_hi, d_mid = DIL_GROUPS[order[0]][1], DIL_GROUPS[order[1]][1]
    o, lse = _dilated_group(group_qkv(order[0], d_hi), None, None, None, tq=DIL_BACK)
    o, lse = _dilated_regroup(group_qkv(order[1], d_mid), o, lse, fine=d_hi, tq=DIL_BACK)
    o = _dilated_group(group_qkv(order[2], 1), o, lse, gate, tq=DIL_BACK)
    return _out_project([o.reshape(T, W)], [w_out.astype(BF16)], tm=_row_tile(T, 1024), tn=512)


def kernel(x, l0_w_in, l0_nsa_pe_k, l0_nsa_w1_k, l0_nsa_w2_k, l0_nsa_pe_v, l0_nsa_w1_v, l0_nsa_w2_v, l0_gla_w_a2, l0_gla_b_a, l0_gla_norm_g, l0_w_out, l0_ln_g, l0_ln_b, l1_w_in, l1_w_out, l1_ln_g, l1_ln_b):
    B, S, D = x.shape
    tr = _row_tile(S, 256)
    dils = sorted({d for _, d in DIL_GROUPS} | {1})
    y0 = _even_layer(x.astype(BF16)[:, None], l0_w_in, l0_nsa_pe_k, l0_nsa_w1_k, l0_nsa_w2_k, l0_nsa_pe_v, l0_nsa_w1_v,
                     l0_nsa_w2_v, l0_gla_w_a2, l0_gla_b_a, l0_gla_norm_g, l0_w_out)
    x1, *x1_streams = _deepnorm_ln(x, y0, l0_ln_g, l0_ln_b, tr=tr, dilations=dils)
    y1 = _odd_layer(dict(zip(dils, x1_streams)), l1_w_in, l1_w_out)
    return _deepnorm_ln(x1, y1, l1_ln_g, l1_ln_b, tr=tr)[0]
```

```python
import functools

import numpy as np
import jax
import jax.numpy as jnp
from jax import lax
from jax.experimental import pallas as pl
from jax.experimental.pallas import tpu as pltpu

F32 = jnp.float32
BF16 = jnp.bfloat16

HEAD_DIM = 128
ROPE_THETA = 10000.0
LN_EPS = 1e-5
NORM_EPS = 1e-6
NEG_INF = -1e30
M_INIT = -1e29
TAKEN = -3e38
SEL_UNROLL = 4
SEL_SUB = 256
VT_ROWS = HEAD_DIM + 16

NSA_KV_GROUPS = 4
NSA_HEADS_PER_GROUP = 4
NSA_CMP_LEN = 32
NSA_CMP_STRIDE = 16
NSA_SEL_LEN = 64
NSA_N_SEL = 16
NSA_WINDOW = 512
NSA_FORCE_BONUS = 1e4
NSA_BLK_PAD = 128

GLA_HEADS = 4
GLA_DK = 256
GLA_DV = 512
GLA_RANK = 16
GLA_TAU = 16.0
GLA_CHUNK = 64

DIL_GROUPS = ((128, 1), (512, 4), (2048, 16))
DIL_HEADS = 16
DIL_BACK = 128

DEPTH = 2
DEEPNORM_ALPHA = (2.0 * DEPTH) ** 0.25

VMEM_LIMIT_BYTES = 56 * 1024 * 1024

Q_SCALE = HEAD_DIM ** -0.5
LOG2E = 1.4426950408889634
GLA_Q_SCALE = GLA_DK ** -0.5

NT_DIMS = (((1,), (1,)), ((), ()))
TN_DIMS = (((0,), (0,)), ((), ()))


def _params(*sem):
    return pltpu.CompilerParams(dimension_semantics=sem, vmem_limit_bytes=VMEM_LIMIT_BYTES)


def _silu(x):
    return x * (1.0 / (1.0 + jnp.exp(-x)))


def _rope_tables(pos):
    half = HEAD_DIM // 2
    inv_freq = ROPE_THETA ** (-jnp.arange(half, dtype=F32) / half)
    ang = pos.astype(F32)[:, None] * inv_freq[None, :]
    cos = jnp.cos(ang)
    sin = jnp.sin(ang)
    return jnp.concatenate([cos, cos], axis=-1), jnp.concatenate([-sin, sin], axis=-1)


def _rope(a, cos, sin):
    return a * cos + pltpu.roll(a, HEAD_DIM // 2, 1) * sin


def _proj_kernel(wblk_ref, tbase_ref, tstep_ref, a_ref, w_ref, c_ref, s_ref, o_ref):
    acc = jnp.dot(a_ref[...], w_ref[...].astype(BF16), preferred_element_type=F32)
    c = c_ref[...]
    s = s_ref[...]
    for k in range(acc.shape[1] // HEAD_DIM):
        sl = slice(k * HEAD_DIM, (k + 1) * HEAD_DIM)
        o_ref[:, sl] = (acc[:, sl] * c + pltpu.roll(acc[:, sl], HEAD_DIM // 2, 1) * s).astype(o_ref.dtype)


def _project(act4, w, tiles, pos, *, tm, tn, out_dtype):
    B, d, L, K = act4.shape
    nI, nJ = L // tm, len(tiles)
    assert L % tm == 0 and pos.shape == (d, L)
    kinds = sorted({(rope, scale) for _, rope, scale in tiles})
    cos, sin = _rope_tables(pos.reshape(-1))
    c_rows, s_rows, base = [], [], {}
    n_blocks = 0
    for rope, scale in kinds:
        base[(rope, scale)] = n_blocks
        if rope:
            c_rows += [cos * scale]
            s_rows += [sin * scale]
            n_blocks += d * nI
        else:
            c_rows += [jnp.full((tm, HEAD_DIM), scale, F32)]
            s_rows += [jnp.zeros((tm, HEAD_DIM), F32)]
            n_blocks += 1
    c_tab, s_tab = jnp.concatenate(c_rows, axis=0), jnp.concatenate(s_rows, axis=0)
    wblk = jnp.asarray([t[0] for t in tiles], jnp.int32)
    tbase = jnp.asarray([base[(t[1], t[2])] for t in tiles], jnp.int32)
    tstep = jnp.asarray([int(t[1]) for t in tiles], jnp.int32)

    def split(m):
        return m // (d * nI), (m // nI) % d, m % nI

    def a_idx(m, j, wb, tb, ts):
        b, r, i = split(m)
        return (b, r, i, 0)

    def t_idx(m, j, wb, tb, ts):
        b, r, i = split(m)
        return (tb[j] + ts[j] * (r * nI + i), 0)

    def o_idx(m, j, wb, tb, ts):
        b, r, i = split(m)
        return (b, r, i, j)

    grid_spec = pltpu.PrefetchScalarGridSpec(
        num_scalar_prefetch=3,
        grid=(B * d * nI, nJ),
        in_specs=[
            pl.BlockSpec((None, None, tm, K), a_idx),
            pl.BlockSpec((K, tn), lambda m, j, wb, tb, ts: (0, wb[j])),
            pl.BlockSpec((tm, HEAD_DIM), t_idx),
            pl.BlockSpec((tm, HEAD_DIM), t_idx),
        ],
        out_specs=pl.BlockSpec((None, None, tm, tn), o_idx),
    )
    return pl.pallas_call(
        _proj_kernel,
        out_shape=jax.ShapeDtypeStruct((B, d, L, nJ * tn), out_dtype),
        grid_spec=grid_spec,
        compiler_params=_params("parallel", "arbitrary"),
        name=f"proj_d{d}_n{nJ * tn}",
    )(wblk, tbase, tstep, act4, w, c_tab, s_tab)


def _outproj_kernel(*refs):
    o_ref = refs[-1]
    acc = None
    for a_ref, w_ref in zip(refs[0:-1:2], refs[1:-1:2]):
        part = jnp.dot(a_ref[...], w_ref[...].astype(BF16), preferred_element_type=F32)
        acc = part if acc is None else acc + part
    o_ref[...] = acc.astype(o_ref.dtype)


def _out_project(acts, w, *, tm, tn):
    T, K = acts[0].shape
    N = w.shape[1]
    assert all(a.shape == (T, K) for a in acts) and w.shape[0] == len(acts) * K
    in_specs, args = [], []
    for n, a in enumerate(acts):
        in_specs += [pl.BlockSpec((tm, K), lambda i, j: (i, 0)), pl.BlockSpec((K, tn), lambda i, j, n=n: (n, j))]
        args += [a, w]
    return pl.pallas_call(
        _outproj_kernel,
        out_shape=jax.ShapeDtypeStruct((T, N), BF16),
        grid=(T // tm, N // tn),
        in_specs=in_specs,
        out_specs=pl.BlockSpec((tm, tn), lambda i, j: (i, j)),
        compiler_params=_params("parallel", "arbitrary"),
        name=f"outproj_{len(acts)}",
    )(*args)


def _ln_kernel(x_ref, y_ref, g_ref, b_ref, o_ref, *stream_refs, dilations):
    z = DEEPNORM_ALPHA * x_ref[...] + y_ref[...].astype(F32)
    mu = jnp.mean(z, axis=-1, keepdims=True)
    zc = z - mu
    var = jnp.mean(zc * zc, axis=-1, keepdims=True)
    out = zc * lax.rsqrt(var + LN_EPS) * g_ref[...] + b_ref[...]
    o_ref[...] = out
    tr = out.shape[0]
    ob = out.astype(BF16)
    dst = lax.broadcasted_iota(jnp.int32, (tr, tr), 0)
    src = lax.broadcasted_iota(jnp.int32, (tr, tr), 1)
    for d, s_ref in zip(dilations, stream_refs):
        if d == 1:
            s_ref[0] = ob
            continue
        per = tr // d
        perm = jnp.where(src == (dst & (per - 1)) * d + (dst >> (per.bit_length() - 1)), 1.0, 0.0).astype(BF16)
        sm = jnp.dot(perm, ob, preferred_element_type=F32).astype(BF16)
        for r in range(d):
            s_ref[r] = sm[r * per:(r + 1) * per]


def _deepnorm_ln(x3, y2, g, b, *, tr, dilations=()):
    B, S, D = x3.shape
    nI = S // tr
    assert S % tr == 0 and all(tr % (d * 16) == 0 and d & (d - 1) == 0 for d in dilations) and tr & (tr - 1) == 0
    row3 = pl.BlockSpec((None, tr, D), lambda b, i: (b, i, 0))
    vec = pl.BlockSpec((1, D), lambda b, i: (0, 0))
    out_shape = [jax.ShapeDtypeStruct((B, S, D), F32)]
    out_specs = [row3]
    for d in dilations:
        out_shape.append(jax.ShapeDtypeStruct((B, d, S // d, D), BF16))
        out_specs.append(pl.BlockSpec((None, d, tr // d, D), lambda b, i: (b, 0, i, 0)))
    return pl.pallas_call(
        functools.partial(_ln_kernel, dilations=tuple(dilations)),
        out_shape=tuple(out_shape),
        grid=(B, nI),
        in_specs=[row3, pl.BlockSpec((tr, D), lambda b, i: (b * nI + i, 0)), vec, vec],
        out_specs=tuple(out_specs),
        compiler_params=_params("parallel", "parallel"),
        name="deepnorm_ln",
    )(x3, y2, g.reshape(1, D), b.reshape(1, D))


def _compress_kernel(zk_ref, zv_ref, pe_k_ref, pe_v_ref, w1ka_ref, w1kb_ref, w1va_ref, w1vb_ref,
                     w2k_ref, w2vt_ref, cos_ref, sin_ref, kc_ref, vct_ref, uk_ref, vk_ref, uv_ref, vv_ref):
    p = pl.program_id(1)
    n = uk_ref.shape[0]

    @pl.when(p == 0)
    def _():
        for ref in (uk_ref, vk_ref, uv_ref, vv_ref):
            ref[...] = jnp.zeros_like(ref)

    def accumulate(z_ref, pe_ref, wa_ref, wb_ref, u_ref, v_ref):
        z = z_ref[...].astype(F32)
        za = (z + pe_ref[pl.ds(p, 1), :]).astype(BF16)
        zb = (z + pe_ref[pl.ds(p + NSA_CMP_STRIDE, 1), :]).astype(BF16)
        u_ref[...] += jnp.dot(za, wa_ref[...], preferred_element_type=F32)
        v_ref[...] += jnp.dot(zb, wb_ref[...], preferred_element_type=F32)

    accumulate(zk_ref, pe_k_ref, w1ka_ref, w1kb_ref, uk_ref, vk_ref)
    accumulate(zv_ref, pe_v_ref, w1va_ref, w1vb_ref, uv_ref, vv_ref)

    @pl.when(p == NSA_CMP_STRIDE - 1)
    def _():
        hid_k = _silu(uk_ref[...] + pltpu.roll(vk_ref[...], n - 1, 0)).astype(BF16)
        hid_v = _silu(uv_ref[...] + pltpu.roll(vv_ref[...], n - 1, 0)).astype(BF16)
        kc = jnp.dot(hid_k, w2k_ref[...], preferred_element_type=F32)
        kc = _rope(kc, cos_ref[...], sin_ref[...])
        row = lax.broadcasted_iota(jnp.int32, kc.shape, 0)
        kc_ref[...] = jnp.where(row < n - 1, kc, 0.0).astype(kc_ref.dtype)
        vct = lax.dot_general(w2vt_ref[...], hid_v, NT_DIMS, preferred_element_type=F32)
        col = lax.broadcasted_iota(jnp.int32, vct.shape, 1)
        vct_ref[...] = jnp.where(col < n - 1, vct, 0.0).astype(vct_ref.dtype)


def _nsa_compress(h2, B, S, pe_k, w1_k, w2_k, pe_v, w1_v, w2_v, kc_block, vc_block):
    G = NSA_KV_GROUPS
    n = S // NSA_CMP_STRIDE
    c0, c1 = kc_block * HEAD_DIM, (vc_block + G) * HEAD_DIM
    nb = (c1 - c0) // HEAD_DIM
    h3 = h2[:, c0:c1].reshape(B, n, NSA_CMP_STRIDE * (c1 - c0))
    kc_block, vc_block = 0, vc_block - kc_block
    cos, sin = _rope_tables(jnp.arange(n) * NSA_CMP_STRIDE + NSA_CMP_LEN - 1)

    def z_spec(block0):
        return pl.BlockSpec((None, n, HEAD_DIM), lambda bg, p: (bg // G, 0, p * nb + block0 + bg % G))

    full = lambda shape: pl.BlockSpec(shape, lambda bg, p: (0,) * len(shape))
    w1a = pl.BlockSpec((HEAD_DIM, HEAD_DIM), lambda bg, p: (p, 0))
    w1b = pl.BlockSpec((HEAD_DIM, HEAD_DIM), lambda bg, p: (p + NSA_CMP_STRIDE, 0))
    acc = pltpu.VMEM((n, HEAD_DIM), F32)
    return pl.pallas_call(
        _compress_kernel,
        out_shape=(jax.ShapeDtypeStruct((B * G, n, HEAD_DIM), BF16), jax.ShapeDtypeStruct((B * G, HEAD_DIM, n), BF16)),
        grid=(B * G, NSA_CMP_STRIDE),
        in_specs=[z_spec(kc_block), z_spec(vc_block), full((NSA_CMP_LEN, HEAD_DIM)), full((NSA_CMP_LEN, HEAD_DIM)),
                  w1a, w1b, w1a, w1b, full((HEAD_DIM, HEAD_DIM)), full((HEAD_DIM, HEAD_DIM)),
                  full((n, HEAD_DIM)), full((n, HEAD_DIM))],
        out_specs=(pl.BlockSpec((None, n, HEAD_DIM), lambda bg, p: (bg, 0, 0)),
                   pl.BlockSpec((None, HEAD_DIM, n), lambda bg, p: (bg, 0, 0))),
        scratch_shapes=[acc, acc, acc, acc],
        compiler_params=_params("parallel", "arbitrary"),
        name="nsa_compress",
    )(h3, h3, pe_k, pe_v, w1_k.astype(BF16), w1_k.astype(BF16), w1_v.astype(BF16), w1_v.astype(BF16),
      w2_k.astype(BF16), w2_v.T.astype(BF16), cos, sin)


def _split3(x):
    hi = x.astype(BF16)
    r1 = x - hi.astype(F32)
    mid = r1.astype(BF16)
    lo = (r1 - mid.astype(F32)).astype(BF16)
    return hi, mid, lo


def _vt_kernel(blk_ref, v_ref, o_ref):
    extra = VT_ROWS - HEAD_DIM
    ones_row = jnp.where(lax.broadcasted_iota(jnp.int32, (extra, HEAD_DIM), 0) == 0, 1.0, 0.0).astype(o_ref.dtype)
    for c in range(o_ref.shape[0]):
        o_ref[c, 0:HEAD_DIM, :] = v_ref[c * HEAD_DIM:(c + 1) * HEAD_DIM, :].astype(F32).T.astype(o_ref.dtype)
        o_ref[c, HEAD_DIM:VT_ROWS, :] = ones_row


def _transpose_values(h2, B, S, blocks, *, rows):
    ns = S // rows
    per = rows // HEAD_DIM
    nv = len(blocks)
    grid_spec = pltpu.PrefetchScalarGridSpec(
        num_scalar_prefetch=1,
        grid=(B, nv, ns),
        in_specs=[pl.BlockSpec((rows, HEAD_DIM), lambda b, i, s, blk: (b * ns + s, blk[i]))],
        out_specs=pl.BlockSpec((None, None, per, VT_ROWS, HEAD_DIM), lambda b, i, s, blk: (b, i, s, 0, 0)),
    )
    return pl.pallas_call(
        _vt_kernel,
        out_shape=jax.ShapeDtypeStruct((B, nv, S // HEAD_DIM, VT_ROWS, HEAD_DIM), BF16),
        grid_spec=grid_spec,
        compiler_params=_params("parallel", "parallel", "arbitrary"),
        name="transpose_values",
    )(jnp.asarray(blocks, jnp.int32), h2)


def _nsa_kernel(q_ref, kc_ref, vct_ref, ks_ref, vst_ref, kw_ref, vwt_ref, gl_ref, gate_ref, ovt_ref,
                o_ref, sel_ref, sig_ref, *, tq, tk, n_sel):
    g = pl.program_id(1)
    qi = pl.program_id(2)
    t0 = qi * tq
    hg = NSA_HEADS_PER_GROUP
    n_cmp = kc_ref.shape[0]
    nbp = NSA_BLK_PAD
    head = lambda h: slice(h * HEAD_DIM, (h + 1) * HEAD_DIM)
    lanes = lambda h: slice(h * tq, (h + 1) * tq)
    per_head = lambda a: jnp.concatenate([a] * hg, axis=1)
    chunk = HEAD_DIM

    q4 = q_ref[...]
    qs = jnp.concatenate([q4[:, head(h)] for h in range(hg)], axis=0)

    def softmax_rows(s, m_floor):
        m = jnp.maximum(jnp.max(s, axis=0, keepdims=True), m_floor)
        e = jnp.exp2(s - m)
        den = jnp.sum(e, axis=0, keepdims=True)
        return e, jnp.where(den > 0.0, 1.0 / den, 0.0)

    t_lane = t0 + lax.broadcasted_iota(jnp.int32, (n_cmp, tq), 1)
    cmp_end = lax.broadcasted_iota(jnp.int32, (n_cmp, tq), 0) * NSA_CMP_STRIDE + (NSA_CMP_LEN - 1)
    cbias = per_head(jnp.where(cmp_end <= t_lane, 0.0, NEG_INF))
    s = lax.dot_general(kc_ref[...], qs, NT_DIMS, preferred_element_type=F32) + cbias
    e, inv = softmax_rows(s, M_INIT)
    p = e * inv
    psum = sum(p[:, lanes(h)] for h in range(hg))
    o_ct = jnp.dot(vct_ref[...], p.astype(BF16), preferred_element_type=F32)

    ovt = ovt_ref[...]
    imp = sum(jnp.dot(ovt, part, preferred_element_type=F32) for part in _split3(psum))
    j_idx = lax.broadcasted_iota(jnp.int32, (nbp, tq), 0)
    cur = (t0 + lax.broadcasted_iota(jnp.int32, (nbp, tq), 1)) >> 6
    forced = (j_idx == 0) | (j_idx == cur) | (j_idx == cur - 1)
    imp = jnp.where(forced, imp + NSA_FORCE_BONUS, imp)
    imp = jnp.where(j_idx <= cur, imp, NEG_INF)

    j_f = j_idx.astype(F32)
    picked = jnp.zeros((nbp, tq), F32)
    for _ in range(n_sel):
        top = jnp.max(imp, axis=0, keepdims=True)
        first = jnp.min(jnp.where(imp == top, j_f, float(nbp)), axis=0, keepdims=True)
        hit = j_f == first
        picked = jnp.where(hit, 1.0, picked)
        imp = jnp.where(hit, TAKEN, imp)
    sel_ref[...] = jnp.where(j_idx <= cur, picked, 0.0)

    n_pairs = hg // 2
    key_i = lax.broadcasted_iota(jnp.int32, (SEL_SUB, tq), 0)
    t_i = t0 + lax.broadcasted_iota(jnp.int32, (SEL_SUB, tq), 1)
    blocks_per_tile = tk // NSA_SEL_LEN
    blocks_per_sub = SEL_SUB // NSA_SEL_LEN
    q_pair = [qs[pr * 2 * tq:(pr + 1) * 2 * tq] for pr in range(n_pairs)]

    def sel_scores(kt):
        k0 = pl.multiple_of(kt * tk, tk)
        sel8 = sel_ref[pl.ds(pl.multiple_of(kt * blocks_per_tile, blocks_per_tile), blocks_per_tile), :]
        scores = []
        for sub in range(tk // SEL_SUB):
            ks0 = k0 + sub * SEL_SUB
            k = ks_ref[pl.ds(pl.multiple_of(ks0, SEL_SUB), SEL_SUB), :]
            chosen = jnp.concatenate([jnp.broadcast_to(sel8[sub * blocks_per_sub + b:sub * blocks_per_sub + b + 1, :],
                                                       (NSA_SEL_LEN, tq)) for b in range(blocks_per_sub)], axis=0)
            bias = jnp.where((chosen > 0.5) & (key_i + ks0 <= t_i), 0.0, NEG_INF)
            bias2 = jnp.concatenate([bias, bias], axis=1)
            for pr in range(n_pairs):
                scores.append(lax.dot_general(k, q_pair[pr], NT_DIMS, preferred_element_type=F32) + bias2)
        return tuple(scores)

    def sel_tiles(kt0, count, carry):
        m, acc = list(carry[0]), list(carry[1])
        scores = [sel_scores(kt0 + u) for u in range(count)]
        for u in range(count):
            for sub in range(tk // SEL_SUB):
                c0 = (kt0 + u) * (tk // chunk) + sub * (SEL_SUB // chunk)
                vt = jnp.concatenate([vst_ref[c0 + c] for c in range(SEL_SUB // chunk)], axis=1)
                for pr in range(n_pairs):
                    s = scores[u][sub * n_pairs + pr]
                    m_new = jnp.maximum(m[pr], jnp.max(s, axis=0, keepdims=True))
                    p = jnp.exp2(s - m_new).astype(BF16)
                    acc[pr] = jnp.exp2(m[pr] - m_new) * acc[pr] + jnp.dot(vt, p, preferred_element_type=F32)
                    m[pr] = m_new
        return tuple(m), tuple(acc)

    n_tiles = (t0 + tq + tk - 1) // tk
    n_groups = n_tiles // SEL_UNROLL
    carry = (tuple(jnp.full((1, 2 * tq), M_INIT, F32) for _ in range(n_pairs)),
             tuple(jnp.zeros((VT_ROWS, 2 * tq), F32) for _ in range(n_pairs)))
    carry = lax.fori_loop(0, n_groups, lambda it, c: sel_tiles(it * SEL_UNROLL, SEL_UNROLL, c), carry)
    _, acc_pairs = lax.fori_loop(n_groups * SEL_UNROLL, n_tiles, lambda kt, c: sel_tiles(kt, 1, c), carry)
    acc = jnp.concatenate(acc_pairs, axis=1)
    l_s = acc[HEAD_DIM:HEAD_DIM + 1, :]
    o_st = acc[:HEAD_DIM, :] * jnp.where(l_s > 0.0, 1.0 / l_s, 0.0)

    band = NSA_WINDOW + tq
    wc = jnp.maximum(qi - NSA_WINDOW // tq, 0)
    w0 = pl.multiple_of(wc * tq, tq)
    kb = kw_ref[pl.ds(w0, band), :]
    vbt = jnp.concatenate([vwt_ref[wc + c, 0:HEAD_DIM, :] for c in range(band // chunk)], axis=1)
    rel = (t0 + lax.broadcasted_iota(jnp.int32, (band, tq), 1)) - (w0 + lax.broadcasted_iota(jnp.int32, (band, tq), 0))
    wbias = per_head(jnp.where((rel >= 0) & (rel < NSA_WINDOW), 0.0, NEG_INF))
    s = lax.dot_general(kb, qs, NT_DIMS, preferred_element_type=F32) + wbias
    e, inv = softmax_rows(s, M_INIT)
    o_wt = jnp.dot(vbt, e.astype(BF16), preferred_element_type=F32) * inv

    sig_ref[...] = (1.0 / (1.0 + jnp.exp(-gl_ref[...]))).T
    gate = gate_ref[...].astype(F32)
    for h in range(hg):
        base = (g * hg + h) * 3
        gc, gs, gw = (sig_ref[pl.ds(base + c, 1), :] for c in range(3))
        o_t = gc * o_ct[:, lanes(h)] + gs * o_st[:, lanes(h)] + gw * o_wt[:, lanes(h)]
        o_ref[:, head(h)] = (o_t.T * _silu(gate[:, head(h)])).astype(o_ref.dtype)


def _nsa_attention(h2, small, kc, vct, vt, B, S, *, q_tile0, kv_block0, gate_tile0, tq, tk):
    T = h2.shape[0]
    G, hg = NSA_KV_GROUPS, NSA_HEADS_PER_GROUP
    n_cmp = S // NSA_CMP_STRIDE
    n_blk = S // NSA_SEL_LEN
    nQ = S // tq
    gw = hg * HEAD_DIM
    assert n_blk <= NSA_BLK_PAD and S % tk == 0 and S >= NSA_WINDOW + tq and tq == HEAD_DIM
    assert tk % SEL_SUB == 0 and (tk // NSA_SEL_LEN) % 8 == 0 and hg % 2 == 0

    nn = np.arange(n_cmp)[None, :]
    jj = np.arange(NSA_BLK_PAD)[:, None]
    ovt = ((nn * NSA_CMP_STRIDE + NSA_CMP_LEN - 1 >= jj * NSA_SEL_LEN) & (nn * NSA_CMP_STRIDE < (jj + 1) * NSA_SEL_LEN)
           & (nn < n_cmp - 1) & (jj < n_blk))

    def k_spec(c):
        return pl.BlockSpec((S, HEAD_DIM), lambda b, g, i: (b, kv_block0 + 4 * c + g))

    def vt_spec(which):
        return pl.BlockSpec((None, None, S // HEAD_DIM, VT_ROWS, HEAD_DIM), lambda b, g, i: (b, which * G + g, 0, 0, 0))

    rows = lambda width, col: pl.BlockSpec((tq, width), lambda b, g, i: (b * nQ + i, col(g)))
    per_bg = lambda shape: pl.BlockSpec((None,) + shape, lambda b, g, i: (b * G + g, 0, 0))
    const = lambda shape: pl.BlockSpec(shape, lambda b, g, i: (0,) * len(shape))
    return pl.pallas_call(
        functools.partial(_nsa_kernel, tq=tq, tk=tk, n_sel=min(NSA_N_SEL, n_blk)),
        out_shape=jax.ShapeDtypeStruct((T, G * gw), BF16),
        grid=(B, G, nQ),
        in_specs=[rows(gw, lambda g: q_tile0 + g), per_bg((n_cmp, HEAD_DIM)), per_bg((HEAD_DIM, n_cmp)),
                  k_spec(2), vt_spec(0), k_spec(4), vt_spec(1),
                  rows(HEAD_DIM, lambda g: 0), rows(gw, lambda g: gate_tile0 + g),
                  const((NSA_BLK_PAD, n_cmp))],
        out_specs=rows(gw, lambda g: g),
        scratch_shapes=[pltpu.VMEM((NSA_BLK_PAD, tq), F32), pltpu.VMEM((HEAD_DIM, tq), F32)],
        compiler_params=_params("parallel", "parallel", "arbitrary"),
        name="nsa_attention",
    )(h2, kc, vct, h2, vt, h2, vt, small, h2, jnp.asarray(ovt, BF16))


def _gla_kernel(q_ref, k_ref, v_ref, a_ref, gate_ref, wa_ref, ba_ref, ng_ref, o_ref, state_ref, *, n_chunks, n_heads):
    C = GLA_CHUNK

    @pl.when(pl.program_id(2) == 0)
    def _():
        state_ref[...] = jnp.zeros_like(state_ref)

    r_i = lax.broadcasted_iota(jnp.int32, (C, C), 0)
    c_i = lax.broadcasted_iota(jnp.int32, (C, C), 1)
    causal = r_i >= c_i
    tri = jnp.where(causal, 1.0, 0.0).astype(BF16)
    ng = ng_ref[...]
    for c in range(n_chunks):
        rows = slice(c * C, (c + 1) * C)
        a = a_ref[rows, :]
        for h in range(n_heads):
            kcols = slice(h * GLA_DK, (h + 1) * GLA_DK)
            vcols = slice(h * GLA_DV, (h + 1) * GLA_DV)
            q = q_ref[rows, kcols].astype(F32)
            k = k_ref[rows, kcols].astype(F32)
            v = v_ref[rows, vcols]
            pre = jnp.dot(a, wa_ref[h], preferred_element_type=F32) + ba_ref[h]
            log_a = (jnp.minimum(pre, 0.0) - jnp.log(1.0 + jnp.exp(-jnp.abs(pre)))) * (1.0 / GLA_TAU)
            bcum = sum(jnp.dot(tri, part, preferred_element_type=F32) for part in _split3(log_a))
            b_last = bcum[C - 1:C, :]
            qe = (q * jnp.exp(bcum)).astype(BF16)
            ke = (k * jnp.exp(-bcum)).astype(BF16)
            kd = (k * jnp.exp(b_last - bcum)).astype(BF16)
            attn = lax.dot_general(qe, ke, NT_DIMS, preferred_element_type=F32)
            attn = jnp.where(causal, attn, 0.0).astype(BF16)
            state_t = state_ref[h]
            o = jnp.dot(attn, v, preferred_element_type=F32)
            o = o + lax.dot_general(qe, state_t.astype(BF16), NT_DIMS, preferred_element_type=F32)
            state_ref[h] = state_t * jnp.exp(b_last) + lax.dot_general(v, kd, TN_DIMS, preferred_element_type=F32)
            o = o * lax.rsqrt(jnp.mean(o * o, axis=-1, keepdims=True) + NORM_EPS) * ng
            o_ref[rows, vcols] = (o * _silu(gate_ref[rows, vcols].astype(F32))).astype(o_ref.dtype)


def _gla(h2, small, w_a2, b_a, norm_g, B, S, *, q_col0, k_col0, v_col0, gate_col0, rows, heads_per_step):
    T = h2.shape[0]
    H, hp = GLA_HEADS, heads_per_step
    nR = S // rows
    a_lo = 3 * NSA_KV_GROUPS * NSA_HEADS_PER_GROUP
    assert H % hp == 0 and all(c % (hp * w) == 0 for c, w in ((q_col0, GLA_DK), (k_col0, GLA_DK), (v_col0, GLA_DV),
                                                              (gate_col0, GLA_DV)))
    wa = jnp.zeros((H, HEAD_DIM, GLA_DK), F32).at[:, a_lo:a_lo + GLA_RANK, :].set(
        w_a2.reshape(GLA_RANK, H, GLA_DK).transpose(1, 0, 2))
    blk = lambda width, col0: pl.BlockSpec((rows, hp * width), lambda b, g, i: (b * nR + i, col0 // (hp * width) + g))
    per_head = lambda shape: pl.BlockSpec((hp,) + shape, lambda b, g, i: (g, 0, 0))
    return pl.pallas_call(
        functools.partial(_gla_kernel, n_chunks=rows // GLA_CHUNK, n_heads=hp),
        out_shape=jax.ShapeDtypeStruct((T, H * GLA_DV), BF16),
        grid=(B, H // hp, nR),
        in_specs=[blk(GLA_DK, q_col0), blk(GLA_DK, k_col0), blk(GLA_DV, v_col0),
                  pl.BlockSpec((rows, HEAD_DIM), lambda b, g, i: (b * nR + i, 0)), blk(GLA_DV, gate_col0),
                  per_head((HEAD_DIM, GLA_DK)), per_head((1, GLA_DK)),
                  pl.BlockSpec((1, GLA_DV), lambda b, g, i: (0, 0))],
        out_specs=blk(GLA_DV, 0),
        scratch_shapes=[pltpu.VMEM((hp, GLA_DV, GLA_DK), F32)],
        compiler_params=_params("parallel", "parallel", "arbitrary"),
        name="gla",
    )(h2, h2, h2, small, h2, wa, b_a.reshape(H, 1, GLA_DK), norm_g.reshape(1, GLA_DV))


def _dilated_stream(i, q, kp, kc, vp, vc, prev, gate, o_scr, m_scr, d_scr):
    tq = o_scr.shape[0]
    head = lambda h: slice(h * HEAD_DIM, (h + 1) * HEAD_DIM)
    row = lax.broadcasted_iota(jnp.int32, (tq, 2 * tq), 0)
    col = lax.broadcasted_iota(jnp.int32, (tq, 2 * tq), 1)
    valid = (col >= row + (tq - DIL_BACK)) & (col <= row + tq) & ((col >= tq) | (i > 0))
    m_scr[...] = jnp.zeros(m_scr.shape, F32)
    d_scr[...] = jnp.ones(d_scr.shape, F32)
    for h in range(DIL_HEADS):
        k = jnp.concatenate([kp[:, head(h)], kc[:, head(h)]], axis=0)
        v = jnp.concatenate([vp[:, head(h)], vc[:, head(h)]], axis=0)
        s = lax.dot_general(q[:, head(h)], k, NT_DIMS, preferred_element_type=F32)
        s = jnp.where(valid, s, NEG_INF)
        m = jnp.max(s, axis=1, keepdims=True)
        e = jnp.exp(s - m)
        o_scr[:, head(h)] = jnp.dot(e.astype(BF16), v, preferred_element_type=F32)
        m_scr[:, h:h + 1] = m
        d_scr[:, h:h + 1] = jnp.sum(e, axis=1, keepdims=True)
    den = d_scr[...]
    lse = m_scr[...] + jnp.log(den)
    w_cur = 1.0 / den
    if prev is not None:
        prev_o, lse_p = prev
        lse_n = jnp.maximum(lse_p, lse) + jnp.log(1.0 + jnp.exp(-jnp.abs(lse_p - lse)))
        w_prev = jnp.exp(lse_p - lse_n)
        w_cur = jnp.exp(lse - lse_n) * w_cur
        lse = lse_n
    outs = []
    for h in range(DIL_HEADS):
        o = o_scr[:, head(h)] * w_cur[:, h:h + 1]
        if prev is not None:
            o = o + prev_o[:, head(h)].astype(F32) * w_prev[:, h:h + 1]
        if gate is not None:
            o = o * _silu(gate[:, head(h)].astype(F32))
        outs.append(o)
    return outs, lse


def _dilated_kernel(*refs, has_prev, is_last):
    q_ref, kp_ref, kc_ref, vp_ref, vc_ref = refs[:5]
    rest = list(refs[5:])
    prev = (rest.pop(0), rest.pop(0)) if has_prev else None
    gate_ref = rest.pop(0) if is_last else None
    o_ref = rest.pop(0)
    lse_ref = None if is_last else rest.pop(0)
    o_scr, m_scr, d_scr = rest
    if prev is not None:
        prev = (prev[0], prev[1][...])
    outs, lse = _dilated_stream(pl.program_id(2), q_ref, kp_ref, kc_ref, vp_ref, vc_ref, prev, gate_ref,
                                o_scr, m_scr, d_scr)
    for h, o in enumerate(outs):
        o_ref[:, h * HEAD_DIM:(h + 1) * HEAD_DIM] = o.astype(o_ref.dtype)
    if lse_ref is not None:
        lse_ref[...] = lse


def _dilated_scratch(tq, W):
    return [pltpu.VMEM((tq, W), F32), pltpu.VMEM((tq, HEAD_DIM), F32), pltpu.VMEM((tq, HEAD_DIM), F32)]


def _dilated_group(hg, prev_o, prev_lse, gate, *, tq):
    B, d, L, W3 = hg.shape
    W = W3 // 3
    has_prev, is_last = prev_o is not None, gate is not None
    assert tq == DIL_BACK and L % tq == 0
    cur = lambda col: pl.BlockSpec((None, None, tq, W), lambda b, r, i: (b, r, i, col))
    prv = lambda col: pl.BlockSpec((None, None, tq, W), lambda b, r, i: (b, r, jnp.maximum(i - 1, 0), col))
    tok = lambda width: pl.BlockSpec((None, tq, width), lambda b, r, i: (b, i, r))
    in_specs = [cur(0), prv(1), cur(1), prv(2), cur(2)]
    args = [hg] * 5
    if has_prev:
        in_specs += [tok(W), tok(HEAD_DIM)]
        args += [prev_o, prev_lse]
    if is_last:
        in_specs += [tok(W)]
        args += [gate]
    out_shape = [jax.ShapeDtypeStruct((B, L, d * W), BF16)]
    out_specs = [tok(W)]
    if not is_last:
        out_shape += [jax.ShapeDtypeStruct((B, L, d * HEAD_DIM), F32)]
        out_specs += [tok(HEAD_DIM)]
    res = pl.pallas_call(
        functools.partial(_dilated_kernel, has_prev=has_prev, is_last=is_last),
        out_shape=tuple(out_shape),
        grid=(B, d, L // tq),
        in_specs=in_specs,
        out_specs=tuple(out_specs),
        scratch_shapes=_dilated_scratch(tq, W),
        compiler_params=_params("parallel", "parallel", "arbitrary"),
        name=f"dilated_d{d}",
    )(*args)
    return res[0] if is_last else (res[0], res[1])


def _regroup_kernel(*refs, d, ratio):
    q_ref, kp_ref, kc_ref, vp_ref, vc_ref = refs[:5]
    po_refs = refs[5:5 + ratio]
    pl_refs = refs[5 + ratio:5 + 2 * ratio]
    o_ref, lse_ref, o_scr, m_scr, d_scr, so_scr, sl_scr = refs[5 + 2 * ratio:]
    i = pl.program_id(1)
    r = pl.program_id(2)
    tq, W = o_scr.shape
    rows_c = tq // ratio
    n = d * tq
    shift = lambda v: v.bit_length() - 1
    dst = lax.broadcasted_iota(jnp.int32, (tq, tq), 0)
    src = lax.broadcasted_iota(jnp.int32, (tq, tq), 1)
    gather = jnp.where(src == (dst & (ratio - 1)) * rows_c + (dst >> shift(ratio)), 1.0, 0.0).astype(BF16)
    prev_o = jnp.dot(gather, jnp.concatenate([p[...] for p in po_refs], axis=0), preferred_element_type=F32)
    prev_l = sum(jnp.dot(gather, part, preferred_element_type=F32)
                 for part in _split3(jnp.concatenate([p[...] for p in pl_refs], axis=0)))
    outs, lse = _dilated_stream(i, q_ref, kp_ref, kc_ref, vp_ref, vc_ref, (prev_o, prev_l), None,
                                o_scr, m_scr, d_scr)
    so_scr[r] = jnp.concatenate(outs, axis=1).astype(BF16)
    sl_scr[r] = lse

    @pl.when(r == d - 1)
    def _():
        dst = lax.broadcasted_iota(jnp.int32, (n, n), 0)
        src = lax.broadcasted_iota(jnp.int32, (n, n), 1)
        scatter = jnp.where(src == (dst & (d - 1)) * tq + (dst >> shift(d)), 1.0, 0.0).astype(BF16)
        o_ref[...] = jnp.dot(scatter, so_scr[...].reshape(n, W), preferred_element_type=F32).astype(o_ref.dtype)
        lse_ref[...] = sum(jnp.dot(scatter, part, preferred_element_type=F32)
                           for part in _split3(sl_scr[...].reshape(n, HEAD_DIM)))


def _dilated_regroup(hg, prev_o, prev_lse, *, fine, tq):
    B, d, L, W3 = hg.shape
    W = W3 // 3
    S = L * d
    ratio = fine // d
    assert tq == DIL_BACK and L % tq == 0 and fine % d == 0 and tq % ratio == 0
    assert d & (d - 1) == 0 and ratio & (ratio - 1) == 0
    rows_c = tq // ratio
    cur = lambda col: pl.BlockSpec((None, None, tq, W), lambda b, i, r: (b, r, i, col))
    prv = lambda col: pl.BlockSpec((None, None, tq, W), lambda b, i, r: (b, r, jnp.maximum(i - 1, 0), col))
    piece = lambda width: [pl.BlockSpec((None, rows_c, width), lambda b, i, r, j=j: (b, i, j * d + r))
                           for j in range(ratio)]
    return pl.pallas_call(
        functools.partial(_regroup_kernel, d=d, ratio=ratio),
        out_shape=(jax.ShapeDtypeStruct((B, S, W), BF16), jax.ShapeDtypeStruct((B, S, HEAD_DIM), F32)),
        grid=(B, L // tq, d),
        in_specs=[cur(0), prv(1), cur(1), prv(2), cur(2)] + piece(W) + piece(HEAD_DIM),
        out_specs=(pl.BlockSpec((None, d * tq, W), lambda b, i, r: (b, i, 0)),
                   pl.BlockSpec((None, d * tq, HEAD_DIM), lambda b, i, r: (b, i, 0))),
        scratch_shapes=_dilated_scratch(tq, W) + [pltpu.VMEM((d, tq, W), BF16), pltpu.VMEM((d, tq, HEAD_DIM), F32)],
        compiler_params=_params("parallel", "arbitrary", "arbitrary"),
        name=f"dilated_regroup_d{d}",
    )(*([hg] * 5 + [prev_o] * ratio + [prev_lse] * ratio))


def _row_tile(n, target):
    t = min(n, target)
    assert n % t == 0
    return t


def _even_layer(xb4, w_in, pe_k, w1_k, w2_k, pe_v, w1_v, w2_v, gla_w_a2, gla_b_a, gla_norm_g, w_out):
    B, _, S, D = xb4.shape
    T = B * S
    nq = NSA_KV_GROUPS * NSA_HEADS_PER_GROUP * HEAD_DIM
    nkv = 6 * NSA_KV_GROUPS * HEAD_DIM
    ng = 3 * NSA_KV_GROUPS * NSA_HEADS_PER_GROUP
    gk = GLA_HEADS * GLA_DK
    gv = GLA_HEADS * GLA_DV
    o_g = nq + nkv
    o_a = o_g + ng + nq + 2 * gk + gv
    w_main = jnp.concatenate([w_in[:, :o_g], w_in[:, o_g + ng:o_a], w_in[:, o_a + GLA_RANK:]], axis=1).astype(BF16)
    w_small = jnp.concatenate([w_in[:, o_g:o_g + ng], w_in[:, o_a:o_a + GLA_RANK],
                               jnp.zeros((D, HEAD_DIM - ng - GLA_RANK), w_in.dtype)], axis=1).astype(BF16)
    tn = 512
    plain, rope = (False, 1.0), (True, 1.0)
    tile_kinds = ([(True, Q_SCALE * LOG2E)] * (nq // tn)
                  + [plain, plain, rope, plain, rope, plain]
                  + [plain] * (nq // tn) + [(False, GLA_Q_SCALE)] * (gk // tn) + [plain] * (gk // tn)
                  + [plain] * (gv // tn) + [plain] * (gv // tn))
    pos = jnp.arange(S)[None]
    tm = _row_tile(S, 1024)
    h2 = _project(xb4, w_main, [(j,) + k for j, k in enumerate(tile_kinds)], pos, tm=tm, tn=tn,
                  out_dtype=BF16).reshape(T, -1)
    small = _project(xb4, w_small, [(0,) + plain], pos, tm=tm, tn=HEAD_DIM, out_dtype=F32).reshape(T, HEAD_DIM)

    q_tile0 = 0
    kv_block0 = nq // HEAD_DIM
    gate_tile0 = (nq + nkv) // tn
    kc, vct = _nsa_compress(h2, B, S, pe_k, w1_k, w2_k, pe_v, w1_v, w2_v, kv_block0, kv_block0 + NSA_KV_GROUPS)
    v_blocks = [kv_block0 + 4 * c + g for c in (3, 5) for g in range(NSA_KV_GROUPS)]
    vt = _transpose_values(h2, B, S, v_blocks, rows=_row_tile(S, 1024))
    o_nsa = _nsa_attention(h2, small, kc, vct, vt, B, S, q_tile0=q_tile0, kv_block0=kv_block0,
                           gate_tile0=gate_tile0, tq=128, tk=_row_tile(S, 512))
    c0 = nq + nkv + nq
    o_gla = _gla(h2, small, gla_w_a2, gla_b_a, gla_norm_g, B, S, q_col0=c0, k_col0=c0 + gk, v_col0=c0 + 2 * gk,
                 gate_col0=c0 + 2 * gk + gv, rows=_row_tile(S, 512), heads_per_step=2)
    return _out_project([o_nsa, o_gla], w_out, tm=_row_tile(T, 1024), tn=512)


def _odd_layer(streams, w_in, w_out):
    B, _, S, D = streams[1].shape
    T = B * S
    n_g = len(DIL_GROUPS)
    W = DIL_HEADS * HEAD_DIM
    tn = 512
    blocks = lambda col0: [col0 // tn + t for t in range(W // tn)]
    gate = _project(streams[1], w_in, [(blk, False, 1.0) for blk in blocks(3 * n_g * W)], jnp.arange(S)[None],
                    tm=_row_tile(S, 1024), tn=tn, out_dtype=BF16).reshape(B, S, W)

    def group_qkv(gi, d):
        L = S // d
        tiles = ([(blk, True, Q_SCALE) for blk in blocks(gi * W)] + [(blk, True, 1.0) for blk in blocks((n_g + gi) * W)]
                 + [(blk, False, 1.0) for blk in blocks((2 * n_g + gi) * W)])
        return _project(streams[d], w_in, tiles, jnp.arange(S).reshape(L, d).T, tm=_row_tile(L, 1024), tn=tn,
                        out_dtype=BF16)

    order = sorted(range(n_g), key=lambda gi: -DIL_GROUPS[gi][1])
    assert len(order) == 3 and DIL_GROUPS[order[-1]][1] == 1
    for gi in order:
        assert DIL_GROUPS[gi][0] // DIL_GROUPS[gi][1] == DIL_BACK
    d_hi, d_mid = DIL_GROUPS[order[0]][1], DIL_GROUPS[order[1]][1]
    o, lse = _dilated_group(group_qkv(order[0], d_hi), None, None, None, tq=DIL_BACK)
    o, lse = _dilated_regroup(group_qkv(order[1], d_mid), o, lse, fine=d_hi, tq=DIL_BACK)
    o = _dilated_group(group_qkv(order[2], 1), o, lse, gate, tq=DIL_BACK)
    return _out_project([o.reshape(T, W)], w_out, tm=_row_tile(T, 1024), tn=512)


def kernel(x, l0_w_in, l0_nsa_pe_k, l0_nsa_w1_k, l0_nsa_w2_k, l0_nsa_pe_v, l0_nsa_w1_v, l0_nsa_w2_v, l0_gla_w_a2, l0_gla_b_a, l0_gla_norm_g, l0_w_out, l0_ln_g, l0_ln_b, l1_w_in, l1_w_out, l1_ln_g, l1_ln_b):
    B, S, D = x.shape
    tr = _row_tile(S, 256)
    dils = sorted({d for _, d in DIL_GROUPS} | {1})
    y0 = _even_layer(x.astype(BF16)[:, None], l0_w_in, l0_nsa_pe_k, l0_nsa_w1_k, l0_nsa_w2_k, l0_nsa_pe_v, l0_nsa_w1_v,
                     l0_nsa_w2_v, l0_gla_w_a2, l0_gla_b_a, l0_gla_norm_g, l0_w_out)
    x1, *x1_streams = _deepnorm_ln(x, y0, l0_ln_g, l0_ln_b, tr=tr, dilations=dils)
    y1 = _odd_layer(dict(zip(dils, x1_streams)), l1_w_in, l1_w_out)
    return _deepnorm_ln(x1, y1, l1_ln_g, l1_ln_b, tr=tr)[0]
```

```python
import functools

import numpy as np
import jax
import jax.numpy as jnp
from jax import lax
from jax.experimental import pallas as pl
from jax.experimental.pallas import tpu as pltpu

F32 = jnp.float32
BF16 = jnp.bfloat16

HEAD_DIM = 128
ROPE_THETA = 10000.0
LN_EPS = 1e-5
NORM_EPS = 1e-6
NEG_INF = -1e30
M_INIT = -1e29
TAKEN = -3e38
SEL_UNROLL = 4
SEL_SUB = 256
VT_ROWS = HEAD_DIM + 16

NSA_KV_GROUPS = 4
NSA_HEADS_PER_GROUP = 4
NSA_CMP_LEN = 32
NSA_CMP_STRIDE = 16
NSA_SEL_LEN = 64
NSA_N_SEL = 16
NSA_WINDOW = 512
NSA_FORCE_BONUS = 1e4
NSA_BLK_PAD = 128

GLA_HEADS = 4
GLA_DK = 256
GLA_DV = 512
GLA_RANK = 16
GLA_TAU = 16.0
GLA_CHUNK = 64

DIL_GROUPS = ((128, 1), (512, 4), (2048, 16))
DIL_HEADS = 16
DIL_BACK = 128

DEPTH = 2
DEEPNORM_ALPHA = (2.0 * DEPTH) ** 0.25

VMEM_LIMIT_BYTES = 56 * 1024 * 1024

Q_SCALE = HEAD_DIM ** -0.5
LOG2E = 1.4426950408889634
GLA_Q_SCALE = GLA_DK ** -0.5

NT_DIMS = (((1,), (1,)), ((), ()))
TN_DIMS = (((0,), (0,)), ((), ()))


def _params(*sem):
    return pltpu.CompilerParams(dimension_semantics=sem, vmem_limit_bytes=VMEM_LIMIT_BYTES)


def _silu(x):
    return x * (1.0 / (1.0 + jnp.exp(-x)))


def _rope_tables(pos):
    half = HEAD_DIM // 2
    inv_freq = ROPE_THETA ** (-jnp.arange(half, dtype=F32) / half)
    ang = pos.astype(F32)[:, None] * inv_freq[None, :]
    cos = jnp.cos(ang)
    sin = jnp.sin(ang)
    return jnp.concatenate([cos, cos], axis=-1), jnp.concatenate([-sin, sin], axis=-1)


def _rope(a, cos, sin):
    return a * cos + pltpu.roll(a, HEAD_DIM // 2, 1) * sin


def _proj_kernel(wblk_ref, tbase_ref, tstep_ref, a_ref, w_ref, c_ref, s_ref, o_ref, *, n_split):
    w = w_ref[...].astype(BF16)
    sb, tm, _ = a_ref.shape
    chunk = tm // n_split
    for st in range(sb):
        for h in range(n_split):
            rows = slice(h * chunk, (h + 1) * chunk)
            trows = slice(st * tm + h * chunk, st * tm + (h + 1) * chunk)
            acc = jnp.dot(a_ref[st, rows, :], w, preferred_element_type=F32)
            c = c_ref[trows, :]
            s = s_ref[trows, :]
            for k in range(acc.shape[1] // HEAD_DIM):
                sl = slice(k * HEAD_DIM, (k + 1) * HEAD_DIM)
                o_ref[st, rows, sl] = (acc[:, sl] * c + pltpu.roll(acc[:, sl], HEAD_DIM // 2, 1) * s).astype(o_ref.dtype)


def _project(act4, w, tiles, pos, *, tm, tn, out_dtype, streams_per_step=1, n_split=4):
    B, d, L, K = act4.shape
    sb = streams_per_step
    nI, nJ, nR = L // tm, len(tiles), d // sb
    assert L % tm == 0 and pos.shape == (d, L) and d % sb == 0 and (sb == 1 or nI == 1) and tm % n_split == 0
    tr = sb * tm
    kinds = sorted({(rope, scale) for _, rope, scale in tiles})
    cos, sin = _rope_tables(pos.reshape(-1))
    c_rows, s_rows, base = [], [], {}
    n_blocks = 0
    for rope, scale in kinds:
        base[(rope, scale)] = n_blocks
        if rope:
            c_rows += [cos * scale]
            s_rows += [sin * scale]
            n_blocks += nR * nI
        else:
            c_rows += [jnp.full((tr, HEAD_DIM), scale, F32)]
            s_rows += [jnp.zeros((tr, HEAD_DIM), F32)]
            n_blocks += 1
    c_tab, s_tab = jnp.concatenate(c_rows, axis=0), jnp.concatenate(s_rows, axis=0)
    wblk = jnp.asarray([t[0] for t in tiles], jnp.int32)
    tbase = jnp.asarray([base[(t[1], t[2])] for t in tiles], jnp.int32)
    tstep = jnp.asarray([int(t[1]) for t in tiles], jnp.int32)

    def split(m):
        return m // (nR * nI), (m // nI) % nR, m % nI

    def a_idx(m, j, wb, tb, ts):
        b, rg, i = split(m)
        return (b, rg, i, 0)

    def t_idx(m, j, wb, tb, ts):
        b, rg, i = split(m)
        return (tb[j] + ts[j] * (rg * nI + i), 0)

    def o_idx(m, j, wb, tb, ts):
        b, rg, i = split(m)
        return (b, rg, i, j)

    grid_spec = pltpu.PrefetchScalarGridSpec(
        num_scalar_prefetch=3,
        grid=(B * nR * nI, nJ),
        in_specs=[
            pl.BlockSpec((None, sb, tm, K), a_idx),
            pl.BlockSpec((K, tn), lambda m, j, wb, tb, ts: (0, wb[j])),
            pl.BlockSpec((tr, HEAD_DIM), t_idx),
            pl.BlockSpec((tr, HEAD_DIM), t_idx),
        ],
        out_specs=pl.BlockSpec((None, sb, tm, tn), o_idx),
    )
    return pl.pallas_call(
        functools.partial(_proj_kernel, n_split=n_split),
        out_shape=jax.ShapeDtypeStruct((B, d, L, nJ * tn), out_dtype),
        grid_spec=grid_spec,
        compiler_params=_params("parallel", "arbitrary"),
        name=f"proj_d{d}_n{nJ * tn}",
    )(wblk, tbase, tstep, act4, w, c_tab, s_tab)


def _outproj_kernel(*refs, n_split):
    o_ref = refs[-1]
    ws = [w_ref[...].astype(BF16) for w_ref in refs[1:-1:2]]
    chunk = o_ref.shape[0] // n_split
    for h in range(n_split):
        rows = slice(h * chunk, (h + 1) * chunk)
        acc = None
        for a_ref, w in zip(refs[0:-1:2], ws):
            part = jnp.dot(a_ref[rows, :], w, preferred_element_type=F32)
            acc = part if acc is None else acc + part
        o_ref[rows, :] = acc.astype(o_ref.dtype)


def _out_project(acts, w, *, tm, tn, n_split=4):
    T, K = acts[0].shape
    N = w.shape[1]
    assert all(a.shape == (T, K) for a in acts) and w.shape[0] == len(acts) * K
    in_specs, args = [], []
    for n, a in enumerate(acts):
        in_specs += [pl.BlockSpec((tm, K), lambda i, j: (i, 0)), pl.BlockSpec((K, tn), lambda i, j, n=n: (n, j))]
        args += [a, w]
    return pl.pallas_call(
        functools.partial(_outproj_kernel, n_split=n_split),
        out_shape=jax.ShapeDtypeStruct((T, N), BF16),
        grid=(T // tm, N // tn),
        in_specs=in_specs,
        out_specs=pl.BlockSpec((tm, tn), lambda i, j: (i, j)),
        compiler_params=_params("parallel", "arbitrary"),
        name=f"outproj_{len(acts)}",
    )(*args)


def _ln_kernel(x_ref, y_ref, g_ref, b_ref, o_ref, *stream_refs, dilations):
    z = DEEPNORM_ALPHA * x_ref[...] + y_ref[...].astype(F32)
    mu = jnp.mean(z, axis=-1, keepdims=True)
    zc = z - mu
    var = jnp.mean(zc * zc, axis=-1, keepdims=True)
    out = zc * lax.rsqrt(var + LN_EPS) * g_ref[...] + b_ref[...]
    o_ref[...] = out
    tr = out.shape[0]
    ob = out.astype(BF16)
    dst = lax.broadcasted_iota(jnp.int32, (tr, tr), 0)
    src = lax.broadcasted_iota(jnp.int32, (tr, tr), 1)
    for d, s_ref in zip(dilations, stream_refs):
        if d == 1:
            s_ref[0] = ob
            continue
        per = tr // d
        perm = jnp.where(src == (dst & (per - 1)) * d + (dst >> (per.bit_length() - 1)), 1.0, 0.0).astype(BF16)
        sm = jnp.dot(perm, ob, preferred_element_type=F32).astype(BF16)
        for r in range(d):
            s_ref[r] = sm[r * per:(r + 1) * per]


def _deepnorm_ln(x3, y2, g, b, *, tr, dilations=()):
    B, S, D = x3.shape
    nI = S // tr
    assert S % tr == 0 and all(tr % (d * 16) == 0 and d & (d - 1) == 0 for d in dilations) and tr & (tr - 1) == 0
    row3 = pl.BlockSpec((None, tr, D), lambda b, i: (b, i, 0))
    vec = pl.BlockSpec((1, D), lambda b, i: (0, 0))
    out_shape = [jax.ShapeDtypeStruct((B, S, D), F32)]
    out_specs = [row3]
    for d in dilations:
        out_shape.append(jax.ShapeDtypeStruct((B, d, S // d, D), BF16))
        out_specs.append(pl.BlockSpec((None, d, tr // d, D), lambda b, i: (b, 0, i, 0)))
    return pl.pallas_call(
        functools.partial(_ln_kernel, dilations=tuple(dilations)),
        out_shape=tuple(out_shape),
        grid=(B, nI),
        in_specs=[row3, pl.BlockSpec((tr, D), lambda b, i: (b * nI + i, 0)), vec, vec],
        out_specs=tuple(out_specs),
        compiler_params=_params("parallel", "parallel"),
        name="deepnorm_ln",
    )(x3, y2, g.reshape(1, D), b.reshape(1, D))


def _compress_kernel(zk_ref, zv_ref, pe_k_ref, pe_v_ref, w1ka_ref, w1kb_ref, w1va_ref, w1vb_ref,
                     w2k_ref, w2vt_ref, cos_ref, sin_ref, kc_ref, vct_ref, uk_ref, vk_ref, uv_ref, vv_ref):
    p = pl.program_id(1)
    n = uk_ref.shape[0]

    @pl.when(p == 0)
    def _():
        for ref in (uk_ref, vk_ref, uv_ref, vv_ref):
            ref[...] = jnp.zeros_like(ref)

    def accumulate(z_ref, pe_ref, wa_ref, wb_ref, u_ref, v_ref):
        z = z_ref[...].astype(F32)
        za = (z + pe_ref[pl.ds(p, 1), :]).astype(BF16)
        zb = (z + pe_ref[pl.ds(p + NSA_CMP_STRIDE, 1), :]).astype(BF16)
        u_ref[...] += jnp.dot(za, wa_ref[...], preferred_element_type=F32)
        v_ref[...] += jnp.dot(zb, wb_ref[...], preferred_element_type=F32)

    accumulate(zk_ref, pe_k_ref, w1ka_ref, w1kb_ref, uk_ref, vk_ref)
    accumulate(zv_ref, pe_v_ref, w1va_ref, w1vb_ref, uv_ref, vv_ref)

    @pl.when(p == NSA_CMP_STRIDE - 1)
    def _():
        hid_k = _silu(uk_ref[...] + pltpu.roll(vk_ref[...], n - 1, 0)).astype(BF16)
        hid_v = _silu(uv_ref[...] + pltpu.roll(vv_ref[...], n - 1, 0)).astype(BF16)
        kc = jnp.dot(hid_k, w2k_ref[...], preferred_element_type=F32)
        kc = _rope(kc, cos_ref[...], sin_ref[...])
        row = lax.broadcasted_iota(jnp.int32, kc.shape, 0)
        kc_ref[...] = jnp.where(row < n - 1, kc, 0.0).astype(kc_ref.dtype)
        vct = lax.dot_general(w2vt_ref[...], hid_v, NT_DIMS, preferred_element_type=F32)
        col = lax.broadcasted_iota(jnp.int32, vct.shape, 1)
        vct_ref[...] = jnp.where(col < n - 1, vct, 0.0).astype(vct_ref.dtype)


def _nsa_compress(h2, B, S, pe_k, w1_k, w2_k, pe_v, w1_v, w2_v, kc_block, vc_block):
    G = NSA_KV_GROUPS
    n = S // NSA_CMP_STRIDE
    c0, c1 = kc_block * HEAD_DIM, (vc_block + G) * HEAD_DIM
    nb = (c1 - c0) // HEAD_DIM
    h3 = h2[:, c0:c1].reshape(B, n, NSA_CMP_STRIDE * (c1 - c0))
    kc_block, vc_block = 0, vc_block - kc_block
    cos, sin = _rope_tables(jnp.arange(n) * NSA_CMP_STRIDE + NSA_CMP_LEN - 1)

    def z_spec(block0):
        return pl.BlockSpec((None, n, HEAD_DIM), lambda bg, p: (bg // G, 0, p * nb + block0 + bg % G))

    full = lambda shape: pl.BlockSpec(shape, lambda bg, p: (0,) * len(shape))
    w1a = pl.BlockSpec((HEAD_DIM, HEAD_DIM), lambda bg, p: (p, 0))
    w1b = pl.BlockSpec((HEAD_DIM, HEAD_DIM), lambda bg, p: (p + NSA_CMP_STRIDE, 0))
    acc = pltpu.VMEM((n, HEAD_DIM), F32)
    return pl.pallas_call(
        _compress_kernel,
        out_shape=(jax.ShapeDtypeStruct((B * G, n, HEAD_DIM), BF16), jax.ShapeDtypeStruct((B * G, HEAD_DIM, n), BF16)),
        grid=(B * G, NSA_CMP_STRIDE),
        in_specs=[z_spec(kc_block), z_spec(vc_block), full((NSA_CMP_LEN, HEAD_DIM)), full((NSA_CMP_LEN, HEAD_DIM)),
                  w1a, w1b, w1a, w1b, full((HEAD_DIM, HEAD_DIM)), full((HEAD_DIM, HEAD_DIM)),
                  full((n, HEAD_DIM)), full((n, HEAD_DIM))],
        out_specs=(pl.BlockSpec((None, n, HEAD_DIM), lambda bg, p: (bg, 0, 0)),
                   pl.BlockSpec((None, HEAD_DIM, n), lambda bg, p: (bg, 0, 0))),
        scratch_shapes=[acc, acc, acc, acc],
        compiler_params=_params("parallel", "arbitrary"),
        name="nsa_compress",
    )(h3, h3, pe_k, pe_v, w1_k.astype(BF16), w1_k.astype(BF16), w1_v.astype(BF16), w1_v.astype(BF16),
      w2_k.astype(BF16), w2_v.T.astype(BF16), cos, sin)


def _split3(x):
    hi = x.astype(BF16)
    r1 = x - hi.astype(F32)
    mid = r1.astype(BF16)
    lo = (r1 - mid.astype(F32)).astype(BF16)
    return hi, mid, lo


def _vt_kernel(blk_ref, v_ref, o_ref):
    extra = VT_ROWS - HEAD_DIM
    ones_row = jnp.where(lax.broadcasted_iota(jnp.int32, (extra, HEAD_DIM), 0) == 0, 1.0, 0.0).astype(o_ref.dtype)
    for c in range(o_ref.shape[0]):
        o_ref[c, 0:HEAD_DIM, :] = v_ref[c * HEAD_DIM:(c + 1) * HEAD_DIM, :].astype(F32).T.astype(o_ref.dtype)
        o_ref[c, HEAD_DIM:VT_ROWS, :] = ones_row


def _transpose_values(h2, B, S, blocks, *, rows):
    ns = S // rows
    per = rows // HEAD_DIM
    nv = len(blocks)
    grid_spec = pltpu.PrefetchScalarGridSpec(
        num_scalar_prefetch=1,
        grid=(B, nv, ns),
        in_specs=[pl.BlockSpec((rows, HEAD_DIM), lambda b, i, s, blk: (b * ns + s, blk[i]))],
        out_specs=pl.BlockSpec((None, None, per, VT_ROWS, HEAD_DIM), lambda b, i, s, blk: (b, i, s, 0, 0)),
    )
    return pl.pallas_call(
        _vt_kernel,
        out_shape=jax.ShapeDtypeStruct((B, nv, S // HEAD_DIM, VT_ROWS, HEAD_DIM), BF16),
        grid_spec=grid_spec,
        compiler_params=_params("parallel", "parallel", "arbitrary"),
        name="transpose_values",
    )(jnp.asarray(blocks, jnp.int32), h2)


def _nsa_kernel(q_ref, kc_ref, vct_ref, ks_ref, vst_ref, kw_ref, vwt_ref, gl_ref, gate_ref, ovt_ref,
                o_ref, sel_ref, sig_ref, *, tq, tk, n_sel):
    g = pl.program_id(1)
    qi = pl.program_id(2)
    t0 = qi * tq
    hg = NSA_HEADS_PER_GROUP
    n_cmp = kc_ref.shape[0]
    nbp = NSA_BLK_PAD
    head = lambda h: slice(h * HEAD_DIM, (h + 1) * HEAD_DIM)
    lanes = lambda h: slice(h * tq, (h + 1) * tq)
    per_head = lambda a: jnp.concatenate([a] * hg, axis=1)
    chunk = HEAD_DIM

    q4 = q_ref[...]
    qs = jnp.concatenate([q4[:, head(h)] for h in range(hg)], axis=0)

    def softmax_rows(s, m_floor):
        m = jnp.maximum(jnp.max(s, axis=0, keepdims=True), m_floor)
        e = jnp.exp2(s - m)
        den = jnp.sum(e, axis=0, keepdims=True)
        return e, jnp.where(den > 0.0, 1.0 / den, 0.0)

    t_lane = t0 + lax.broadcasted_iota(jnp.int32, (n_cmp, tq), 1)
    cmp_end = lax.broadcasted_iota(jnp.int32, (n_cmp, tq), 0) * NSA_CMP_STRIDE + (NSA_CMP_LEN - 1)
    cbias = per_head(jnp.where(cmp_end <= t_lane, 0.0, NEG_INF))
    s = lax.dot_general(kc_ref[...], qs, NT_DIMS, preferred_element_type=F32) + cbias
    e, inv = softmax_rows(s, M_INIT)
    p = e * inv
    psum = sum(p[:, lanes(h)] for h in range(hg))
    o_ct = jnp.dot(vct_ref[...], p.astype(BF16), preferred_element_type=F32)

    ovt = ovt_ref[...]
    imp = sum(jnp.dot(ovt, part, preferred_element_type=F32) for part in _split3(psum))
    j_idx = lax.broadcasted_iota(jnp.int32, (nbp, tq), 0)
    cur = (t0 + lax.broadcasted_iota(jnp.int32, (nbp, tq), 1)) >> 6
    forced = (j_idx == 0) | (j_idx == cur) | (j_idx == cur - 1)
    imp = jnp.where(forced, imp + NSA_FORCE_BONUS, imp)
    imp = jnp.where(j_idx <= cur, imp, NEG_INF)

    j_f = j_idx.astype(F32)
    picked = jnp.zeros((nbp, tq), F32)
    for _ in range(n_sel):
        top = jnp.max(imp, axis=0, keepdims=True)
        first = jnp.min(jnp.where(imp == top, j_f, float(nbp)), axis=0, keepdims=True)
        hit = j_f == first
        picked = jnp.where(hit, 1.0, picked)
        imp = jnp.where(hit, TAKEN, imp)
    sel_ref[...] = jnp.where(j_idx <= cur, picked, 0.0)

    n_pairs = hg // 2
    key_i = lax.broadcasted_iota(jnp.int32, (SEL_SUB, tq), 0)
    t_i = t0 + lax.broadcasted_iota(jnp.int32, (SEL_SUB, tq), 1)
    blocks_per_tile = tk // NSA_SEL_LEN
    blocks_per_sub = SEL_SUB // NSA_SEL_LEN
    q_pair = [qs[pr * 2 * tq:(pr + 1) * 2 * tq] for pr in range(n_pairs)]

    def sel_scores(kt):
        k0 = pl.multiple_of(kt * tk, tk)
        sel8 = sel_ref[pl.ds(pl.multiple_of(kt * blocks_per_tile, blocks_per_tile), blocks_per_tile), :]
        scores = []
        for sub in range(tk // SEL_SUB):
            ks0 = k0 + sub * SEL_SUB
            k = ks_ref[pl.ds(pl.multiple_of(ks0, SEL_SUB), SEL_SUB), :]
            chosen = jnp.concatenate([jnp.broadcast_to(sel8[sub * blocks_per_sub + b:sub * blocks_per_sub + b + 1, :],
                                                       (NSA_SEL_LEN, tq)) for b in range(blocks_per_sub)], axis=0)
            bias = jnp.where((chosen > 0.5) & (key_i + ks0 <= t_i), 0.0, NEG_INF)
            bias2 = jnp.concatenate([bias, bias], axis=1)
            for pr in range(n_pairs):
                scores.append(lax.dot_general(k, q_pair[pr], NT_DIMS, preferred_element_type=F32) + bias2)
        return tuple(scores)

    def sel_tiles(kt0, count, carry):
        m, acc = list(carry[0]), list(carry[1])
        scores = [sel_scores(kt0 + u) for u in range(count)]
        for u in range(count):
            for sub in range(tk // SEL_SUB):
                c0 = (kt0 + u) * (tk // chunk) + sub * (SEL_SUB // chunk)
                vt = jnp.concatenate([vst_ref[c0 + c] for c in range(SEL_SUB // chunk)], axis=1)
                for pr in range(n_pairs):
                    s = scores[u][sub * n_pairs + pr]
                    m_new = jnp.maximum(m[pr], jnp.max(s, axis=0, keepdims=True))
                    p = jnp.exp2(s - m_new).astype(BF16)
                    acc[pr] = jnp.exp2(m[pr] - m_new) * acc[pr] + jnp.dot(vt, p, preferred_element_type=F32)
                    m[pr] = m_new
        return tuple(m), tuple(acc)

    n_tiles = (t0 + tq + tk - 1) // tk
    n_groups = n_tiles // SEL_UNROLL
    carry = (tuple(jnp.full((1, 2 * tq), M_INIT, F32) for _ in range(n_pairs)),
             tuple(jnp.zeros((VT_ROWS, 2 * tq), F32) for _ in range(n_pairs)))
    carry = lax.fori_loop(0, n_groups, lambda it, c: sel_tiles(it * SEL_UNROLL, SEL_UNROLL, c), carry)
    _, acc_pairs = lax.fori_loop(n_groups * SEL_UNROLL, n_tiles, lambda kt, c: sel_tiles(kt, 1, c), carry)
    acc = jnp.concatenate(acc_pairs, axis=1)
    l_s = acc[HEAD_DIM:HEAD_DIM + 1, :]
    o_st = acc[:HEAD_DIM, :] * jnp.where(l_s > 0.0, 1.0 / l_s, 0.0)

    band = NSA_WINDOW + tq
    wc = jnp.maximum(qi - NSA_WINDOW // tq, 0)
    w0 = pl.multiple_of(wc * tq, tq)
    kb = kw_ref[pl.ds(w0, band), :]
    vbt = jnp.concatenate([vwt_ref[wc + c, 0:HEAD_DIM, :] for c in range(band // chunk)], axis=1)
    rel = (t0 + lax.broadcasted_iota(jnp.int32, (band, tq), 1)) - (w0 + lax.broadcasted_iota(jnp.int32, (band, tq), 0))
    wbias = per_head(jnp.where((rel >= 0) & (rel < NSA_WINDOW), 0.0, NEG_INF))
    s = lax.dot_general(kb, qs, NT_DIMS, preferred_element_type=F32) + wbias
    e, inv = softmax_rows(s, M_INIT)
    o_wt = jnp.dot(vbt, e.astype(BF16), preferred_element_type=F32) * inv

    sig_ref[...] = (1.0 / (1.0 + jnp.exp(-gl_ref[...]))).T
    gate = gate_ref[...].astype(F32)
    for h in range(hg):
        base = (g * hg + h) * 3
        gc, gs, gw = (sig_ref[pl.ds(base + c, 1), :] for c in range(3))
        o_t = gc * o_ct[:, lanes(h)] + gs * o_st[:, lanes(h)] + gw * o_wt[:, lanes(h)]
        o_ref[:, head(h)] = (o_t.T * _silu(gate[:, head(h)])).astype(o_ref.dtype)


def _nsa_attention(h2, small, kc, vct, vt, B, S, *, q_tile0, kv_block0, gate_tile0, tq, tk):
    T = h2.shape[0]
    G, hg = NSA_KV_GROUPS, NSA_HEADS_PER_GROUP
    n_cmp = S // NSA_CMP_STRIDE
    n_blk = S // NSA_SEL_LEN
    nQ = S // tq
    gw = hg * HEAD_DIM
    assert n_blk <= NSA_BLK_PAD and S % tk == 0 and S >= NSA_WINDOW + tq and tq == HEAD_DIM
    assert tk % SEL_SUB == 0 and (tk // NSA_SEL_LEN) % 8 == 0 and hg % 2 == 0

    nn = np.arange(n_cmp)[None, :]
    jj = np.arange(NSA_BLK_PAD)[:, None]
    ovt = ((nn * NSA_CMP_STRIDE + NSA_CMP_LEN - 1 >= jj * NSA_SEL_LEN) & (nn * NSA_CMP_STRIDE < (jj + 1) * NSA_SEL_LEN)
           & (nn < n_cmp - 1) & (jj < n_blk))

    def k_spec(c):
        return pl.BlockSpec((S, HEAD_DIM), lambda b, g, i: (b, kv_block0 + 4 * c + g))

    def vt_spec(which):
        return pl.BlockSpec((None, None, S // HEAD_DIM, VT_ROWS, HEAD_DIM), lambda b, g, i: (b, which * G + g, 0, 0, 0))

    rows = lambda width, col: pl.BlockSpec((tq, width), lambda b, g, i: (b * nQ + i, col(g)))
    per_bg = lambda shape: pl.BlockSpec((None,) + shape, lambda b, g, i: (b * G + g, 0, 0))
    const = lambda shape: pl.BlockSpec(shape, lambda b, g, i: (0,) * len(shape))
    return pl.pallas_call(
        functools.partial(_nsa_kernel, tq=tq, tk=tk, n_sel=min(NSA_N_SEL, n_blk)),
        out_shape=jax.ShapeDtypeStruct((T, G * gw), BF16),
        grid=(B, G, nQ),
        in_specs=[rows(gw, lambda g: q_tile0 + g), per_bg((n_cmp, HEAD_DIM)), per_bg((HEAD_DIM, n_cmp)),
                  k_spec(2), vt_spec(0), k_spec(4), vt_spec(1),
                  rows(HEAD_DIM, lambda g: 0), rows(gw, lambda g: gate_tile0 + g),
                  const((NSA_BLK_PAD, n_cmp))],
        out_specs=rows(gw, lambda g: g),
        scratch_shapes=[pltpu.VMEM((NSA_BLK_PAD, tq), F32), pltpu.VMEM((HEAD_DIM, tq), F32)],
        compiler_params=_params("parallel", "parallel", "arbitrary"),
        name="nsa_attention",
    )(h2, kc, vct, h2, vt, h2, vt, small, h2, jnp.asarray(ovt, BF16))


def _gla_kernel(q_ref, k_ref, v_ref, a_ref, gate_ref, wa_ref, ba_ref, ng_ref, o_ref, state_ref, *, n_chunks, n_heads):
    C = GLA_CHUNK

    @pl.when(pl.program_id(2) == 0)
    def _():
        state_ref[...] = jnp.zeros_like(state_ref)

    r_i = lax.broadcasted_iota(jnp.int32, (C, C), 0)
    c_i = lax.broadcasted_iota(jnp.int32, (C, C), 1)
    causal = r_i >= c_i
    tri = jnp.where(causal, 1.0, 0.0).astype(BF16)
    ng = ng_ref[...]
    for c in range(n_chunks):
        rows = slice(c * C, (c + 1) * C)
        a = a_ref[rows, :]
        for h in range(n_heads):
            kcols = slice(h * GLA_DK, (h + 1) * GLA_DK)
            vcols = slice(h * GLA_DV, (h + 1) * GLA_DV)
            q = q_ref[rows, kcols].astype(F32)
            k = k_ref[rows, kcols].astype(F32)
            v = v_ref[rows, vcols]
            pre = jnp.dot(a, wa_ref[h], preferred_element_type=F32) + ba_ref[h]
            log_a = (jnp.minimum(pre, 0.0) - jnp.log(1.0 + jnp.exp(-jnp.abs(pre)))) * (1.0 / GLA_TAU)
            bcum = sum(jnp.dot(tri, part, preferred_element_type=F32) for part in _split3(log_a))
            b_last = bcum[C - 1:C, :]
            qe = (q * jnp.exp(bcum)).astype(BF16)
            ke = (k * jnp.exp(-bcum)).astype(BF16)
            kd = (k * jnp.exp(b_last - bcum)).astype(BF16)
            attn = lax.dot_general(qe, ke, NT_DIMS, preferred_element_type=F32)
            attn = jnp.where(causal, attn, 0.0).astype(BF16)
            state_t = state_ref[h]
            o = jnp.dot(attn, v, preferred_element_type=F32)
            o = o + lax.dot_general(qe, state_t.astype(BF16), NT_DIMS, preferred_element_type=F32)
            state_ref[h] = state_t * jnp.exp(b_last) + lax.dot_general(v, kd, TN_DIMS, preferred_element_type=F32)
            o = o * lax.rsqrt(jnp.mean(o * o, axis=-1, keepdims=True) + NORM_EPS) * ng
            o_ref[rows, vcols] = (o * _silu(gate_ref[rows, vcols].astype(F32))).astype(o_ref.dtype)


def _gla(h2, small, w_a2, b_a, norm_g, B, S, *, q_col0, k_col0, v_col0, gate_col0, rows, heads_per_step):
    T = h2.shape[0]
    H, hp = GLA_HEADS, heads_per_step
    nR = S // rows
    a_lo = 3 * NSA_KV_GROUPS * NSA_HEADS_PER_GROUP
    assert H % hp == 0 and all(c % (hp * w) == 0 for c, w in ((q_col0, GLA_DK), (k_col0, GLA_DK), (v_col0, GLA_DV),
                                                              (gate_col0, GLA_DV)))
    wa = jnp.zeros((H, HEAD_DIM, GLA_DK), F32).at[:, a_lo:a_lo + GLA_RANK, :].set(
        w_a2.reshape(GLA_RANK, H, GLA_DK).transpose(1, 0, 2))
    blk = lambda width, col0: pl.BlockSpec((rows, hp * width), lambda b, g, i: (b * nR + i, col0 // (hp * width) + g))
    per_head = lambda shape: pl.BlockSpec((hp,) + shape, lambda b, g, i: (g, 0, 0))
    return pl.pallas_call(
        functools.partial(_gla_kernel, n_chunks=rows // GLA_CHUNK, n_heads=hp),
        out_shape=jax.ShapeDtypeStruct((T, H * GLA_DV), BF16),
        grid=(B, H // hp, nR),
        in_specs=[blk(GLA_DK, q_col0), blk(GLA_DK, k_col0), blk(GLA_DV, v_col0),
                  pl.BlockSpec((rows, HEAD_DIM), lambda b, g, i: (b * nR + i, 0)), blk(GLA_DV, gate_col0),
                  per_head((HEAD_DIM, GLA_DK)), per_head((1, GLA_DK)),
                  pl.BlockSpec((1, GLA_DV), lambda b, g, i: (0, 0))],
        out_specs=blk(GLA_DV, 0),
        scratch_shapes=[pltpu.VMEM((hp, GLA_DV, GLA_DK), F32)],
        compiler_params=_params("parallel", "parallel", "arbitrary"),
        name="gla",
    )(h2, h2, h2, small, h2, wa, b_a.reshape(H, 1, GLA_DK), norm_g.reshape(1, GLA_DV))


def _dilated_stream(i, q, kp, kc, vp, vc, prev, gate, o_scr, m_scr, d_scr):
    tq = o_scr.shape[0]
    head = lambda h: slice(h * HEAD_DIM, (h + 1) * HEAD_DIM)
    row = lax.broadcasted_iota(jnp.int32, (tq, 2 * tq), 0)
    col = lax.broadcasted_iota(jnp.int32, (tq, 2 * tq), 1)
    valid = (col >= row + (tq - DIL_BACK)) & (col <= row + tq) & ((col >= tq) | (i > 0))
    m_scr[...] = jnp.zeros(m_scr.shape, F32)
    d_scr[...] = jnp.ones(d_scr.shape, F32)
    for h in range(DIL_HEADS):
        k = jnp.concatenate([kp[:, head(h)], kc[:, head(h)]], axis=0)
        v = jnp.concatenate([vp[:, head(h)], vc[:, head(h)]], axis=0)
        s = lax.dot_general(q[:, head(h)], k, NT_DIMS, preferred_element_type=F32)
        s = jnp.where(valid, s, NEG_INF)
        m = jnp.max(s, axis=1, keepdims=True)
        e = jnp.exp(s - m)
        o_scr[:, head(h)] = jnp.dot(e.astype(BF16), v, preferred_element_type=F32)
        m_scr[:, h:h + 1] = m
        d_scr[:, h:h + 1] = jnp.sum(e, axis=1, keepdims=True)
    den = d_scr[...]
    lse = m_scr[...] + jnp.log(den)
    w_cur = 1.0 / den
    if prev is not None:
        prev_o, lse_p = prev
        lse_n = jnp.maximum(lse_p, lse) + jnp.log(1.0 + jnp.exp(-jnp.abs(lse_p - lse)))
        w_prev = jnp.exp(lse_p - lse_n)
        w_cur = jnp.exp(lse - lse_n) * w_cur
        lse = lse_n
    outs = []
    for h in range(DIL_HEADS):
        o = o_scr[:, head(h)] * w_cur[:, h:h + 1]
        if prev is not None:
            o = o + prev_o[:, head(h)].astype(F32) * w_prev[:, h:h + 1]
        if gate is not None:
            o = o * _silu(gate[:, head(h)].astype(F32))
        outs.append(o)
    return outs, lse


def _dilated_kernel(*refs, has_prev, is_last):
    q_ref, kp_ref, kc_ref, vp_ref, vc_ref = refs[:5]
    rest = list(refs[5:])
    prev = (rest.pop(0), rest.pop(0)) if has_prev else None
    gate_ref = rest.pop(0) if is_last else None
    o_ref = rest.pop(0)
    lse_ref = None if is_last else rest.pop(0)
    o_scr, m_scr, d_scr = rest
    if prev is not None:
        prev = (prev[0], prev[1][...])
    outs, lse = _dilated_stream(pl.program_id(2), q_ref, kp_ref, kc_ref, vp_ref, vc_ref, prev, gate_ref,
                                o_scr, m_scr, d_scr)
    for h, o in enumerate(outs):
        o_ref[:, h * HEAD_DIM:(h + 1) * HEAD_DIM] = o.astype(o_ref.dtype)
    if lse_ref is not None:
        lse_ref[...] = lse


def _dilated_scratch(tq, W):
    return [pltpu.VMEM((tq, W), F32), pltpu.VMEM((tq, HEAD_DIM), F32), pltpu.VMEM((tq, HEAD_DIM), F32)]


def _dilated_group(hg, prev_o, prev_lse, gate, *, tq):
    B, d, L, W3 = hg.shape
    W = W3 // 3
    has_prev, is_last = prev_o is not None, gate is not None
    assert tq == DIL_BACK and L % tq == 0
    cur = lambda col: pl.BlockSpec((None, None, tq, W), lambda b, r, i: (b, r, i, col))
    prv = lambda col: pl.BlockSpec((None, None, tq, W), lambda b, r, i: (b, r, jnp.maximum(i - 1, 0), col))
    tok = lambda width: pl.BlockSpec((None, tq, width), lambda b, r, i: (b, i, r))
    in_specs = [cur(0), prv(1), cur(1), prv(2), cur(2)]
    args = [hg] * 5
    if has_prev:
        in_specs += [tok(W), tok(HEAD_DIM)]
        args += [prev_o, prev_lse]
    if is_last:
        in_specs += [tok(W)]
        args += [gate]
    out_shape = [jax.ShapeDtypeStruct((B, L, d * W), BF16)]
    out_specs = [tok(W)]
    if not is_last:
        out_shape += [jax.ShapeDtypeStruct((B, L, d * HEAD_DIM), F32)]
        out_specs += [tok(HEAD_DIM)]
    res = pl.pallas_call(
        functools.partial(_dilated_kernel, has_prev=has_prev, is_last=is_last),
        out_shape=tuple(out_shape),
        grid=(B, d, L // tq),
        in_specs=in_specs,
        out_specs=tuple(out_specs),
        scratch_shapes=_dilated_scratch(tq, W),
        compiler_params=_params("parallel", "parallel", "arbitrary"),
        name=f"dilated_d{d}",
    )(*args)
    return res[0] if is_last else (res[0], res[1])


def _regroup_kernel(*refs, d, ratio):
    q_ref, kp_ref, kc_ref, vp_ref, vc_ref = refs[:5]
    po_refs = refs[5:5 + ratio]
    pl_refs = refs[5 + ratio:5 + 2 * ratio]
    o_ref, lse_ref, o_scr, m_scr, d_scr, so_scr, sl_scr = refs[5 + 2 * ratio:]
    i = pl.program_id(1)
    r = pl.program_id(2)
    tq, W = o_scr.shape
    rows_c = tq // ratio
    n = d * tq
    shift = lambda v: v.bit_length() - 1
    dst = lax.broadcasted_iota(jnp.int32, (tq, tq), 0)
    src = lax.broadcasted_iota(jnp.int32, (tq, tq), 1)
    gather = jnp.where(src == (dst & (ratio - 1)) * rows_c + (dst >> shift(ratio)), 1.0, 0.0).astype(BF16)
    prev_o = jnp.dot(gather, jnp.concatenate([p[...] for p in po_refs], axis=0), preferred_element_type=F32)
    prev_l = sum(jnp.dot(gather, part, preferred_element_type=F32)
                 for part in _split3(jnp.concatenate([p[...] for p in pl_refs], axis=0)))
    outs, lse = _dilated_stream(i, q_ref, kp_ref, kc_ref, vp_ref, vc_ref, (prev_o, prev_l), None,
                                o_scr, m_scr, d_scr)
    so_scr[r] = jnp.concatenate(outs, axis=1).astype(BF16)
    sl_scr[r] = lse

    @pl.when(r == d - 1)
    def _():
        dst = lax.broadcasted_iota(jnp.int32, (n, n), 0)
        src = lax.broadcasted_iota(jnp.int32, (n, n), 1)
        scatter = jnp.where(src == (dst & (d - 1)) * tq + (dst >> shift(d)), 1.0, 0.0).astype(BF16)
        o_ref[...] = jnp.dot(scatter, so_scr[...].reshape(n, W), preferred_element_type=F32).astype(o_ref.dtype)
        lse_ref[...] = sum(jnp.dot(scatter, part, preferred_element_type=F32)
                           for part in _split3(sl_scr[...].reshape(n, HEAD_DIM)))


def _dilated_regroup(hg, prev_o, prev_lse, *, fine, tq):
    B, d, L, W3 = hg.shape
    W = W3 // 3
    S = L * d
    ratio = fine // d
    assert tq == DIL_BACK and L % tq == 0 and fine % d == 0 and tq % ratio == 0
    assert d & (d - 1) == 0 and ratio & (ratio - 1) == 0
    rows_c = tq // ratio
    cur = lambda col: pl.BlockSpec((None, None, tq, W), lambda b, i, r: (b, r, i, col))
    prv = lambda col: pl.BlockSpec((None, None, tq, W), lambda b, i, r: (b, r, jnp.maximum(i - 1, 0), col))
    piece = lambda width: [pl.BlockSpec((None, rows_c, width), lambda b, i, r, j=j: (b, i, j * d + r))
                           for j in range(ratio)]
    return pl.pallas_call(
        functools.partial(_regroup_kernel, d=d, ratio=ratio),
        out_shape=(jax.ShapeDtypeStruct((B, S, W), BF16), jax.ShapeDtypeStruct((B, S, HEAD_DIM), F32)),
        grid=(B, L // tq, d),
        in_specs=[cur(0), prv(1), cur(1), prv(2), cur(2)] + piece(W) + piece(HEAD_DIM),
        out_specs=(pl.BlockSpec((None, d * tq, W), lambda b, i, r: (b, i, 0)),
                   pl.BlockSpec((None, d * tq, HEAD_DIM), lambda b, i, r: (b, i, 0))),
        scratch_shapes=_dilated_scratch(tq, W) + [pltpu.VMEM((d, tq, W), BF16), pltpu.VMEM((d, tq, HEAD_DIM), F32)],
        compiler_params=_params("parallel", "arbitrary", "arbitrary"),
        name=f"dilated_regroup_d{d}",
    )(*([hg] * 5 + [prev_o] * ratio + [prev_lse] * ratio))


def _row_tile(n, target):
    t = min(n, target)
    assert n % t == 0
    return t


def _even_layer(xb4, w_in, pe_k, w1_k, w2_k, pe_v, w1_v, w2_v, gla_w_a2, gla_b_a, gla_norm_g, w_out):
    B, _, S, D = xb4.shape
    T = B * S
    nq = NSA_KV_GROUPS * NSA_HEADS_PER_GROUP * HEAD_DIM
    nkv = 6 * NSA_KV_GROUPS * HEAD_DIM
    ng = 3 * NSA_KV_GROUPS * NSA_HEADS_PER_GROUP
    gk = GLA_HEADS * GLA_DK
    gv = GLA_HEADS * GLA_DV
    o_g = nq + nkv
    o_a = o_g + ng + nq + 2 * gk + gv
    w_main = jnp.concatenate([w_in[:, :o_g], w_in[:, o_g + ng:o_a], w_in[:, o_a + GLA_RANK:]], axis=1).astype(BF16)
    w_small = jnp.concatenate([w_in[:, o_g:o_g + ng], w_in[:, o_a:o_a + GLA_RANK],
                               jnp.zeros((D, HEAD_DIM - ng - GLA_RANK), w_in.dtype)], axis=1).astype(BF16)
    tn = 512
    plain, rope = (False, 1.0), (True, 1.0)
    tile_kinds = ([(True, Q_SCALE * LOG2E)] * (nq // tn)
                  + [plain, plain, rope, plain, rope, plain]
                  + [plain] * (nq // tn) + [(False, GLA_Q_SCALE)] * (gk // tn) + [plain] * (gk // tn)
                  + [plain] * (gv // tn) + [plain] * (gv // tn))
    pos = jnp.arange(S)[None]
    tm = _row_tile(S, 1024)
    h2 = _project(xb4, w_main, [(j,) + k for j, k in enumerate(tile_kinds)], pos, tm=tm, tn=tn,
                  out_dtype=BF16).reshape(T, -1)
    small = _project(xb4, w_small, [(0,) + plain], pos, tm=tm, tn=HEAD_DIM, out_dtype=F32).reshape(T, HEAD_DIM)

    q_tile0 = 0
    kv_block0 = nq // HEAD_DIM
    gate_tile0 = (nq + nkv) // tn
    kc, vct = _nsa_compress(h2, B, S, pe_k, w1_k, w2_k, pe_v, w1_v, w2_v, kv_block0, kv_block0 + NSA_KV_GROUPS)
    v_blocks = [kv_block0 + 4 * c + g for c in (3, 5) for g in range(NSA_KV_GROUPS)]
    vt = _transpose_values(h2, B, S, v_blocks, rows=_row_tile(S, 1024))
    o_nsa = _nsa_attention(h2, small, kc, vct, vt, B, S, q_tile0=q_tile0, kv_block0=kv_block0,
                           gate_tile0=gate_tile0, tq=128, tk=_row_tile(S, 512))
    c0 = nq + nkv + nq
    o_gla = _gla(h2, small, gla_w_a2, gla_b_a, gla_norm_g, B, S, q_col0=c0, k_col0=c0 + gk, v_col0=c0 + 2 * gk,
                 gate_col0=c0 + 2 * gk + gv, rows=_row_tile(S, 512), heads_per_step=2)
    return _out_project([o_nsa, o_gla], w_out, tm=_row_tile(T, 1024), tn=512)


def _odd_layer(streams, w_in, w_out):
    B, _, S, D = streams[1].shape
    T = B * S
    n_g = len(DIL_GROUPS)
    W = DIL_HEADS * HEAD_DIM
    tn = 512
    blocks = lambda col0: [col0 // tn + t for t in range(W // tn)]
    gate = _project(streams[1], w_in, [(blk, False, 1.0) for blk in blocks(3 * n_g * W)], jnp.arange(S)[None],
                    tm=_row_tile(S, 1024), tn=tn, out_dtype=BF16).reshape(B, S, W)

    def group_qkv(gi, d):
        L = S // d
        tiles = ([(blk, True, Q_SCALE) for blk in blocks(gi * W)] + [(blk, True, 1.0) for blk in blocks((n_g + gi) * W)]
                 + [(blk, False, 1.0) for blk in blocks((2 * n_g + gi) * W)])
        tm = _row_tile(L, 1024)
        return _project(streams[d], w_in, tiles, jnp.arange(S).reshape(L, d).T, tm=tm, tn=tn, out_dtype=BF16,
                        streams_per_step=min(d, 1024 // tm) if tm == L else 1, n_split=max(1, tm // 256))

    order = sorted(range(n_g), key=lambda gi: -DIL_GROUPS[gi][1])
    assert len(order) == 3 and DIL_GROUPS[order[-1]][1] == 1
    for gi in order:
        assert DIL_GROUPS[gi][0] // DIL_GROUPS[gi][1] == DIL_BACK
    d_hi, d_mid = DIL_GROUPS[order[0]][1], DIL_GROUPS[order[1]][1]
    o, lse = _dilated_group(group_qkv(order[0], d_hi), None, None, None, tq=DIL_BACK)
    o, lse = _dilated_regroup(group_qkv(order[1], d_mid), o, lse, fine=d_hi, tq=DIL_BACK)
    o = _dilated_group(group_qkv(order[2], 1), o, lse, gate, tq=DIL_BACK)
    return _out_project([o.reshape(T, W)], w_out, tm=_row_tile(T, 1024), tn=512)


def kernel(x, l0_w_in, l0_nsa_pe_k, l0_nsa_w1_k, l0_nsa_w2_k, l0_nsa_pe_v, l0_nsa_w1_v, l0_nsa_w2_v, l0_gla_w_a2, l0_gla_b_a, l0_gla_norm_g, l0_w_out, l0_ln_g, l0_ln_b, l1_w_in, l1_w_out, l1_ln_g, l1_ln_b):
    B, S, D = x.shape
    tr = _row_tile(S, 256)
    dils = sorted({d for _, d in DIL_GROUPS} | {1})
    y0 = _even_layer(x.astype(BF16)[:, None], l0_w_in, l0_nsa_pe_k, l0_nsa_w1_k, l0_nsa_w2_k, l0_nsa_pe_v, l0_nsa_w1_v,
                     l0_nsa_w2_v, l0_gla_w_a2, l0_gla_b_a, l0_gla_norm_g, l0_w_out)
    x1, *x1_streams = _deepnorm_ln(x, y0, l0_ln_g, l0_ln_b, tr=tr, dilations=dils)
    y1 = _odd_layer(dict(zip(dils, x1_streams)), l1_w_in, l1_w_out)
    return _deepnorm_ln(x1, y1, l1_ln_g, l1_ln_b, tr=tr)[0]
```

```python
import functools

import numpy as np
import jax
import jax.numpy as jnp
from jax import lax
from jax.experimental import pallas as pl
from jax.experimental.pallas import tpu as pltpu

F32 = jnp.float32
BF16 = jnp.bfloat16

HEAD_DIM = 128
ROPE_THETA = 10000.0
LN_EPS = 1e-5
NORM_EPS = 1e-6
NEG_INF = -1e30
M_INIT = -1e29
TAKEN = -3e38
SEL_UNROLL = 4
SEL_SUB = 256
VT_ROWS = HEAD_DIM + 16

NSA_KV_GROUPS = 4
NSA_HEADS_PER_GROUP = 4
NSA_CMP_LEN = 32
NSA_CMP_STRIDE = 16
NSA_SEL_LEN = 64
NSA_N_SEL = 16
NSA_WINDOW = 512
NSA_FORCE_BONUS = 1e4
NSA_BLK_PAD = 128

GLA_HEADS = 4
GLA_DK = 256
GLA_DV = 512
GLA_RANK = 16
GLA_TAU = 16.0
GLA_CHUNK = 64

DIL_GROUPS = ((128, 1), (512, 4), (2048, 16))
DIL_HEADS = 16
DIL_BACK = 128

DEPTH = 2
DEEPNORM_ALPHA = (2.0 * DEPTH) ** 0.25

VMEM_LIMIT_BYTES = 56 * 1024 * 1024

Q_SCALE = HEAD_DIM ** -0.5
LOG2E = 1.4426950408889634
GLA_Q_SCALE = GLA_DK ** -0.5

NT_DIMS = (((1,), (1,)), ((), ()))
TN_DIMS = (((0,), (0,)), ((), ()))


def _params(*sem):
    return pltpu.CompilerParams(dimension_semantics=sem, vmem_limit_bytes=VMEM_LIMIT_BYTES)


def _silu(x):
    return x * (1.0 / (1.0 + jnp.exp(-x)))


def _rope_tables(pos):
    half = HEAD_DIM // 2
    inv_freq = ROPE_THETA ** (-jnp.arange(half, dtype=F32) / half)
    ang = pos.astype(F32)[:, None] * inv_freq[None, :]
    cos = jnp.cos(ang)
    sin = jnp.sin(ang)
    return jnp.concatenate([cos, cos], axis=-1), jnp.concatenate([-sin, sin], axis=-1)


def _rope(a, cos, sin):
    return a * cos + pltpu.roll(a, HEAD_DIM // 2, 1) * sin


def _proj_kernel(wblk_ref, tbase_ref, tstep_ref, a_ref, w_ref, c_ref, s_ref, o_ref, *, n_split):
    w = w_ref[...].astype(BF16)
    sb, tm, _ = a_ref.shape
    chunk = tm // n_split
    for st in range(sb):
        for h in range(n_split):
            rows = slice(h * chunk, (h + 1) * chunk)
            trows = slice(st * tm + h * chunk, st * tm + (h + 1) * chunk)
            acc = jnp.dot(a_ref[st, rows, :], w, preferred_element_type=F32)
            c = c_ref[trows, :]
            s = s_ref[trows, :]
            for k in range(acc.shape[1] // HEAD_DIM):
                sl = slice(k * HEAD_DIM, (k + 1) * HEAD_DIM)
                o_ref[st, rows, sl] = (acc[:, sl] * c + pltpu.roll(acc[:, sl], HEAD_DIM // 2, 1) * s).astype(o_ref.dtype)


def _project(act4, w, tiles, pos, *, tm, tn, out_dtype, streams_per_step=1, n_split=4):
    B, d, L, K = act4.shape
    sb = streams_per_step
    nI, nJ, nR = L // tm, len(tiles), d // sb
    assert L % tm == 0 and pos.shape == (d, L) and d % sb == 0 and (sb == 1 or nI == 1) and tm % n_split == 0
    tr = sb * tm
    kinds = sorted({(rope, scale) for _, rope, scale in tiles})
    cos, sin = _rope_tables(pos.reshape(-1))
    c_rows, s_rows, base = [], [], {}
    n_blocks = 0
    for rope, scale in kinds:
        base[(rope, scale)] = n_blocks
        if rope:
            c_rows += [cos * scale]
            s_rows += [sin * scale]
            n_blocks += nR * nI
        else:
            c_rows += [jnp.full((tr, HEAD_DIM), scale, F32)]
            s_rows += [jnp.zeros((tr, HEAD_DIM), F32)]
            n_blocks += 1
    c_tab, s_tab = jnp.concatenate(c_rows, axis=0), jnp.concatenate(s_rows, axis=0)
    wblk = jnp.asarray([t[0] for t in tiles], jnp.int32)
    tbase = jnp.asarray([base[(t[1], t[2])] for t in tiles], jnp.int32)
    tstep = jnp.asarray([int(t[1]) for t in tiles], jnp.int32)

    def split(m):
        return m // (nR * nI), (m // nI) % nR, m % nI

    def a_idx(m, j, wb, tb, ts):
        b, rg, i = split(m)
        return (b, rg, i, 0)

    def t_idx(m, j, wb, tb, ts):
        b, rg, i = split(m)
        return (tb[j] + ts[j] * (rg * nI + i), 0)

    def o_idx(m, j, wb, tb, ts):
        b, rg, i = split(m)
        return (b, rg, i, j)

    grid_spec = pltpu.PrefetchScalarGridSpec(
        num_scalar_prefetch=3,
        grid=(B * nR * nI, nJ),
        in_specs=[
            pl.BlockSpec((None, sb, tm, K), a_idx),
            pl.BlockSpec((K, tn), lambda m, j, wb, tb, ts: (0, wb[j])),
            pl.BlockSpec((tr, HEAD_DIM), t_idx),
            pl.BlockSpec((tr, HEAD_DIM), t_idx),
        ],
        out_specs=pl.BlockSpec((None, sb, tm, tn), o_idx),
    )
    return pl.pallas_call(
        functools.partial(_proj_kernel, n_split=n_split),
        out_shape=jax.ShapeDtypeStruct((B, d, L, nJ * tn), out_dtype),
        grid_spec=grid_spec,
        compiler_params=_params("parallel", "arbitrary"),
        name=f"proj_d{d}_n{nJ * tn}",
    )(wblk, tbase, tstep, act4, w, c_tab, s_tab)


def _outproj_kernel(*refs, n_split):
    o_ref = refs[-1]
    ws = [w_ref[...].astype(BF16) for w_ref in refs[1:-1:2]]
    chunk = o_ref.shape[0] // n_split
    for h in range(n_split):
        rows = slice(h * chunk, (h + 1) * chunk)
        acc = None
        for a_ref, w in zip(refs[0:-1:2], ws):
            part = jnp.dot(a_ref[rows, :], w, preferred_element_type=F32)
            acc = part if acc is None else acc + part
        o_ref[rows, :] = acc.astype(o_ref.dtype)


def _out_project(acts, w, *, tm, tn, n_split=4):
    T, K = acts[0].shape
    N = w.shape[1]
    assert all(a.shape == (T, K) for a in acts) and w.shape[0] == len(acts) * K
    in_specs, args = [], []
    for n, a in enumerate(acts):
        in_specs += [pl.BlockSpec((tm, K), lambda i, j: (i, 0)), pl.BlockSpec((K, tn), lambda i, j, n=n: (n, j))]
        args += [a, w]
    return pl.pallas_call(
        functools.partial(_outproj_kernel, n_split=n_split),
        out_shape=jax.ShapeDtypeStruct((T, N), BF16),
        grid=(T // tm, N // tn),
        in_specs=in_specs,
        out_specs=pl.BlockSpec((tm, tn), lambda i, j: (i, j)),
        compiler_params=_params("parallel", "arbitrary"),
        name=f"outproj_{len(acts)}",
    )(*args)


def _ln_kernel(x_ref, y_ref, g_ref, b_ref, o_ref, *stream_refs, dilations):
    z = DEEPNORM_ALPHA * x_ref[...] + y_ref[...].astype(F32)
    mu = jnp.mean(z, axis=-1, keepdims=True)
    zc = z - mu
    var = jnp.mean(zc * zc, axis=-1, keepdims=True)
    out = zc * lax.rsqrt(var + LN_EPS) * g_ref[...] + b_ref[...]
    o_ref[...] = out
    tr = out.shape[0]
    ob = out.astype(BF16)
    dst = lax.broadcasted_iota(jnp.int32, (tr, tr), 0)
    src = lax.broadcasted_iota(jnp.int32, (tr, tr), 1)
    for d, s_ref in zip(dilations, stream_refs):
        if d == 1:
            s_ref[0] = ob
            continue
        per = tr // d
        perm = jnp.where(src == (dst & (per - 1)) * d + (dst >> (per.bit_length() - 1)), 1.0, 0.0).astype(BF16)
        sm = jnp.dot(perm, ob, preferred_element_type=F32).astype(BF16)
        for r in range(d):
            s_ref[r] = sm[r * per:(r + 1) * per]


def _deepnorm_ln(x3, y2, g, b, *, tr, dilations=()):
    B, S, D = x3.shape
    nI = S // tr
    assert S % tr == 0 and all(tr % (d * 16) == 0 and d & (d - 1) == 0 for d in dilations) and tr & (tr - 1) == 0
    row3 = pl.BlockSpec((None, tr, D), lambda b, i: (b, i, 0))
    vec = pl.BlockSpec((1, D), lambda b, i: (0, 0))
    out_shape = [jax.ShapeDtypeStruct((B, S, D), F32)]
    out_specs = [row3]
    for d in dilations:
        out_shape.append(jax.ShapeDtypeStruct((B, d, S // d, D), BF16))
        out_specs.append(pl.BlockSpec((None, d, tr // d, D), lambda b, i: (b, 0, i, 0)))
    return pl.pallas_call(
        functools.partial(_ln_kernel, dilations=tuple(dilations)),
        out_shape=tuple(out_shape),
        grid=(B, nI),
        in_specs=[row3, pl.BlockSpec((tr, D), lambda b, i: (b * nI + i, 0)), vec, vec],
        out_specs=tuple(out_specs),
        compiler_params=_params("parallel", "parallel"),
        name="deepnorm_ln",
    )(x3, y2, g.reshape(1, D), b.reshape(1, D))


def _compress_kernel(zk_ref, zv_ref, pe_k_ref, pe_v_ref, w1ka_ref, w1kb_ref, w1va_ref, w1vb_ref,
                     w2k_ref, w2vt_ref, cos_ref, sin_ref, kc_ref, vct_ref, uk_ref, vk_ref, uv_ref, vv_ref):
    p = pl.program_id(1)
    n = uk_ref.shape[0]

    @pl.when(p == 0)
    def _():
        for ref in (uk_ref, vk_ref, uv_ref, vv_ref):
            ref[...] = jnp.zeros_like(ref)

    def accumulate(z_ref, pe_ref, wa_ref, wb_ref, u_ref, v_ref):
        z = z_ref[...].astype(F32)
        za = (z + pe_ref[pl.ds(p, 1), :]).astype(BF16)
        zb = (z + pe_ref[pl.ds(p + NSA_CMP_STRIDE, 1), :]).astype(BF16)
        u_ref[...] += jnp.dot(za, wa_ref[...], preferred_element_type=F32)
        v_ref[...] += jnp.dot(zb, wb_ref[...], preferred_element_type=F32)

    accumulate(zk_ref, pe_k_ref, w1ka_ref, w1kb_ref, uk_ref, vk_ref)
    accumulate(zv_ref, pe_v_ref, w1va_ref, w1vb_ref, uv_ref, vv_ref)

    @pl.when(p == NSA_CMP_STRIDE - 1)
    def _():
        hid_k = _silu(uk_ref[...] + pltpu.roll(vk_ref[...], n - 1, 0)).astype(BF16)
        hid_v = _silu(uv_ref[...] + pltpu.roll(vv_ref[...], n - 1, 0)).astype(BF16)
        kc = jnp.dot(hid_k, w2k_ref[...], preferred_element_type=F32)
        kc = _rope(kc, cos_ref[...], sin_ref[...])
        row = lax.broadcasted_iota(jnp.int32, kc.shape, 0)
        kc_ref[...] = jnp.where(row < n - 1, kc, 0.0).astype(kc_ref.dtype)
        vct = lax.dot_general(w2vt_ref[...], hid_v, NT_DIMS, preferred_element_type=F32)
        col = lax.broadcasted_iota(jnp.int32, vct.shape, 1)
        vct_ref[...] = jnp.where(col < n - 1, vct, 0.0).astype(vct_ref.dtype)


def _nsa_compress(h2, B, S, pe_k, w1_k, w2_k, pe_v, w1_v, w2_v, kc_block, vc_block):
    G = NSA_KV_GROUPS
    n = S // NSA_CMP_STRIDE
    c0, c1 = kc_block * HEAD_DIM, (vc_block + G) * HEAD_DIM
    nb = (c1 - c0) // HEAD_DIM
    h3 = h2[:, c0:c1].reshape(B, n, NSA_CMP_STRIDE * (c1 - c0))
    kc_block, vc_block = 0, vc_block - kc_block
    cos, sin = _rope_tables(jnp.arange(n) * NSA_CMP_STRIDE + NSA_CMP_LEN - 1)

    def z_spec(block0):
        return pl.BlockSpec((None, n, HEAD_DIM), lambda bg, p: (bg // G, 0, p * nb + block0 + bg % G))

    full = lambda shape: pl.BlockSpec(shape, lambda bg, p: (0,) * len(shape))
    w1a = pl.BlockSpec((HEAD_DIM, HEAD_DIM), lambda bg, p: (p, 0))
    w1b = pl.BlockSpec((HEAD_DIM, HEAD_DIM), lambda bg, p: (p + NSA_CMP_STRIDE, 0))
    acc = pltpu.VMEM((n, HEAD_DIM), F32)
    return pl.pallas_call(
        _compress_kernel,
        out_shape=(jax.ShapeDtypeStruct((B * G, n, HEAD_DIM), BF16), jax.ShapeDtypeStruct((B * G, HEAD_DIM, n), BF16)),
        grid=(B * G, NSA_CMP_STRIDE),
        in_specs=[z_spec(kc_block), z_spec(vc_block), full((NSA_CMP_LEN, HEAD_DIM)), full((NSA_CMP_LEN, HEAD_DIM)),
                  w1a, w1b, w1a, w1b, full((HEAD_DIM, HEAD_DIM)), full((HEAD_DIM, HEAD_DIM)),
                  full((n, HEAD_DIM)), full((n, HEAD_DIM))],
        out_specs=(pl.BlockSpec((None, n, HEAD_DIM), lambda bg, p: (bg, 0, 0)),
                   pl.BlockSpec((None, HEAD_DIM, n), lambda bg, p: (bg, 0, 0))),
        scratch_shapes=[acc, acc, acc, acc],
        compiler_params=_params("parallel", "arbitrary"),
        name="nsa_compress",
    )(h3, h3, pe_k, pe_v, w1_k.astype(BF16), w1_k.astype(BF16), w1_v.astype(BF16), w1_v.astype(BF16),
      w2_k.astype(BF16), w2_v.T.astype(BF16), cos, sin)


def _split3(x):
    hi = x.astype(BF16)
    r1 = x - hi.astype(F32)
    mid = r1.astype(BF16)
    lo = (r1 - mid.astype(F32)).astype(BF16)
    return hi, mid, lo


def _vt_kernel(blk_ref, v_ref, o_ref):
    extra = VT_ROWS - HEAD_DIM
    ones_row = jnp.where(lax.broadcasted_iota(jnp.int32, (extra, HEAD_DIM), 0) == 0, 1.0, 0.0).astype(o_ref.dtype)
    for c in range(o_ref.shape[0]):
        o_ref[c, 0:HEAD_DIM, :] = v_ref[c * HEAD_DIM:(c + 1) * HEAD_DIM, :].astype(F32).T.astype(o_ref.dtype)
        o_ref[c, HEAD_DIM:VT_ROWS, :] = ones_row


def _transpose_values(h2, B, S, blocks, *, rows):
    ns = S // rows
    per = rows // HEAD_DIM
    nv = len(blocks)
    grid_spec = pltpu.PrefetchScalarGridSpec(
        num_scalar_prefetch=1,
        grid=(B, nv, ns),
        in_specs=[pl.BlockSpec((rows, HEAD_DIM), lambda b, i, s, blk: (b * ns + s, blk[i]))],
        out_specs=pl.BlockSpec((None, None, per, VT_ROWS, HEAD_DIM), lambda b, i, s, blk: (b, i, s, 0, 0)),
    )
    return pl.pallas_call(
        _vt_kernel,
        out_shape=jax.ShapeDtypeStruct((B, nv, S // HEAD_DIM, VT_ROWS, HEAD_DIM), BF16),
        grid_spec=grid_spec,
        compiler_params=_params("parallel", "parallel", "arbitrary"),
        name="transpose_values",
    )(jnp.asarray(blocks, jnp.int32), h2)


def _nsa_kernel(q_ref, kc_ref, vct_ref, ks_ref, vst_ref, kw_ref, vwt_ref, gl_ref, gate_ref, ovt_ref,
                o_ref, sel_ref, sig_ref, *, tq, tk, n_sel):
    g = pl.program_id(1)
    qi = pl.program_id(2)
    t0 = qi * tq
    hg = NSA_HEADS_PER_GROUP
    n_cmp = kc_ref.shape[0]
    nbp = NSA_BLK_PAD
    head = lambda h: slice(h * HEAD_DIM, (h + 1) * HEAD_DIM)
    lanes = lambda h: slice(h * tq, (h + 1) * tq)
    per_head = lambda a: jnp.concatenate([a] * hg, axis=1)
    chunk = HEAD_DIM

    q4 = q_ref[...]
    qs = jnp.concatenate([q4[:, head(h)] for h in range(hg)], axis=0)

    def softmax_rows(s, m_floor):
        m = jnp.maximum(jnp.max(s, axis=0, keepdims=True), m_floor)
        e = jnp.exp2(s - m)
        den = jnp.sum(e, axis=0, keepdims=True)
        return e, jnp.where(den > 0.0, 1.0 / den, 0.0)

    t_lane = t0 + lax.broadcasted_iota(jnp.int32, (n_cmp, tq), 1)
    cmp_end = lax.broadcasted_iota(jnp.int32, (n_cmp, tq), 0) * NSA_CMP_STRIDE + (NSA_CMP_LEN - 1)
    cbias = per_head(jnp.where(cmp_end <= t_lane, 0.0, NEG_INF))
    s = lax.dot_general(kc_ref[...], qs, NT_DIMS, preferred_element_type=F32) + cbias
    e, inv = softmax_rows(s, M_INIT)
    p = e * inv
    psum = sum(p[:, lanes(h)] for h in range(hg))
    o_ct = jnp.dot(vct_ref[...], p.astype(BF16), preferred_element_type=F32)

    ovt = ovt_ref[...]
    imp = sum(jnp.dot(ovt, part, preferred_element_type=F32) for part in _split3(psum))
    j_idx = lax.broadcasted_iota(jnp.int32, (nbp, tq), 0)
    cur = (t0 + lax.broadcasted_iota(jnp.int32, (nbp, tq), 1)) >> 6
    forced = (j_idx == 0) | (j_idx == cur) | (j_idx == cur - 1)
    imp = jnp.where(forced, imp + NSA_FORCE_BONUS, imp)
    imp = jnp.where(j_idx <= cur, imp, NEG_INF)

    band = NSA_WINDOW + tq
    wc = jnp.maximum(qi - NSA_WINDOW // tq, 0)
    w0 = pl.multiple_of(wc * tq, tq)
    kb = kw_ref[pl.ds(w0, band), :]
    vbt = jnp.concatenate([vwt_ref[wc + c, 0:HEAD_DIM, :] for c in range(band // chunk)], axis=1)
    rel = (t0 + lax.broadcasted_iota(jnp.int32, (band, tq), 1)) - (w0 + lax.broadcasted_iota(jnp.int32, (band, tq), 0))
    wbias = per_head(jnp.where((rel >= 0) & (rel < NSA_WINDOW), 0.0, NEG_INF))
    s = lax.dot_general(kb, qs, NT_DIMS, preferred_element_type=F32) + wbias
    e, inv = softmax_rows(s, M_INIT)
    o_wt = jnp.dot(vbt, e.astype(BF16), preferred_element_type=F32) * inv

    j_f = j_idx.astype(F32)
    picked = jnp.zeros((nbp, tq), F32)
    for _ in range(n_sel):
        top = jnp.max(imp, axis=0, keepdims=True)
        first = jnp.min(jnp.where(imp == top, j_f, float(nbp)), axis=0, keepdims=True)
        hit = j_f == first
        picked = jnp.where(hit, 1.0, picked)
        imp = jnp.where(hit, TAKEN, imp)
    sel_ref[...] = jnp.where(j_idx <= cur, picked, 0.0)

    n_pairs = hg // 2
    key_i = lax.broadcasted_iota(jnp.int32, (SEL_SUB, tq), 0)
    t_i = t0 + lax.broadcasted_iota(jnp.int32, (SEL_SUB, tq), 1)
    blocks_per_tile = tk // NSA_SEL_LEN
    blocks_per_sub = SEL_SUB // NSA_SEL_LEN
    q_pair = [qs[pr * 2 * tq:(pr + 1) * 2 * tq] for pr in range(n_pairs)]

    def sel_scores(kt):
        k0 = pl.multiple_of(kt * tk, tk)
        sel8 = sel_ref[pl.ds(pl.multiple_of(kt * blocks_per_tile, blocks_per_tile), blocks_per_tile), :]
        scores = []
        for sub in range(tk // SEL_SUB):
            ks0 = k0 + sub * SEL_SUB
            k = ks_ref[pl.ds(pl.multiple_of(ks0, SEL_SUB), SEL_SUB), :]
            chosen = jnp.concatenate([jnp.broadcast_to(sel8[sub * blocks_per_sub + b:sub * blocks_per_sub + b + 1, :],
                                                       (NSA_SEL_LEN, tq)) for b in range(blocks_per_sub)], axis=0)
            bias = jnp.where((chosen > 0.5) & (key_i + ks0 <= t_i), 0.0, NEG_INF)
            bias2 = jnp.concatenate([bias, bias], axis=1)
            for pr in range(n_pairs):
                scores.append(lax.dot_general(k, q_pair[pr], NT_DIMS, preferred_element_type=F32) + bias2)
        return tuple(scores)

    def sel_tiles(kt0, count, carry):
        m, acc = list(carry[0]), list(carry[1])
        scores = [sel_scores(kt0 + u) for u in range(count)]
        for u in range(count):
            for sub in range(tk // SEL_SUB):
                c0 = (kt0 + u) * (tk // chunk) + sub * (SEL_SUB // chunk)
                vt = jnp.concatenate([vst_ref[c0 + c] for c in range(SEL_SUB // chunk)], axis=1)
                for pr in range(n_pairs):
                    s = scores[u][sub * n_pairs + pr]
                    m_new = jnp.maximum(m[pr], jnp.max(s, axis=0, keepdims=True))
                    p = jnp.exp2(s - m_new).astype(BF16)
                    acc[pr] = jnp.exp2(m[pr] - m_new) * acc[pr] + jnp.dot(vt, p, preferred_element_type=F32)
                    m[pr] = m_new
        return tuple(m), tuple(acc)

    n_tiles = (t0 + tq + tk - 1) // tk
    n_groups = n_tiles // SEL_UNROLL
    carry = (tuple(jnp.full((1, 2 * tq), M_INIT, F32) for _ in range(n_pairs)),
             tuple(jnp.zeros((VT_ROWS, 2 * tq), F32) for _ in range(n_pairs)))
    carry = lax.fori_loop(0, n_groups, lambda it, c: sel_tiles(it * SEL_UNROLL, SEL_UNROLL, c), carry)
    _, acc_pairs = lax.fori_loop(n_groups * SEL_UNROLL, n_tiles, lambda kt, c: sel_tiles(kt, 1, c), carry)
    acc = jnp.concatenate(acc_pairs, axis=1)
    l_s = acc[HEAD_DIM:HEAD_DIM + 1, :]
    o_st = acc[:HEAD_DIM, :] * jnp.where(l_s > 0.0, 1.0 / l_s, 0.0)

    sig_ref[...] = (1.0 / (1.0 + jnp.exp(-gl_ref[...]))).T
    gate = gate_ref[...].astype(F32)
    for h in range(hg):
        base = (g * hg + h) * 3
        gc, gs, gw = (sig_ref[pl.ds(base + c, 1), :] for c in range(3))
        o_t = gc * o_ct[:, lanes(h)] + gs * o_st[:, lanes(h)] + gw * o_wt[:, lanes(h)]
        o_ref[:, head(h)] = (o_t.T * _silu(gate[:, head(h)])).astype(o_ref.dtype)


def _nsa_attention(h2, small, kc, vct, vt, B, S, *, q_tile0, kv_block0, gate_tile0, tq, tk):
    T = h2.shape[0]
    G, hg = NSA_KV_GROUPS, NSA_HEADS_PER_GROUP
    n_cmp = S // NSA_CMP_STRIDE
    n_blk = S // NSA_SEL_LEN
    nQ = S // tq
    gw = hg * HEAD_DIM
    assert n_blk <= NSA_BLK_PAD and S % tk == 0 and S >= NSA_WINDOW + tq and tq == HEAD_DIM
    assert tk % SEL_SUB == 0 and (tk // NSA_SEL_LEN) % 8 == 0 and hg % 2 == 0

    nn = np.arange(n_cmp)[None, :]
    jj = np.arange(NSA_BLK_PAD)[:, None]
    ovt = ((nn * NSA_CMP_STRIDE + NSA_CMP_LEN - 1 >= jj * NSA_SEL_LEN) & (nn * NSA_CMP_STRIDE < (jj + 1) * NSA_SEL_LEN)
           & (nn < n_cmp - 1) & (jj < n_blk))

    def k_spec(c):
        return pl.BlockSpec((S, HEAD_DIM), lambda b, g, i: (b, kv_block0 + 4 * c + g))

    def vt_spec(which):
        return pl.BlockSpec((None, None, S // HEAD_DIM, VT_ROWS, HEAD_DIM), lambda b, g, i: (b, which * G + g, 0, 0, 0))

    rows = lambda width, col: pl.BlockSpec((tq, width), lambda b, g, i: (b * nQ + i, col(g)))
    per_bg = lambda shape: pl.BlockSpec((None,) + shape, lambda b, g, i: (b * G + g, 0, 0))
    const = lambda shape: pl.BlockSpec(shape, lambda b, g, i: (0,) * len(shape))
    return pl.pallas_call(
        functools.partial(_nsa_kernel, tq=tq, tk=tk, n_sel=min(NSA_N_SEL, n_blk)),
        out_shape=jax.ShapeDtypeStruct((T, G * gw), BF16),
        grid=(B, G, nQ),
        in_specs=[rows(gw, lambda g: q_tile0 + g), per_bg((n_cmp, HEAD_DIM)), per_bg((HEAD_DIM, n_cmp)),
                  k_spec(2), vt_spec(0), k_spec(4), vt_spec(1),
                  rows(HEAD_DIM, lambda g: 0), rows(gw, lambda g: gate_tile0 + g),
                  const((NSA_BLK_PAD, n_cmp))],
        out_specs=rows(gw, lambda g: g),
        scratch_shapes=[pltpu.VMEM((NSA_BLK_PAD, tq), F32), pltpu.VMEM((HEAD_DIM, tq), F32)],
        compiler_params=_params("parallel", "parallel", "arbitrary"),
        name="nsa_attention",
    )(h2, kc, vct, h2, vt, h2, vt, small, h2, jnp.asarray(ovt, BF16))


def _gla_kernel(q_ref, k_ref, v_ref, a_ref, gate_ref, wa_ref, ba_ref, ng_ref, o_ref, state_ref, intra_ref,
                *, n_chunks, n_heads):
    C = GLA_CHUNK

    @pl.when(pl.program_id(2) == 0)
    def _():
        state_ref[...] = jnp.zeros_like(state_ref)

    r_i = lax.broadcasted_iota(jnp.int32, (C, C), 0)
    c_i = lax.broadcasted_iota(jnp.int32, (C, C), 1)
    causal = r_i >= c_i
    tri = jnp.where(causal, 1.0, 0.0).astype(BF16)
    ng = ng_ref[...]
    heads = range(n_heads)
    kcols = lambda h: slice(h * GLA_DK, (h + 1) * GLA_DK)
    vcols = lambda h: slice(h * GLA_DV, (h + 1) * GLA_DV)

    qe_all, kd_all, decay_all = [], [], []
    for c in range(n_chunks):
        rows = slice(c * C, (c + 1) * C)
        a = a_ref[rows, :]
        pre = [jnp.dot(a, wa_ref[h], preferred_element_type=F32) + ba_ref[h] for h in heads]
        log_a = [(jnp.minimum(p, 0.0) - jnp.log(1.0 + jnp.exp(-jnp.abs(p)))) * (1.0 / GLA_TAU) for p in pre]
        bcum = [sum(jnp.dot(tri, part, preferred_element_type=F32) for part in _split3(la)) for la in log_a]
        b_last = [bc[C - 1:C, :] for bc in bcum]
        q = [q_ref[rows, kcols(h)].astype(F32) for h in heads]
        k = [k_ref[rows, kcols(h)].astype(F32) for h in heads]
        qe = [(q[h] * jnp.exp(bcum[h])).astype(BF16) for h in heads]
        ke = [(k[h] * jnp.exp(-bcum[h])).astype(BF16) for h in heads]
        kd = [(k[h] * jnp.exp(b_last[h] - bcum[h])).astype(BF16) for h in heads]
        attn = [lax.dot_general(qe[h], ke[h], NT_DIMS, preferred_element_type=F32) for h in heads]
        attn = [jnp.where(causal, at, 0.0).astype(BF16) for at in attn]
        for h in heads:
            intra_ref[rows, vcols(h)] = jnp.dot(attn[h], v_ref[rows, vcols(h)], preferred_element_type=F32)
        qe_all.append(qe)
        kd_all.append(kd)
        decay_all.append([jnp.exp(bl) for bl in b_last])

    for c in range(n_chunks):
        rows = slice(c * C, (c + 1) * C)
        state_t = [state_ref[h] for h in heads]
        inter = [lax.dot_general(qe_all[c][h], state_t[h].astype(BF16), NT_DIMS, preferred_element_type=F32)
                 for h in heads]
        upd = [lax.dot_general(v_ref[rows, vcols(h)], kd_all[c][h], TN_DIMS, preferred_element_type=F32) for h in heads]
        for h in heads:
            state_ref[h] = state_t[h] * decay_all[c][h] + upd[h]
        for h in heads:
            o = intra_ref[rows, vcols(h)] + inter[h]
            o = o * lax.rsqrt(jnp.mean(o * o, axis=-1, keepdims=True) + NORM_EPS) * ng
            o_ref[rows, vcols(h)] = (o * _silu(gate_ref[rows, vcols(h)].astype(F32))).astype(o_ref.dtype)


def _gla(h2, small, w_a2, b_a, norm_g, B, S, *, q_col0, k_col0, v_col0, gate_col0, rows, heads_per_step):
    T = h2.shape[0]
    H, hp = GLA_HEADS, heads_per_step
    nR = S // rows
    a_lo = 3 * NSA_KV_GROUPS * NSA_HEADS_PER_GROUP
    assert H % hp == 0 and all(c % (hp * w) == 0 for c, w in ((q_col0, GLA_DK), (k_col0, GLA_DK), (v_col0, GLA_DV),
                                                              (gate_col0, GLA_DV)))
    wa = jnp.zeros((H, HEAD_DIM, GLA_DK), F32).at[:, a_lo:a_lo + GLA_RANK, :].set(
        w_a2.reshape(GLA_RANK, H, GLA_DK).transpose(1, 0, 2))
    blk = lambda width, col0: pl.BlockSpec((rows, hp * width), lambda b, g, i: (b * nR + i, col0 // (hp * width) + g))
    per_head = lambda shape: pl.BlockSpec((hp,) + shape, lambda b, g, i: (g, 0, 0))
    return pl.pallas_call(
        functools.partial(_gla_kernel, n_chunks=rows // GLA_CHUNK, n_heads=hp),
        out_shape=jax.ShapeDtypeStruct((T, H * GLA_DV), BF16),
        grid=(B, H // hp, nR),
        in_specs=[blk(GLA_DK, q_col0), blk(GLA_DK, k_col0), blk(GLA_DV, v_col0),
                  pl.BlockSpec((rows, HEAD_DIM), lambda b, g, i: (b * nR + i, 0)), blk(GLA_DV, gate_col0),
                  per_head((HEAD_DIM, GLA_DK)), per_head((1, GLA_DK)),
                  pl.BlockSpec((1, GLA_DV), lambda b, g, i: (0, 0))],
        out_specs=blk(GLA_DV, 0),
        scratch_shapes=[pltpu.VMEM((hp, GLA_DV, GLA_DK), F32), pltpu.VMEM((rows, hp * GLA_DV), F32)],
        compiler_params=_params("parallel", "parallel", "arbitrary"),
        name="gla",
    )(h2, h2, h2, small, h2, wa, b_a.reshape(H, 1, GLA_DK), norm_g.reshape(1, GLA_DV))


def _dilated_stream(i, q, kp, kc, vp, vc, prev, gate, o_scr, m_scr, d_scr):
    tq = o_scr.shape[0]
    head = lambda h: slice(h * HEAD_DIM, (h + 1) * HEAD_DIM)
    row = lax.broadcasted_iota(jnp.int32, (tq, 2 * tq), 0)
    col = lax.broadcasted_iota(jnp.int32, (tq, 2 * tq), 1)
    valid = (col >= row + (tq - DIL_BACK)) & (col <= row + tq) & ((col >= tq) | (i > 0))
    m_scr[...] = jnp.zeros(m_scr.shape, F32)
    d_scr[...] = jnp.ones(d_scr.shape, F32)
    scores = [lax.dot_general(q[:, head(h)], jnp.concatenate([kp[:, head(h)], kc[:, head(h)]], axis=0), NT_DIMS,
                              preferred_element_type=F32) for h in range(DIL_HEADS)]
    for h in range(DIL_HEADS):
        v = jnp.concatenate([vp[:, head(h)], vc[:, head(h)]], axis=0)
        s = jnp.where(valid, scores[h], NEG_INF)
        m = jnp.max(s, axis=1, keepdims=True)
        e = jnp.exp(s - m)
        o_scr[:, head(h)] = jnp.dot(e.astype(BF16), v, preferred_element_type=F32)
        m_scr[:, h:h + 1] = m
        d_scr[:, h:h + 1] = jnp.sum(e, axis=1, keepdims=True)
    den = d_scr[...]
    lse = m_scr[...] + jnp.log(den)
    w_cur = 1.0 / den
    if prev is not None:
        prev_o, lse_p = prev
        lse_n = jnp.maximum(lse_p, lse) + jnp.log(1.0 + jnp.exp(-jnp.abs(lse_p - lse)))
        w_prev = jnp.exp(lse_p - lse_n)
        w_cur = jnp.exp(lse - lse_n) * w_cur
        lse = lse_n
    outs = []
    for h in range(DIL_HEADS):
        o = o_scr[:, head(h)] * w_cur[:, h:h + 1]
        if prev is not None:
            o = o + prev_o[:, head(h)].astype(F32) * w_prev[:, h:h + 1]
        if gate is not None:
            o = o * _silu(gate[:, head(h)].astype(F32))
        outs.append(o)
    return outs, lse


def _dilated_kernel(*refs, has_prev, is_last):
    q_ref, kp_ref, kc_ref, vp_ref, vc_ref = refs[:5]
    rest = list(refs[5:])
    prev = (rest.pop(0), rest.pop(0)) if has_prev else None
    gate_ref = rest.pop(0) if is_last else None
    o_ref = rest.pop(0)
    lse_ref = None if is_last else rest.pop(0)
    o_scr, m_scr, d_scr = rest
    if prev is not None:
        prev = (prev[0], prev[1][...])
    outs, lse = _dilated_stream(pl.program_id(2), q_ref, kp_ref, kc_ref, vp_ref, vc_ref, prev, gate_ref,
                                o_scr, m_scr, d_scr)
    for h, o in enumerate(outs):
        o_ref[:, h * HEAD_DIM:(h + 1) * HEAD_DIM] = o.astype(o_ref.dtype)
    if lse_ref is not None:
        lse_ref[...] = lse


def _dilated_scratch(tq, W):
    return [pltpu.VMEM((tq, W), F32), pltpu.VMEM((tq, HEAD_DIM), F32), pltpu.VMEM((tq, HEAD_DIM), F32)]


def _dilated_group(hg, prev_o, prev_lse, gate, *, tq):
    B, d, L, W3 = hg.shape
    W = W3 // 3
    has_prev, is_last = prev_o is not None, gate is not None
    assert tq == DIL_BACK and L % tq == 0
    cur = lambda col: pl.BlockSpec((None, None, tq, W), lambda b, r, i: (b, r, i, col))
    prv = lambda col: pl.BlockSpec((None, None, tq, W), lambda b, r, i: (b, r, jnp.maximum(i - 1, 0), col))
    tok = lambda width: pl.BlockSpec((None, tq, width), lambda b, r, i: (b, i, r))
    in_specs = [cur(0), prv(1), cur(1), prv(2), cur(2)]
    args = [hg] * 5
    if has_prev:
        in_specs += [tok(W), tok(HEAD_DIM)]
        args += [prev_o, prev_lse]
    if is_last:
        in_specs += [tok(W)]
        args += [gate]
    out_shape = [jax.ShapeDtypeStruct((B, L, d * W), BF16)]
    out_specs = [tok(W)]
    if not is_last:
        out_shape += [jax.ShapeDtypeStruct((B, L, d * HEAD_DIM), F32)]
        out_specs += [tok(HEAD_DIM)]
    res = pl.pallas_call(
        functools.partial(_dilated_kernel, has_prev=has_prev, is_last=is_last),
        out_shape=tuple(out_shape),
        grid=(B, d, L // tq),
        in_specs=in_specs,
        out_specs=tuple(out_specs),
        scratch_shapes=_dilated_scratch(tq, W),
        compiler_params=_params("parallel", "parallel", "arbitrary"),
        name=f"dilated_d{d}",
    )(*args)
    return res[0] if is_last else (res[0], res[1])


def _regroup_kernel(*refs, d, ratio):
    q_ref, kp_ref, kc_ref, vp_ref, vc_ref = refs[:5]
    po_refs = refs[5:5 + ratio]
    pl_refs = refs[5 + ratio:5 + 2 * ratio]
    o_ref, lse_ref, o_scr, m_scr, d_scr, so_scr, sl_scr = refs[5 + 2 * ratio:]
    i = pl.program_id(1)
    r = pl.program_id(2)
    tq, W = o_scr.shape
    rows_c = tq // ratio
    n = d * tq
    shift = lambda v: v.bit_length() - 1
    dst = lax.broadcasted_iota(jnp.int32, (tq, tq), 0)
    src = lax.broadcasted_iota(jnp.int32, (tq, tq), 1)
    gather = jnp.where(src == (dst & (ratio - 1)) * rows_c + (dst >> shift(ratio)), 1.0, 0.0).astype(BF16)
    prev_o = jnp.dot(gather, jnp.concatenate([p[...] for p in po_refs], axis=0), preferred_element_type=F32)
    prev_l = sum(jnp.dot(gather, part, preferred_element_type=F32)
                 for part in _split3(jnp.concatenate([p[...] for p in pl_refs], axis=0)))
    outs, lse = _dilated_stream(i, q_ref, kp_ref, kc_ref, vp_ref, vc_ref, (prev_o, prev_l), None,
                                o_scr, m_scr, d_scr)
    so_scr[r] = jnp.concatenate(outs, axis=1).astype(BF16)
    sl_scr[r] = lse

    @pl.when(r == d - 1)
    def _():
        dst = lax.broadcasted_iota(jnp.int32, (n, n), 0)
        src = lax.broadcasted_iota(jnp.int32, (n, n), 1)
        scatter = jnp.where(src == (dst & (d - 1)) * tq + (dst >> shift(d)), 1.0, 0.0).astype(BF16)
        o_ref[...] = jnp.dot(scatter, so_scr[...].reshape(n, W), preferred_element_type=F32).astype(o_ref.dtype)
        lse_ref[...] = sum(jnp.dot(scatter, part, preferred_element_type=F32)
                           for part in _split3(sl_scr[...].reshape(n, HEAD_DIM)))


def _dilated_regroup(hg, prev_o, prev_lse, *, fine, tq):
    B, d, L, W3 = hg.shape
    W = W3 // 3
    S = L * d
    ratio = fine // d
    assert tq == DIL_BACK and L % tq == 0 and fine % d == 0 and tq % ratio == 0
    assert d & (d - 1) == 0 and ratio & (ratio - 1) == 0
    rows_c = tq // ratio
    cur = lambda col: pl.BlockSpec((None, None, tq, W), lambda b, i, r: (b, r, i, col))
    prv = lambda col: pl.BlockSpec((None, None, tq, W), lambda b, i, r: (b, r, jnp.maximum(i - 1, 0), col))
    piece = lambda width: [pl.BlockSpec((None, rows_c, width), lambda b, i, r, j=j: (b, i, j * d + r))
                           for j in range(ratio)]
    return pl.pallas_call(
        functools.partial(_regroup_kernel, d=d, ratio=ratio),
        out_shape=(jax.ShapeDtypeStruct((B, S, W), BF16), jax.ShapeDtypeStruct((B, S, HEAD_DIM), F32)),
        grid=(B, L // tq, d),
        in_specs=[cur(0), prv(1), cur(1), prv(2), cur(2)] + piece(W) + piece(HEAD_DIM),
        out_specs=(pl.BlockSpec((None, d * tq, W), lambda b, i, r: (b, i, 0)),
                   pl.BlockSpec((None, d * tq, HEAD_DIM), lambda b, i, r: (b, i, 0))),
        scratch_shapes=_dilated_scratch(tq, W) + [pltpu.VMEM((d, tq, W), BF16), pltpu.VMEM((d, tq, HEAD_DIM), F32)],
        compiler_params=_params("parallel", "arbitrary", "arbitrary"),
        name=f"dilated_regroup_d{d}",
    )(*([hg] * 5 + [prev_o] * ratio + [prev_lse] * ratio))


def _row_tile(n, target):
    t = min(n, target)
    assert n % t == 0
    return t


def _even_layer(xb4, w_in, pe_k, w1_k, w2_k, pe_v, w1_v, w2_v, gla_w_a2, gla_b_a, gla_norm_g, w_out):
    B, _, S, D = xb4.shape
    T = B * S
    nq = NSA_KV_GROUPS * NSA_HEADS_PER_GROUP * HEAD_DIM
    nkv = 6 * NSA_KV_GROUPS * HEAD_DIM
    ng = 3 * NSA_KV_GROUPS * NSA_HEADS_PER_GROUP
    gk = GLA_HEADS * GLA_DK
    gv = GLA_HEADS * GLA_DV
    o_g = nq + nkv
    o_a = o_g + ng + nq + 2 * gk + gv
    w_main = jnp.concatenate([w_in[:, :o_g], w_in[:, o_g + ng:o_a], w_in[:, o_a + GLA_RANK:]], axis=1).astype(BF16)
    w_small = jnp.concatenate([w_in[:, o_g:o_g + ng], w_in[:, o_a:o_a + GLA_RANK],
                               jnp.zeros((D, HEAD_DIM - ng - GLA_RANK), w_in.dtype)], axis=1).astype(BF16)
    tn = 512
    plain, rope = (False, 1.0), (True, 1.0)
    groups = [("nsa_q", [(True, Q_SCALE * LOG2E)] * (nq // tn)),
              ("nsa_kv", [plain, plain, rope, plain, rope, plain]),
              ("nsa_gate", [plain] * (nq // tn)), ("gla_q", [(False, GLA_Q_SCALE)] * (gk // tn)),
              ("gla_k", [plain] * (gk // tn)), ("gla_v", [plain] * (gv // tn)), ("gla_gate", [plain] * (gv // tn))]
    w_block, n_seen = {}, 0
    for name, kinds in groups:
        w_block[name] = n_seen
        n_seen += len(kinds)
    order = ["nsa_q", "gla_v", "gla_gate", "nsa_gate", "gla_q", "gla_k", "nsa_kv"]
    tiles, col = [], {}
    for name in order:
        col[name] = len(tiles) * tn
        tiles += [(w_block[name] + t,) + kind for t, kind in enumerate(dict(groups)[name])]
    pos = jnp.arange(S)[None]
    tm = _row_tile(S, 1024)
    h2 = _project(xb4, w_main, tiles, pos, tm=tm, tn=tn, out_dtype=BF16).reshape(T, -1)
    small = _project(xb4, w_small, [(0,) + plain], pos, tm=tm, tn=HEAD_DIM, out_dtype=F32).reshape(T, HEAD_DIM)

    kv_block0 = col["nsa_kv"] // HEAD_DIM
    kc, vct = _nsa_compress(h2, B, S, pe_k, w1_k, w2_k, pe_v, w1_v, w2_v, kv_block0, kv_block0 + NSA_KV_GROUPS)
    v_blocks = [kv_block0 + 4 * c + g for c in (3, 5) for g in range(NSA_KV_GROUPS)]
    vt = _transpose_values(h2, B, S, v_blocks, rows=_row_tile(S, 1024))
    o_nsa = _nsa_attention(h2, small, kc, vct, vt, B, S, q_tile0=col["nsa_q"] // tn, kv_block0=kv_block0,
                           gate_tile0=col["nsa_gate"] // tn, tq=128, tk=_row_tile(S, 512))
    o_gla = _gla(h2, small, gla_w_a2, gla_b_a, gla_norm_g, B, S, q_col0=col["gla_q"], k_col0=col["gla_k"],
                 v_col0=col["gla_v"], gate_col0=col["gla_gate"], rows=_row_tile(S, 512), heads_per_step=GLA_HEADS)
    return _out_project([o_nsa, o_gla], w_out, tm=_row_tile(T, 1024), tn=512)


def _odd_layer(streams, w_in, w_out):
    B, _, S, D = streams[1].shape
    T = B * S
    n_g = len(DIL_GROUPS)
    W = DIL_HEADS * HEAD_DIM
    tn = 512
    blocks = lambda col0: [col0 // tn + t for t in range(W // tn)]
    gate = _project(streams[1], w_in, [(blk, False, 1.0) for blk in blocks(3 * n_g * W)], jnp.arange(S)[None],
                    tm=_row_tile(S, 1024), tn=tn, out_dtype=BF16).reshape(B, S, W)

    def group_qkv(gi, d):
        L = S // d
        tiles = ([(blk, True, Q_SCALE) for blk in blocks(gi * W)] + [(blk, True, 1.0) for blk in blocks((n_g + gi) * W)]
                 + [(blk, False, 1.0) for blk in blocks((2 * n_g + gi) * W)])
        tm = _row_tile(L, 1024)
        return _project(streams[d], w_in, tiles, jnp.arange(S).reshape(L, d).T, tm=tm, tn=tn, out_dtype=BF16,
                        streams_per_step=min(d, 1024 // tm) if tm == L else 1, n_split=max(1, tm // 256))

    order = sorted(range(n_g), key=lambda gi: -DIL_GROUPS[gi][1])
    assert len(order) == 3 and DIL_GROUPS[order[-1]][1] == 1
    for gi in order:
        assert DIL_GROUPS[gi][0] // DIL_GROUPS[gi][1] == DIL_BACK
    d_hi, d_mid = DIL_GROUPS[order[0]][1], DIL_GROUPS[order[1]][1]
    o, lse = _dilated_group(group_qkv(order[0], d_hi), None, None, None, tq=DIL_BACK)
    o, lse = _dilated_regroup(group_qkv(order[1], d_mid), o, lse, fine=d_hi, tq=DIL_BACK)
    o = _dilated_group(group_qkv(order[2], 1), o, lse, gate, tq=DIL_BACK)
    return _out_project([o.reshape(T, W)], w_out, tm=_row_tile(T, 1024), tn=512)


def kernel(x, l0_w_in, l0_nsa_pe_k, l0_nsa_w1_k, l0_nsa_w2_k, l0_nsa_pe_v, l0_nsa_w1_v, l0_nsa_w2_v, l0_gla_w_a2, l0_gla_b_a, l0_gla_norm_g, l0_w_out, l0_ln_g, l0_ln_b, l1_w_in, l1_w_out, l1_ln_g, l1_ln_b):
    B, S, D = x.shape
    tr = _row_tile(S, 256)
    dils = sorted({d for _, d in DIL_GROUPS} | {1})
    y0 = _even_layer(x.astype(BF16)[:, None], l0_w_in, l0_nsa_pe_k, l0_nsa_w1_k, l0_nsa_w2_k, l0_nsa_pe_v, l0_nsa_w1_v,
                     l0_nsa_w2_v, l0_gla_w_a2, l0_gla_b_a, l0_gla_norm_g, l0_w_out)
    x1, *x1_streams = _deepnorm_ln(x, y0, l0_ln_g, l0_ln_b, tr=tr, dilations=dils)
    y1 = _odd_layer(dict(zip(dils, x1_streams)), l1_w_in, l1_w_out)
    return _deepnorm_ln(x1, y1, l1_ln_g, l1_ln_b, tr=tr)[0]
```

```python
import functools

import numpy as np
import jax
import jax.numpy as jnp
from jax import lax
from jax.experimental import pallas as pl
from jax.experimental.pallas import tpu as pltpu

F32 = jnp.float32
BF16 = jnp.bfloat16

HEAD_DIM = 128
ROPE_THETA = 10000.0
LN_EPS = 1e-5
NORM_EPS = 1e-6
NEG_INF = -1e30
M_INIT = -1e29
TAKEN = -3e38
SEL_UNROLL = 4
SEL_SUB = 256
VT_ROWS = HEAD_DIM + 16

NSA_KV_GROUPS = 4
NSA_HEADS_PER_GROUP = 4
NSA_CMP_LEN = 32
NSA_CMP_STRIDE = 16
NSA_SEL_LEN = 64
NSA_N_SEL = 16
NSA_WINDOW = 512
NSA_FORCE_BONUS = 1e4
NSA_BLK_PAD = 128

GLA_HEADS = 4
GLA_DK = 256
GLA_DV = 512
GLA_RANK = 16
GLA_TAU = 16.0
GLA_CHUNK = 64

DIL_GROUPS = ((128, 1), (512, 4), (2048, 16))
DIL_HEADS = 16
DIL_BACK = 128

DEPTH = 2
DEEPNORM_ALPHA = (2.0 * DEPTH) ** 0.25

VMEM_LIMIT_BYTES = 56 * 1024 * 1024

Q_SCALE = HEAD_DIM ** -0.5
LOG2E = 1.4426950408889634
GLA_Q_SCALE = GLA_DK ** -0.5

NT_DIMS = (((1,), (1,)), ((), ()))
TN_DIMS = (((0,), (0,)), ((), ()))


def _params(*sem):
    return pltpu.CompilerParams(dimension_semantics=sem, vmem_limit_bytes=VMEM_LIMIT_BYTES)


def _silu(x):
    return x * (1.0 / (1.0 + jnp.exp(-x)))


def _rope_tables(pos):
    half = HEAD_DIM // 2
    inv_freq = ROPE_THETA ** (-jnp.arange(half, dtype=F32) / half)
    ang = pos.astype(F32)[:, None] * inv_freq[None, :]
    cos = jnp.cos(ang)
    sin = jnp.sin(ang)
    return jnp.concatenate([cos, cos], axis=-1), jnp.concatenate([-sin, sin], axis=-1)


def _rope(a, cos, sin):
    return a * cos + pltpu.roll(a, HEAD_DIM // 2, 1) * sin


def _proj_kernel(wblk_ref, tbase_ref, tstep_ref, a_ref, w_ref, c_ref, s_ref, o_ref, *, n_split):
    w = w_ref[...].astype(BF16)
    sb, tm, _ = a_ref.shape
    chunk = tm // n_split
    for st in range(sb):
        for h in range(n_split):
            rows = slice(h * chunk, (h + 1) * chunk)
            trows = slice(st * tm + h * chunk, st * tm + (h + 1) * chunk)
            acc = jnp.dot(a_ref[st, rows, :], w, preferred_element_type=F32)
            c = c_ref[trows, :]
            s = s_ref[trows, :]
            for k in range(acc.shape[1] // HEAD_DIM):
                sl = slice(k * HEAD_DIM, (k + 1) * HEAD_DIM)
                o_ref[st, rows, sl] = (acc[:, sl] * c + pltpu.roll(acc[:, sl], HEAD_DIM // 2, 1) * s).astype(o_ref.dtype)


def _project(act4, w, tiles, pos, *, tm, tn, out_dtype, streams_per_step=1, n_split=4):
    B, d, L, K = act4.shape
    sb = streams_per_step
    nI, nJ, nR = L // tm, len(tiles), d // sb
    assert L % tm == 0 and pos.shape == (d, L) and d % sb == 0 and (sb == 1 or nI == 1) and tm % n_split == 0
    tr = sb * tm
    kinds = sorted({(rope, scale) for _, rope, scale in tiles})
    cos, sin = _rope_tables(pos.reshape(-1))
    c_rows, s_rows, base = [], [], {}
    n_blocks = 0
    for rope, scale in kinds:
        base[(rope, scale)] = n_blocks
        if rope:
            c_rows += [cos * scale]
            s_rows += [sin * scale]
            n_blocks += nR * nI
        else:
            c_rows += [jnp.full((tr, HEAD_DIM), scale, F32)]
            s_rows += [jnp.zeros((tr, HEAD_DIM), F32)]
            n_blocks += 1
    c_tab, s_tab = jnp.concatenate(c_rows, axis=0), jnp.concatenate(s_rows, axis=0)
    wblk = jnp.asarray([t[0] for t in tiles], jnp.int32)
    tbase = jnp.asarray([base[(t[1], t[2])] for t in tiles], jnp.int32)
    tstep = jnp.asarray([int(t[1]) for t in tiles], jnp.int32)

    def split(m):
        return m // (nR * nI), (m // nI) % nR, m % nI

    def a_idx(m, j, wb, tb, ts):
        b, rg, i = split(m)
        return (b, rg, i, 0)

    def t_idx(m, j, wb, tb, ts):
        b, rg, i = split(m)
        return (tb[j] + ts[j] * (rg * nI + i), 0)

    def o_idx(m, j, wb, tb, ts):
        b, rg, i = split(m)
        return (b, rg, i, j)

    grid_spec = pltpu.PrefetchScalarGridSpec(
        num_scalar_prefetch=3,
        grid=(B * nR * nI, nJ),
        in_specs=[
            pl.BlockSpec((None, sb, tm, K), a_idx),
            pl.BlockSpec((K, tn), lambda m, j, wb, tb, ts: (0, wb[j])),
            pl.BlockSpec((tr, HEAD_DIM), t_idx),
            pl.BlockSpec((tr, HEAD_DIM), t_idx),
        ],
        out_specs=pl.BlockSpec((None, sb, tm, tn), o_idx),
    )
    return pl.pallas_call(
        functools.partial(_proj_kernel, n_split=n_split),
        out_shape=jax.ShapeDtypeStruct((B, d, L, nJ * tn), out_dtype),
        grid_spec=grid_spec,
        compiler_params=_params("parallel", "arbitrary"),
        name=f"proj_d{d}_n{nJ * tn}",
    )(wblk, tbase, tstep, act4, w, c_tab, s_tab)


def _outproj_kernel(*refs, n_split):
    o_ref = refs[-1]
    ws = [w_ref[...].astype(BF16) for w_ref in refs[1:-1:2]]
    chunk = o_ref.shape[0] // n_split
    for h in range(n_split):
        rows = slice(h * chunk, (h + 1) * chunk)
        acc = None
        for a_ref, w in zip(refs[0:-1:2], ws):
            part = jnp.dot(a_ref[rows, :], w, preferred_element_type=F32)
            acc = part if acc is None else acc + part
        o_ref[rows, :] = acc.astype(o_ref.dtype)


def _out_project(acts, w, *, tm, tn, n_split=4):
    T, K = acts[0].shape
    N = w.shape[1]
    assert all(a.shape == (T, K) for a in acts) and w.shape[0] == len(acts) * K
    in_specs, args = [], []
    for n, a in enumerate(acts):
        in_specs += [pl.BlockSpec((tm, K), lambda i, j: (i, 0)), pl.BlockSpec((K, tn), lambda i, j, n=n: (n, j))]
        args += [a, w]
    return pl.pallas_call(
        functools.partial(_outproj_kernel, n_split=n_split),
        out_shape=jax.ShapeDtypeStruct((T, N), BF16),
        grid=(T // tm, N // tn),
        in_specs=in_specs,
        out_specs=pl.BlockSpec((tm, tn), lambda i, j: (i, j)),
        compiler_params=_params("parallel", "arbitrary"),
        name=f"outproj_{len(acts)}",
    )(*args)


def _ln_kernel(x_ref, y_ref, g_ref, b_ref, o_ref, *stream_refs, dilations):
    z = DEEPNORM_ALPHA * x_ref[...] + y_ref[...].astype(F32)
    mu = jnp.mean(z, axis=-1, keepdims=True)
    zc = z - mu
    var = jnp.mean(zc * zc, axis=-1, keepdims=True)
    out = zc * lax.rsqrt(var + LN_EPS) * g_ref[...] + b_ref[...]
    o_ref[...] = out
    tr = out.shape[0]
    ob = out.astype(BF16)
    dst = lax.broadcasted_iota(jnp.int32, (tr, tr), 0)
    src = lax.broadcasted_iota(jnp.int32, (tr, tr), 1)
    for d, s_ref in zip(dilations, stream_refs):
        if d == 1:
            s_ref[0] = ob
            continue
        per = tr // d
        perm = jnp.where(src == (dst & (per - 1)) * d + (dst >> (per.bit_length() - 1)), 1.0, 0.0).astype(BF16)
        sm = jnp.dot(perm, ob, preferred_element_type=F32).astype(BF16)
        for r in range(d):
            s_ref[r] = sm[r * per:(r + 1) * per]


def _deepnorm_ln(x3, y2, g, b, *, tr, dilations=()):
    B, S, D = x3.shape
    nI = S // tr
    assert S % tr == 0 and all(tr % (d * 16) == 0 and d & (d - 1) == 0 for d in dilations) and tr & (tr - 1) == 0
    row3 = pl.BlockSpec((None, tr, D), lambda b, i: (b, i, 0))
    vec = pl.BlockSpec((1, D), lambda b, i: (0, 0))
    out_shape = [jax.ShapeDtypeStruct((B, S, D), F32)]
    out_specs = [row3]
    for d in dilations:
        out_shape.append(jax.ShapeDtypeStruct((B, d, S // d, D), BF16))
        out_specs.append(pl.BlockSpec((None, d, tr // d, D), lambda b, i: (b, 0, i, 0)))
    return pl.pallas_call(
        functools.partial(_ln_kernel, dilations=tuple(dilations)),
        out_shape=tuple(out_shape),
        grid=(B, nI),
        in_specs=[row3, pl.BlockSpec((tr, D), lambda b, i: (b * nI + i, 0)), vec, vec],
        out_specs=tuple(out_specs),
        compiler_params=_params("parallel", "parallel"),
        name="deepnorm_ln",
    )(x3, y2, g.reshape(1, D), b.reshape(1, D))


def _compress_kernel(zk_ref, zv_ref, pe_k_ref, pe_v_ref, w1ka_ref, w1kb_ref, w1va_ref, w1vb_ref,
                     w2k_ref, w2vt_ref, cos_ref, sin_ref, kc_ref, vct_ref, uk_ref, vk_ref, uv_ref, vv_ref):
    p = pl.program_id(1)
    n = uk_ref.shape[0]

    @pl.when(p == 0)
    def _():
        for ref in (uk_ref, vk_ref, uv_ref, vv_ref):
            ref[...] = jnp.zeros_like(ref)

    def accumulate(z_ref, pe_ref, wa_ref, wb_ref, u_ref, v_ref):
        z = z_ref[...].astype(F32)
        za = (z + pe_ref[pl.ds(p, 1), :]).astype(BF16)
        zb = (z + pe_ref[pl.ds(p + NSA_CMP_STRIDE, 1), :]).astype(BF16)
        u_ref[...] += jnp.dot(za, wa_ref[...], preferred_element_type=F32)
        v_ref[...] += jnp.dot(zb, wb_ref[...], preferred_element_type=F32)

    accumulate(zk_ref, pe_k_ref, w1ka_ref, w1kb_ref, uk_ref, vk_ref)
    accumulate(zv_ref, pe_v_ref, w1va_ref, w1vb_ref, uv_ref, vv_ref)

    @pl.when(p == NSA_CMP_STRIDE - 1)
    def _():
        hid_k = _silu(uk_ref[...] + pltpu.roll(vk_ref[...], n - 1, 0)).astype(BF16)
        hid_v = _silu(uv_ref[...] + pltpu.roll(vv_ref[...], n - 1, 0)).astype(BF16)
        kc = jnp.dot(hid_k, w2k_ref[...], preferred_element_type=F32)
        kc = _rope(kc, cos_ref[...], sin_ref[...])
        row = lax.broadcasted_iota(jnp.int32, kc.shape, 0)
        kc_ref[...] = jnp.where(row < n - 1, kc, 0.0).astype(kc_ref.dtype)
        vct = lax.dot_general(w2vt_ref[...], hid_v, NT_DIMS, preferred_element_type=F32)
        col = lax.broadcasted_iota(jnp.int32, vct.shape, 1)
        vct_ref[...] = jnp.where(col < n - 1, vct, 0.0).astype(vct_ref.dtype)


def _nsa_compress(h2, B, S, pe_k, w1_k, w2_k, pe_v, w1_v, w2_v, kc_block, vc_block):
    G = NSA_KV_GROUPS
    n = S // NSA_CMP_STRIDE
    c0, c1 = kc_block * HEAD_DIM, (vc_block + G) * HEAD_DIM
    nb = (c1 - c0) // HEAD_DIM
    h3 = h2[:, c0:c1].reshape(B, n, NSA_CMP_STRIDE * (c1 - c0))
    kc_block, vc_block = 0, vc_block - kc_block
    cos, sin = _rope_tables(jnp.arange(n) * NSA_CMP_STRIDE + NSA_CMP_LEN - 1)

    def z_spec(block0):
        return pl.BlockSpec((None, n, HEAD_DIM), lambda bg, p: (bg // G, 0, p * nb + block0 + bg % G))

    full = lambda shape: pl.BlockSpec(shape, lambda bg, p: (0,) * len(shape))
    w1a = pl.BlockSpec((HEAD_DIM, HEAD_DIM), lambda bg, p: (p, 0))
    w1b = pl.BlockSpec((HEAD_DIM, HEAD_DIM), lambda bg, p: (p + NSA_CMP_STRIDE, 0))
    acc = pltpu.VMEM((n, HEAD_DIM), F32)
    return pl.pallas_call(
        _compress_kernel,
        out_shape=(jax.ShapeDtypeStruct((B * G, n, HEAD_DIM), BF16), jax.ShapeDtypeStruct((B * G, HEAD_DIM, n), BF16)),
        grid=(B * G, NSA_CMP_STRIDE),
        in_specs=[z_spec(kc_block), z_spec(vc_block), full((NSA_CMP_LEN, HEAD_DIM)), full((NSA_CMP_LEN, HEAD_DIM)),
                  w1a, w1b, w1a, w1b, full((HEAD_DIM, HEAD_DIM)), full((HEAD_DIM, HEAD_DIM)),
                  full((n, HEAD_DIM)), full((n, HEAD_DIM))],
        out_specs=(pl.BlockSpec((None, n, HEAD_DIM), lambda bg, p: (bg, 0, 0)),
                   pl.BlockSpec((None, HEAD_DIM, n), lambda bg, p: (bg, 0, 0))),
        scratch_shapes=[acc, acc, acc, acc],
        compiler_params=_params("parallel", "arbitrary"),
        name="nsa_compress",
    )(h3, h3, pe_k, pe_v, w1_k.astype(BF16), w1_k.astype(BF16), w1_v.astype(BF16), w1_v.astype(BF16),
      w2_k.astype(BF16), w2_v.T.astype(BF16), cos, sin)


def _split3(x):
    hi = x.astype(BF16)
    r1 = x - hi.astype(F32)
    mid = r1.astype(BF16)
    lo = (r1 - mid.astype(F32)).astype(BF16)
    return hi, mid, lo


def _vt_kernel(blk_ref, v_ref, o_ref):
    extra = VT_ROWS - HEAD_DIM
    ones_row = jnp.where(lax.broadcasted_iota(jnp.int32, (extra, HEAD_DIM), 0) == 0, 1.0, 0.0).astype(o_ref.dtype)
    for c in range(o_ref.shape[0]):
        o_ref[c, 0:HEAD_DIM, :] = v_ref[c * HEAD_DIM:(c + 1) * HEAD_DIM, :].astype(F32).T.astype(o_ref.dtype)
        o_ref[c, HEAD_DIM:VT_ROWS, :] = ones_row


def _transpose_values(h2, B, S, blocks, *, rows):
    ns = S // rows
    per = rows // HEAD_DIM
    nv = len(blocks)
    grid_spec = pltpu.PrefetchScalarGridSpec(
        num_scalar_prefetch=1,
        grid=(B, nv, ns),
        in_specs=[pl.BlockSpec((rows, HEAD_DIM), lambda b, i, s, blk: (b * ns + s, blk[i]))],
        out_specs=pl.BlockSpec((None, None, per, VT_ROWS, HEAD_DIM), lambda b, i, s, blk: (b, i, s, 0, 0)),
    )
    return pl.pallas_call(
        _vt_kernel,
        out_shape=jax.ShapeDtypeStruct((B, nv, S // HEAD_DIM, VT_ROWS, HEAD_DIM), BF16),
        grid_spec=grid_spec,
        compiler_params=_params("parallel", "parallel", "arbitrary"),
        name="transpose_values",
    )(jnp.asarray(blocks, jnp.int32), h2)


def _nsa_kernel(q_ref, kc_ref, vct_ref, ks_ref, vst_ref, kw_ref, vwt_ref, gl_ref, gate_ref, ovt_ref, blk_ref,
                o_ref, sig_ref, *, tq, tk, n_sel):
    g = pl.program_id(1)
    qi = pl.program_id(2)
    t0 = qi * tq
    hg = NSA_HEADS_PER_GROUP
    n_cmp = kc_ref.shape[0]
    nbp = NSA_BLK_PAD
    head = lambda h: slice(h * HEAD_DIM, (h + 1) * HEAD_DIM)
    lanes = lambda h: slice(h * tq, (h + 1) * tq)
    per_head = lambda a: jnp.concatenate([a] * hg, axis=1)
    chunk = HEAD_DIM

    q4 = q_ref[...]
    qs = jnp.concatenate([q4[:, head(h)] for h in range(hg)], axis=0)

    def softmax_rows(s, m_floor):
        m = jnp.maximum(jnp.max(s, axis=0, keepdims=True), m_floor)
        e = jnp.exp2(s - m)
        den = jnp.sum(e, axis=0, keepdims=True)
        return e, jnp.where(den > 0.0, 1.0 / den, 0.0)

    t_lane = t0 + lax.broadcasted_iota(jnp.int32, (n_cmp, tq), 1)
    cmp_end = lax.broadcasted_iota(jnp.int32, (n_cmp, tq), 0) * NSA_CMP_STRIDE + (NSA_CMP_LEN - 1)
    cbias = per_head(jnp.where(cmp_end <= t_lane, 0.0, NEG_INF))
    s = lax.dot_general(kc_ref[...], qs, NT_DIMS, preferred_element_type=F32) + cbias
    e, inv = softmax_rows(s, M_INIT)
    p = e * inv
    psum = sum(p[:, lanes(h)] for h in range(hg))
    o_ct = jnp.dot(vct_ref[...], p.astype(BF16), preferred_element_type=F32)

    ovt = ovt_ref[...]
    imp = sum(jnp.dot(ovt, part, preferred_element_type=F32) for part in _split3(psum))
    j_idx = lax.broadcasted_iota(jnp.int32, (nbp, tq), 0)
    cur = (t0 + lax.broadcasted_iota(jnp.int32, (nbp, tq), 1)) >> 6
    forced = (j_idx == 0) | (j_idx == cur) | (j_idx == cur - 1)
    imp = jnp.where(forced, imp + NSA_FORCE_BONUS, imp)
    imp = jnp.where(j_idx <= cur, imp, NEG_INF)

    band = NSA_WINDOW + tq
    wc = jnp.maximum(qi - NSA_WINDOW // tq, 0)
    w0 = pl.multiple_of(wc * tq, tq)
    kb = kw_ref[pl.ds(w0, band), :]
    vbt = jnp.concatenate([vwt_ref[wc + c, 0:HEAD_DIM, :] for c in range(band // chunk)], axis=1)
    rel = (t0 + lax.broadcasted_iota(jnp.int32, (band, tq), 1)) - (w0 + lax.broadcasted_iota(jnp.int32, (band, tq), 0))
    wbias = per_head(jnp.where((rel >= 0) & (rel < NSA_WINDOW), 0.0, NEG_INF))
    s = lax.dot_general(kb, qs, NT_DIMS, preferred_element_type=F32) + wbias
    e, inv = softmax_rows(s, M_INIT)
    o_wt = jnp.dot(vbt, e.astype(BF16), preferred_element_type=F32) * inv

    j_f = j_idx.astype(F32)
    picked = jnp.zeros((nbp, tq), F32)
    for _ in range(n_sel):
        top = jnp.max(imp, axis=0, keepdims=True)
        first = jnp.min(jnp.where(imp == top, j_f, float(nbp)), axis=0, keepdims=True)
        hit = j_f == first
        picked = jnp.where(hit, 1.0, picked)
        imp = jnp.where(hit, TAKEN, imp)
    sel_t = jnp.where(j_idx <= cur, picked, 0.0)

    n_pairs = hg // 2
    key_i = lax.broadcasted_iota(jnp.int32, (SEL_SUB, tq), 0)
    t_i = t0 + lax.broadcasted_iota(jnp.int32, (SEL_SUB, tq), 1)
    sel_bias = jnp.where(sel_t.T > 0.5, 0.0, NEG_INF).astype(BF16)
    sel_bias2 = jnp.concatenate([sel_bias, sel_bias], axis=0)
    q_aug = [jnp.concatenate([qs[pr * 2 * tq:(pr + 1) * 2 * tq], sel_bias2], axis=1) for pr in range(n_pairs)]

    def sel_scores(kt, causal):
        k0 = pl.multiple_of(kt * tk, tk)
        scores = []
        for sub in range(tk // SEL_SUB):
            ks0 = pl.multiple_of(k0 + sub * SEL_SUB, SEL_SUB)
            k_aug = jnp.concatenate([ks_ref[pl.ds(ks0, SEL_SUB), :], blk_ref[pl.ds(ks0, SEL_SUB), :]], axis=1)
            if causal:
                cbias = jnp.where(key_i + ks0 <= t_i, 0.0, NEG_INF)
                cbias2 = jnp.concatenate([cbias, cbias], axis=1)
            for pr in range(n_pairs):
                s = lax.dot_general(k_aug, q_aug[pr], NT_DIMS, preferred_element_type=F32)
                scores.append(s + cbias2 if causal else s)
        return tuple(scores)

    def sel_tiles(kt0, count, carry, diagonal_last=False):
        m, acc = list(carry[0]), list(carry[1])
        scores = [sel_scores(kt0 + u, diagonal_last and u == count - 1) for u in range(count)]
        for u in range(count):
            for sub in range(tk // SEL_SUB):
                c0 = (kt0 + u) * (tk // chunk) + sub * (SEL_SUB // chunk)
                vt = jnp.concatenate([vst_ref[c0 + c] for c in range(SEL_SUB // chunk)], axis=1)
                for pr in range(n_pairs):
                    s = scores[u][sub * n_pairs + pr]
                    m_new = jnp.maximum(m[pr], jnp.max(s, axis=0, keepdims=True))
                    p = jnp.exp2(s - m_new).astype(BF16)
                    acc[pr] = jnp.exp2(m[pr] - m_new) * acc[pr] + jnp.dot(vt, p, preferred_element_type=F32)
                    m[pr] = m_new
        return tuple(m), tuple(acc)

    n_tiles = (t0 + tq + tk - 1) // tk
    n_groups = (n_tiles - 1) // SEL_UNROLL
    n_last = n_tiles - n_groups * SEL_UNROLL
    carry = (tuple(jnp.full((1, 2 * tq), M_INIT, F32) for _ in range(n_pairs)),
             tuple(jnp.zeros((VT_ROWS, 2 * tq), F32) for _ in range(n_pairs)))
    carry = lax.fori_loop(0, n_groups, lambda it, c: sel_tiles(it * SEL_UNROLL, SEL_UNROLL, c), carry)
    last = [functools.partial(sel_tiles, n_groups * SEL_UNROLL, count, diagonal_last=True)
            for count in range(1, SEL_UNROLL + 1)]
    _, acc_pairs = lax.switch(n_last - 1, last, carry)
    acc = jnp.concatenate(acc_pairs, axis=1)
    l_s = acc[HEAD_DIM:HEAD_DIM + 1, :]
    o_st = acc[:HEAD_DIM, :] * jnp.where(l_s > 0.0, 1.0 / l_s, 0.0)

    sig_ref[...] = (1.0 / (1.0 + jnp.exp(-gl_ref[...]))).T
    gate = gate_ref[...].astype(F32)
    for h in range(hg):
        base = (g * hg + h) * 3
        gc, gs, gw = (sig_ref[pl.ds(base + c, 1), :] for c in range(3))
        o_t = gc * o_ct[:, lanes(h)] + gs * o_st[:, lanes(h)] + gw * o_wt[:, lanes(h)]
        o_ref[:, head(h)] = (o_t.T * _silu(gate[:, head(h)])).astype(o_ref.dtype)


def _nsa_attention(h2, small, kc, vct, vt, B, S, *, q_tile0, kv_block0, gate_tile0, tq, tk):
    T = h2.shape[0]
    G, hg = NSA_KV_GROUPS, NSA_HEADS_PER_GROUP
    n_cmp = S // NSA_CMP_STRIDE
    n_blk = S // NSA_SEL_LEN
    nQ = S // tq
    gw = hg * HEAD_DIM
    assert n_blk <= NSA_BLK_PAD and S % tk == 0 and S >= NSA_WINDOW + tq and tq == HEAD_DIM
    assert tk % SEL_SUB == 0 and (tk // NSA_SEL_LEN) % 8 == 0 and hg % 2 == 0

    nn = np.arange(n_cmp)[None, :]
    jj = np.arange(NSA_BLK_PAD)[:, None]
    ovt = ((nn * NSA_CMP_STRIDE + NSA_CMP_LEN - 1 >= jj * NSA_SEL_LEN) & (nn * NSA_CMP_STRIDE < (jj + 1) * NSA_SEL_LEN)
           & (nn < n_cmp - 1) & (jj < n_blk))

    block_onehot = (np.arange(S)[:, None] // NSA_SEL_LEN) == np.arange(NSA_BLK_PAD)[None, :]

    def k_spec(c):
        return pl.BlockSpec((S, HEAD_DIM), lambda b, g, i: (b, kv_block0 + 4 * c + g))

    def vt_spec(which):
        return pl.BlockSpec((None, None, S // HEAD_DIM, VT_ROWS, HEAD_DIM), lambda b, g, i: (b, which * G + g, 0, 0, 0))

    rows = lambda width, col: pl.BlockSpec((tq, width), lambda b, g, i: (b * nQ + i, col(g)))
    per_bg = lambda shape: pl.BlockSpec((None,) + shape, lambda b, g, i: (b * G + g, 0, 0))
    const = lambda shape: pl.BlockSpec(shape, lambda b, g, i: (0,) * len(shape))
    return pl.pallas_call(
        functools.partial(_nsa_kernel, tq=tq, tk=tk, n_sel=min(NSA_N_SEL, n_blk)),
        out_shape=jax.ShapeDtypeStruct((T, G * gw), BF16),
        grid=(B, G, nQ),
        in_specs=[rows(gw, lambda g: q_tile0 + g), per_bg((n_cmp, HEAD_DIM)), per_bg((HEAD_DIM, n_cmp)),
                  k_spec(2), vt_spec(0), k_spec(4), vt_spec(1),
                  rows(HEAD_DIM, lambda g: 0), rows(gw, lambda g: gate_tile0 + g),
                  const((NSA_BLK_PAD, n_cmp)), const((S, NSA_BLK_PAD))],
        out_specs=rows(gw, lambda g: g),
        scratch_shapes=[pltpu.VMEM((HEAD_DIM, tq), F32)],
        compiler_params=_params("parallel", "parallel", "arbitrary"),
        name="nsa_attention",
    )(h2, kc, vct, h2, vt, h2, vt, small, h2, jnp.asarray(ovt, BF16), jnp.asarray(block_onehot, BF16))


def _gla_kernel(q_ref, k_ref, v_ref, a_ref, gate_ref, wa_ref, ba_ref, ng_ref, o_ref, state_ref, intra_ref,
                *, n_chunks, n_heads):
    C = GLA_CHUNK

    @pl.when(pl.program_id(2) == 0)
    def _():
        state_ref[...] = jnp.zeros_like(state_ref)

    r_i = lax.broadcasted_iota(jnp.int32, (C, C), 0)
    c_i = lax.broadcasted_iota(jnp.int32, (C, C), 1)
    causal = r_i >= c_i
    tri = jnp.where(causal, 1.0, 0.0).astype(BF16)
    ng = ng_ref[...]
    heads = range(n_heads)
    kcols = lambda h: slice(h * GLA_DK, (h + 1) * GLA_DK)
    vcols = lambda h: slice(h * GLA_DV, (h + 1) * GLA_DV)

    qe_all, kd_all, decay_all = [], [], []
    for c in range(n_chunks):
        rows = slice(c * C, (c + 1) * C)
        a = a_ref[rows, :]
        pre = [jnp.dot(a, wa_ref[h], preferred_element_type=F32) + ba_ref[h] for h in heads]
        log_a = [(jnp.minimum(p, 0.0) - jnp.log(1.0 + jnp.exp(-jnp.abs(p)))) * (1.0 / GLA_TAU) for p in pre]
        bcum = [sum(jnp.dot(tri, part, preferred_element_type=F32) for part in _split3(la)) for la in log_a]
        b_last = [bc[C - 1:C, :] for bc in bcum]
        q = [q_ref[rows, kcols(h)].astype(F32) for h in heads]
        k = [k_ref[rows, kcols(h)].astype(F32) for h in heads]
        qe = [(q[h] * jnp.exp(bcum[h])).astype(BF16) for h in heads]
        ke = [(k[h] * jnp.exp(-bcum[h])).astype(BF16) for h in heads]
        kd = [(k[h] * jnp.exp(b_last[h] - bcum[h])).astype(BF16) for h in heads]
        attn = [lax.dot_general(qe[h], ke[h], NT_DIMS, preferred_element_type=F32) for h in heads]
        attn = [jnp.where(causal, at, 0.0).astype(BF16) for at in attn]
        for h in heads:
            intra_ref[rows, vcols(h)] = jnp.dot(attn[h], v_ref[rows, vcols(h)], preferred_element_type=F32)
        qe_all.append(qe)
        kd_all.append(kd)
        decay_all.append([jnp.exp(bl) for bl in b_last])

    for c in range(n_chunks):
        rows = slice(c * C, (c + 1) * C)
        state_t = [state_ref[h] for h in heads]
        inter = [lax.dot_general(qe_all[c][h], state_t[h].astype(BF16), NT_DIMS, preferred_element_type=F32)
                 for h in heads]
        upd = [lax.dot_general(v_ref[rows, vcols(h)], kd_all[c][h], TN_DIMS, preferred_element_type=F32) for h in heads]
        for h in heads:
            state_ref[h] = state_t[h] * decay_all[c][h] + upd[h]
        for h in heads:
            o = intra_ref[rows, vcols(h)] + inter[h]
            o = o * lax.rsqrt(jnp.mean(o * o, axis=-1, keepdims=True) + NORM_EPS) * ng
            o_ref[rows, vcols(h)] = (o * _silu(gate_ref[rows, vcols(h)].astype(F32))).astype(o_ref.dtype)


def _gla(h2, small, w_a2, b_a, norm_g, B, S, *, q_col0, k_col0, v_col0, gate_col0, rows, heads_per_step):
    T = h2.shape[0]
    H, hp = GLA_HEADS, heads_per_step
    nR = S // rows
    a_lo = 3 * NSA_KV_GROUPS * NSA_HEADS_PER_GROUP
    assert H % hp == 0 and all(c % (hp * w) == 0 for c, w in ((q_col0, GLA_DK), (k_col0, GLA_DK), (v_col0, GLA_DV),
                                                              (gate_col0, GLA_DV)))
    wa = jnp.zeros((H, HEAD_DIM, GLA_DK), F32).at[:, a_lo:a_lo + GLA_RANK, :].set(
        w_a2.reshape(GLA_RANK, H, GLA_DK).transpose(1, 0, 2))
    blk = lambda width, col0: pl.BlockSpec((rows, hp * width), lambda b, g, i: (b * nR + i, col0 // (hp * width) + g))
    per_head = lambda shape: pl.BlockSpec((hp,) + shape, lambda b, g, i: (g, 0, 0))
    return pl.pallas_call(
        functools.partial(_gla_kernel, n_chunks=rows // GLA_CHUNK, n_heads=hp),
        out_shape=jax.ShapeDtypeStruct((T, H * GLA_DV), BF16),
        grid=(B, H // hp, nR),
        in_specs=[blk(GLA_DK, q_col0), blk(GLA_DK, k_col0), blk(GLA_DV, v_col0),
                  pl.BlockSpec((rows, HEAD_DIM), lambda b, g, i: (b * nR + i, 0)), blk(GLA_DV, gate_col0),
                  per_head((HEAD_DIM, GLA_DK)), per_head((1, GLA_DK)),
                  pl.BlockSpec((1, GLA_DV), lambda b, g, i: (0, 0))],
        out_specs=blk(GLA_DV, 0),
        scratch_shapes=[pltpu.VMEM((hp, GLA_DV, GLA_DK), F32), pltpu.VMEM((rows, hp * GLA_DV), F32)],
        compiler_params=_params("parallel", "parallel", "arbitrary"),
        name="gla",
    )(h2, h2, h2, small, h2, wa, b_a.reshape(H, 1, GLA_DK), norm_g.reshape(1, GLA_DV))


def _dilated_stream(i, q, kp, kc, vp, vc, prev, gate, o_scr, m_scr, d_scr):
    tq = o_scr.shape[0]
    head = lambda h: slice(h * HEAD_DIM, (h + 1) * HEAD_DIM)
    row = lax.broadcasted_iota(jnp.int32, (tq, 2 * tq), 0)
    col = lax.broadcasted_iota(jnp.int32, (tq, 2 * tq), 1)
    valid = (col >= row + (tq - DIL_BACK)) & (col <= row + tq) & ((col >= tq) | (i > 0))
    m_scr[...] = jnp.zeros(m_scr.shape, F32)
    d_scr[...] = jnp.ones(d_scr.shape, F32)
    scores = [lax.dot_general(q[:, head(h)], jnp.concatenate([kp[:, head(h)], kc[:, head(h)]], axis=0), NT_DIMS,
                              preferred_element_type=F32) for h in range(DIL_HEADS)]
    for h in range(DIL_HEADS):
        v = jnp.concatenate([vp[:, head(h)], vc[:, head(h)]], axis=0)
        s = jnp.where(valid, scores[h], NEG_INF)
        m = jnp.max(s, axis=1, keepdims=True)
        e = jnp.exp(s - m)
        o_scr[:, head(h)] = jnp.dot(e.astype(BF16), v, preferred_element_type=F32)
        m_scr[:, h:h + 1] = m
        d_scr[:, h:h + 1] = jnp.sum(e, axis=1, keepdims=True)
    den = d_scr[...]
    lse = m_scr[...] + jnp.log(den)
    w_cur = 1.0 / den
    if prev is not None:
        prev_o, lse_p = prev
        lse_n = jnp.maximum(lse_p, lse) + jnp.log(1.0 + jnp.exp(-jnp.abs(lse_p - lse)))
        w_prev = jnp.exp(lse_p - lse_n)
        w_cur = jnp.exp(lse - lse_n) * w_cur
        lse = lse_n
    outs = []
    for h in range(DIL_HEADS):
        o = o_scr[:, head(h)] * w_cur[:, h:h + 1]
        if prev is not None:
            o = o + prev_o[:, head(h)].astype(F32) * w_prev[:, h:h + 1]
        if gate is not None:
            o = o * _silu(gate[:, head(h)].astype(F32))
        outs.append(o)
    return outs, lse


def _dilated_kernel(*refs, has_prev, is_last):
    q_ref, kp_ref, kc_ref, vp_ref, vc_ref = refs[:5]
    rest = list(refs[5:])
    prev = (rest.pop(0), rest.pop(0)) if has_prev else None
    gate_ref = rest.pop(0) if is_last else None
    o_ref = rest.pop(0)
    lse_ref = None if is_last else rest.pop(0)
    o_scr, m_scr, d_scr = rest
    if prev is not None:
        prev = (prev[0], prev[1][...])
    outs, lse = _dilated_stream(pl.program_id(2), q_ref, kp_ref, kc_ref, vp_ref, vc_ref, prev, gate_ref,
                                o_scr, m_scr, d_scr)
    for h, o in enumerate(outs):
        o_ref[:, h * HEAD_DIM:(h + 1) * HEAD_DIM] = o.astype(o_ref.dtype)
    if lse_ref is not None:
        lse_ref[...] = lse


def _dilated_scratch(tq, W):
    return [pltpu.VMEM((tq, W), F32), pltpu.VMEM((tq, HEAD_DIM), F32), pltpu.VMEM((tq, HEAD_DIM), F32)]


def _dilated_group(hg, prev_o, prev_lse, gate, *, tq):
    B, d, L, W3 = hg.shape
    W = W3 // 3
    has_prev, is_last = prev_o is not None, gate is not None
    assert tq == DIL_BACK and L % tq == 0
    cur = lambda col: pl.BlockSpec((None, None, tq, W), lambda b, r, i: (b, r, i, col))
    prv = lambda col: pl.BlockSpec((None, None, tq, W), lambda b, r, i: (b, r, jnp.maximum(i - 1, 0), col))
    tok = lambda width: pl.BlockSpec((None, tq, width), lambda b, r, i: (b, i, r))
    in_specs = [cur(0), prv(1), cur(1), prv(2), cur(2)]
    args = [hg] * 5
    if has_prev:
        in_specs += [tok(W), tok(HEAD_DIM)]
        args += [prev_o, prev_lse]
    if is_last:
        in_specs += [tok(W)]
        args += [gate]
    out_shape = [jax.ShapeDtypeStruct((B, L, d * W), BF16)]
    out_specs = [tok(W)]
    if not is_last:
        out_shape += [jax.ShapeDtypeStruct((B, L, d * HEAD_DIM), F32)]
        out_specs += [tok(HEAD_DIM)]
    res = pl.pallas_call(
        functools.partial(_dilated_kernel, has_prev=has_prev, is_last=is_last),
        out_shape=tuple(out_shape),
        grid=(B, d, L // tq),
        in_specs=in_specs,
        out_specs=tuple(out_specs),
        scratch_shapes=_dilated_scratch(tq, W),
        compiler_params=_params("parallel", "parallel", "arbitrary"),
        name=f"dilated_d{d}",
    )(*args)
    return res[0] if is_last else (res[0], res[1])


def _regroup_kernel(*refs, d, ratio):
    q_ref, kp_ref, kc_ref, vp_ref, vc_ref = refs[:5]
    po_refs = refs[5:5 + ratio]
    pl_refs = refs[5 + ratio:5 + 2 * ratio]
    o_ref, lse_ref, o_scr, m_scr, d_scr, so_scr, sl_scr = refs[5 + 2 * ratio:]
    i = pl.program_id(1)
    r = pl.program_id(2)
    tq, W = o_scr.shape
    rows_c = tq // ratio
    n = d * tq
    shift = lambda v: v.bit_length() - 1
    dst = lax.broadcasted_iota(jnp.int32, (tq, tq), 0)
    src = lax.broadcasted_iota(jnp.int32, (tq, tq), 1)
    gather = jnp.where(src == (dst & (ratio - 1)) * rows_c + (dst >> shift(ratio)), 1.0, 0.0).astype(BF16)
    prev_o = jnp.dot(gather, jnp.concatenate([p[...] for p in po_refs], axis=0), preferred_element_type=F32)
    prev_l = sum(jnp.dot(gather, part, preferred_element_type=F32)
                 for part in _split3(jnp.concatenate([p[...] for p in pl_refs], axis=0)))
    outs, lse = _dilated_stream(i, q_ref, kp_ref, kc_ref, vp_ref, vc_ref, (prev_o, prev_l), None,
                                o_scr, m_scr, d_scr)
    so_scr[r] = jnp.concatenate(outs, axis=1).astype(BF16)
    sl_scr[r] = lse

    @pl.when(r == d - 1)
    def _():
        dst = lax.broadcasted_iota(jnp.int32, (n, n), 0)
        src = lax.broadcasted_iota(jnp.int32, (n, n), 1)
        scatter = jnp.where(src == (dst & (d - 1)) * tq + (dst >> shift(d)), 1.0, 0.0).astype(BF16)
        o_ref[...] = jnp.dot(scatter, so_scr[...].reshape(n, W), preferred_element_type=F32).astype(o_ref.dtype)
        lse_ref[...] = sum(jnp.dot(scatter, part, preferred_element_type=F32)
                           for part in _split3(sl_scr[...].reshape(n, HEAD_DIM)))


def _dilated_regroup(hg, prev_o, prev_lse, *, fine, tq):
    B, d, L, W3 = hg.shape
    W = W3 // 3
    S = L * d
    ratio = fine // d
    assert tq == DIL_BACK and L % tq == 0 and fine % d == 0 and tq % ratio == 0
    assert d & (d - 1) == 0 and ratio & (ratio - 1) == 0
    rows_c = tq // ratio
    cur = lambda col: pl.BlockSpec((None, None, tq, W), lambda b, i, r: (b, r, i, col))
    prv = lambda col: pl.BlockSpec((None, None, tq, W), lambda b, i, r: (b, r, jnp.maximum(i - 1, 0), col))
    piece = lambda width: [pl.BlockSpec((None, rows_c, width), lambda b, i, r, j=j: (b, i, j * d + r))
                           for j in range(ratio)]
    return pl.pallas_call(
        functools.partial(_regroup_kernel, d=d, ratio=ratio),
        out_shape=(jax.ShapeDtypeStruct((B, S, W), BF16), jax.ShapeDtypeStruct((B, S, HEAD_DIM), F32)),
        grid=(B, L // tq, d),
        in_specs=[cur(0), prv(1), cur(1), prv(2), cur(2)] + piece(W) + piece(HEAD_DIM),
        out_specs=(pl.BlockSpec((None, d * tq, W), lambda b, i, r: (b, i, 0)),
                   pl.BlockSpec((None, d * tq, HEAD_DIM), lambda b, i, r: (b, i, 0))),
        scratch_shapes=_dilated_scratch(tq, W) + [pltpu.VMEM((d, tq, W), BF16), pltpu.VMEM((d, tq, HEAD_DIM), F32)],
        compiler_params=_params("parallel", "arbitrary", "arbitrary"),
        name=f"dilated_regroup_d{d}",
    )(*([hg] * 5 + [prev_o] * ratio + [prev_lse] * ratio))


def _row_tile(n, target):
    t = min(n, target)
    assert n % t == 0
    return t


def _even_layer(xb4, w_in, pe_k, w1_k, w2_k, pe_v, w1_v, w2_v, gla_w_a2, gla_b_a, gla_norm_g, w_out):
    B, _, S, D = xb4.shape
    T = B * S
    nq = NSA_KV_GROUPS * NSA_HEADS_PER_GROUP * HEAD_DIM
    nkv = 6 * NSA_KV_GROUPS * HEAD_DIM
    ng = 3 * NSA_KV_GROUPS * NSA_HEADS_PER_GROUP
    gk = GLA_HEADS * GLA_DK
    gv = GLA_HEADS * GLA_DV
    o_g = nq + nkv
    o_a = o_g + ng + nq + 2 * gk + gv
    w_main = jnp.concatenate([w_in[:, :o_g], w_in[:, o_g + ng:o_a], w_in[:, o_a + GLA_RANK:]], axis=1).astype(BF16)
    w_small = jnp.concatenate([w_in[:, o_g:o_g + ng], w_in[:, o_a:o_a + GLA_RANK],
                               jnp.zeros((D, HEAD_DIM - ng - GLA_RANK), w_in.dtype)], axis=1).astype(BF16)
    tn = 512
    plain, rope = (False, 1.0), (True, 1.0)
    groups = [("nsa_q", [(True, Q_SCALE * LOG2E)] * (nq // tn)),
              ("nsa_kv", [plain, plain, rope, plain, rope, plain]),
              ("nsa_gate", [plain] * (nq // tn)), ("gla_q", [(False, GLA_Q_SCALE)] * (gk // tn)),
              ("gla_k", [plain] * (gk // tn)), ("gla_v", [plain] * (gv // tn)), ("gla_gate", [plain] * (gv // tn))]
    w_block, n_seen = {}, 0
    for name, kinds in groups:
        w_block[name] = n_seen
        n_seen += len(kinds)
    order = ["nsa_q", "gla_v", "gla_gate", "nsa_gate", "gla_q", "gla_k", "nsa_kv"]
    tiles, col = [], {}
    for name in order:
        col[name] = len(tiles) * tn
        tiles += [(w_block[name] + t,) + kind for t, kind in enumerate(dict(groups)[name])]
    pos = jnp.arange(S)[None]
    tm = _row_tile(S, 1024)
    h2 = _project(xb4, w_main, tiles, pos, tm=tm, tn=tn, out_dtype=BF16).reshape(T, -1)
    small = _project(xb4, w_small, [(0,) + plain], pos, tm=tm, tn=HEAD_DIM, out_dtype=F32).reshape(T, HEAD_DIM)

    kv_block0 = col["nsa_kv"] // HEAD_DIM
    kc, vct = _nsa_compress(h2, B, S, pe_k, w1_k, w2_k, pe_v, w1_v, w2_v, kv_block0, kv_block0 + NSA_KV_GROUPS)
    v_blocks = [kv_block0 + 4 * c + g for c in (3, 5) for g in range(NSA_KV_GROUPS)]
    vt = _transpose_values(h2, B, S, v_blocks, rows=_row_tile(S, 1024))
    o_nsa = _nsa_attention(h2, small, kc, vct, vt, B, S, q_tile0=col["nsa_q"] // tn, kv_block0=kv_block0,
                           gate_tile0=col["nsa_gate"] // tn, tq=128, tk=_row_tile(S, 512))
    o_gla = _gla(h2, small, gla_w_a2, gla_b_a, gla_norm_g, B, S, q_col0=col["gla_q"], k_col0=col["gla_k"],
                 v_col0=col["gla_v"], gate_col0=col["gla_gate"], rows=_row_tile(S, 512), heads_per_step=GLA_HEADS)
    return _out_project([o_nsa, o_gla], w_out, tm=_row_tile(T, 1024), tn=512)


def _odd_layer(streams, w_in, w_out):
    B, _, S, D = streams[1].shape
    T = B * S
    n_g = len(DIL_GROUPS)
    W = DIL_HEADS * HEAD_DIM
    tn = 512
    blocks = lambda col0: [col0 // tn + t for t in range(W // tn)]
    gate = _project(streams[1], w_in, [(blk, False, 1.0) for blk in blocks(3 * n_g * W)], jnp.arange(S)[None],
                    tm=_row_tile(S, 1024), tn=tn, out_dtype=BF16).reshape(B, S, W)

    def group_qkv(gi, d):
        L = S // d
        tiles = ([(blk, True, Q_SCALE) for blk in blocks(gi * W)] + [(blk, True, 1.0) for blk in blocks((n_g + gi) * W)]
                 + [(blk, False, 1.0) for blk in blocks((2 * n_g + gi) * W)])
        tm = _row_tile(L, 1024)
        return _project(streams[d], w_in, tiles, jnp.arange(S).reshape(L, d).T, tm=tm, tn=tn, out_dtype=BF16,
                        streams_per_step=min(d, 1024 // tm) if tm == L else 1, n_split=max(1, tm // 256))

    order = sorted(range(n_g), key=lambda gi: -DIL_GROUPS[gi][1])
    assert len(order) == 3 and DIL_GROUPS[order[-1]][1] == 1
    for gi in order:
        assert DIL_GROUPS[gi][0] // DIL_GROUPS[gi][1] == DIL_BACK
    d_hi, d_mid = DIL_GROUPS[order[0]][1], DIL_GROUPS[order[1]][1]
    o, lse = _dilated_group(group_qkv(order[0], d_hi), None, None, None, tq=DIL_BACK)
    o, lse = _dilated_regroup(group_qkv(order[1], d_mid), o, lse, fine=d_hi, tq=DIL_BACK)
    o = _dilated_group(group_qkv(order[2], 1), o, lse, gate, tq=DIL_BACK)
    return _out_project([o.reshape(T, W)], w_out, tm=_row_tile(T, 1024), tn=512)


def kernel(x, l0_w_in, l0_nsa_pe_k, l0_nsa_w1_k, l0_nsa_w2_k, l0_nsa_pe_v, l0_nsa_w1_v, l0_nsa_w2_v, l0_gla_w_a2, l0_gla_b_a, l0_gla_norm_g, l0_w_out, l0_ln_g, l0_ln_b, l1_w_in, l1_w_out, l1_ln_g, l1_ln_b):
    B, S, D = x.shape
    tr = _row_tile(S, 256)
    dils = sorted({d for _, d in DIL_GROUPS} | {1})
    y0 = _even_layer(x.astype(BF16)[:, None], l0_w_in, l0_nsa_pe_k, l0_nsa_w1_k, l0_nsa_w2_k, l0_nsa_pe_v, l0_nsa_w1_v,
                     l0_nsa_w2_v, l0_gla_w_a2, l0_gla_b_a, l0_gla_norm_g, l0_w_out)
    x1, *x1_streams = _deepnorm_ln(x, y0, l0_ln_g, l0_ln_b, tr=tr, dilations=dils)
    y1 = _odd_layer(dict(zip(dils, x1_streams)), l1_w_in, l1_w_out)
    return _deepnorm_ln(x1, y1, l1_ln_g, l1_ln_b, tr=tr)[0]
```

```python
import functools

import numpy as np
import jax
import jax.numpy as jnp
from jax import lax
from jax.experimental import pallas as pl
from jax.experimental.pallas import tpu as pltpu

F32 = jnp.float32
BF16 = jnp.bfloat16

HEAD_DIM = 128
ROPE_THETA = 10000.0
LN_EPS = 1e-5
NORM_EPS = 1e-6
NEG_INF = -1e30
M_INIT = -1e29
TAKEN = -3e38
SEL_UNROLL = 4
SEL_SUB = 256
VT_ROWS = HEAD_DIM + 16

NSA_KV_GROUPS = 4
NSA_HEADS_PER_GROUP = 4
NSA_CMP_LEN = 32
NSA_CMP_STRIDE = 16
NSA_SEL_LEN = 64
NSA_N_SEL = 16
NSA_WINDOW = 512
NSA_FORCE_BONUS = 1e4
NSA_BLK_PAD = 128

GLA_HEADS = 4
GLA_DK = 256
GLA_DV = 512
GLA_RANK = 16
GLA_TAU = 16.0
GLA_CHUNK = 64

DIL_GROUPS = ((128, 1), (512, 4), (2048, 16))
DIL_HEADS = 16
DIL_BACK = 128

DEPTH = 2
DEEPNORM_ALPHA = (2.0 * DEPTH) ** 0.25

VMEM_LIMIT_BYTES = 56 * 1024 * 1024

Q_SCALE = HEAD_DIM ** -0.5
LOG2E = 1.4426950408889634
GLA_Q_SCALE = GLA_DK ** -0.5

NT_DIMS = (((1,), (1,)), ((), ()))
TN_DIMS = (((0,), (0,)), ((), ()))


def _params(*sem):
    return pltpu.CompilerParams(dimension_semantics=sem, vmem_limit_bytes=VMEM_LIMIT_BYTES)


def _silu(x):
    return x * (1.0 / (1.0 + jnp.exp(-x)))


def _rope_tables(pos):
    half = HEAD_DIM // 2
    inv_freq = ROPE_THETA ** (-jnp.arange(half, dtype=F32) / half)
    ang = pos.astype(F32)[:, None] * inv_freq[None, :]
    cos = jnp.cos(ang)
    sin = jnp.sin(ang)
    return jnp.concatenate([cos, cos], axis=-1), jnp.concatenate([-sin, sin], axis=-1)


def _rope(a, cos, sin):
    return a * cos + pltpu.roll(a, HEAD_DIM // 2, 1) * sin


def _proj_kernel(wblk_ref, tbase_ref, tstep_ref, a_ref, w_ref, c_ref, s_ref, o_ref, *, n_split):
    w = w_ref[...].astype(BF16)
    sb, tm, _ = a_ref.shape
    chunk = tm // n_split
    for st in range(sb):
        for h in range(n_split):
            rows = slice(h * chunk, (h + 1) * chunk)
            trows = slice(st * tm + h * chunk, st * tm + (h + 1) * chunk)
            acc = jnp.dot(a_ref[st, rows, :], w, preferred_element_type=F32)
            c = c_ref[trows, :]
            s = s_ref[trows, :]
            for k in range(acc.shape[1] // HEAD_DIM):
                sl = slice(k * HEAD_DIM, (k + 1) * HEAD_DIM)
                o_ref[st, rows, sl] = (acc[:, sl] * c + pltpu.roll(acc[:, sl], HEAD_DIM // 2, 1) * s).astype(o_ref.dtype)


def _project(act4, w, tiles, pos, *, tm, tn, out_dtype, streams_per_step=1, n_split=4):
    B, d, L, K = act4.shape
    sb = streams_per_step
    nI, nJ, nR = L // tm, len(tiles), d // sb
    assert L % tm == 0 and pos.shape == (d, L) and d % sb == 0 and (sb == 1 or nI == 1) and tm % n_split == 0
    tr = sb * tm
    kinds = sorted({(rope, scale) for _, rope, scale in tiles})
    cos, sin = _rope_tables(pos.reshape(-1))
    c_rows, s_rows, base = [], [], {}
    n_blocks = 0
    for rope, scale in kinds:
        base[(rope, scale)] = n_blocks
        if rope:
            c_rows += [cos * scale]
            s_rows += [sin * scale]
            n_blocks += nR * nI
        else:
            c_rows += [jnp.full((tr, HEAD_DIM), scale, F32)]
            s_rows += [jnp.zeros((tr, HEAD_DIM), F32)]
            n_blocks += 1
    c_tab, s_tab = jnp.concatenate(c_rows, axis=0), jnp.concatenate(s_rows, axis=0)
    wblk = jnp.asarray([t[0] for t in tiles], jnp.int32)
    tbase = jnp.asarray([base[(t[1], t[2])] for t in tiles], jnp.int32)
    tstep = jnp.asarray([int(t[1]) for t in tiles], jnp.int32)

    def split(m):
        return m // (nR * nI), (m // nI) % nR, m % nI

    def a_idx(m, j, wb, tb, ts):
        b, rg, i = split(m)
        return (b, rg, i, 0)

    def t_idx(m, j, wb, tb, ts):
        b, rg, i = split(m)
        return (tb[j] + ts[j] * (rg * nI + i), 0)

    def o_idx(m, j, wb, tb, ts):
        b, rg, i = split(m)
        return (b, rg, i, j)

    grid_spec = pltpu.PrefetchScalarGridSpec(
        num_scalar_prefetch=3,
        grid=(B * nR * nI, nJ),
        in_specs=[
            pl.BlockSpec((None, sb, tm, K), a_idx),
            pl.BlockSpec((K, tn), lambda m, j, wb, tb, ts: (0, wb[j])),
            pl.BlockSpec((tr, HEAD_DIM), t_idx),
            pl.BlockSpec((tr, HEAD_DIM), t_idx),
        ],
        out_specs=pl.BlockSpec((None, sb, tm, tn), o_idx),
    )
    return pl.pallas_call(
        functools.partial(_proj_kernel, n_split=n_split),
        out_shape=jax.ShapeDtypeStruct((B, d, L, nJ * tn), out_dtype),
        grid_spec=grid_spec,
        compiler_params=_params("parallel", "arbitrary"),
        name=f"proj_d{d}_n{nJ * tn}",
    )(wblk, tbase, tstep, act4, w, c_tab, s_tab)


def _outproj_kernel(*refs, n_split):
    o_ref = refs[-1]
    ws = [w_ref[...].astype(BF16) for w_ref in refs[1:-1:2]]
    chunk = o_ref.shape[0] // n_split
    for h in range(n_split):
        rows = slice(h * chunk, (h + 1) * chunk)
        acc = None
        for a_ref, w in zip(refs[0:-1:2], ws):
            part = jnp.dot(a_ref[rows, :], w, preferred_element_type=F32)
            acc = part if acc is None else acc + part
        o_ref[rows, :] = acc.astype(o_ref.dtype)


def _out_project(acts, w, *, tm, tn, n_split=4):
    T, K = acts[0].shape
    N = w.shape[1]
    assert all(a.shape == (T, K) for a in acts) and w.shape[0] == len(acts) * K
    in_specs, args = [], []
    for n, a in enumerate(acts):
        in_specs += [pl.BlockSpec((tm, K), lambda i, j: (i, 0)), pl.BlockSpec((K, tn), lambda i, j, n=n: (n, j))]
        args += [a, w]
    return pl.pallas_call(
        functools.partial(_outproj_kernel, n_split=n_split),
        out_shape=jax.ShapeDtypeStruct((T, N), BF16),
        grid=(T // tm, N // tn),
        in_specs=in_specs,
        out_specs=pl.BlockSpec((tm, tn), lambda i, j: (i, j)),
        compiler_params=_params("parallel", "arbitrary"),
        name=f"outproj_{len(acts)}",
    )(*args)


def _ln_kernel(x_ref, y_ref, g_ref, b_ref, o_ref, *stream_refs, dilations):
    z = DEEPNORM_ALPHA * x_ref[...] + y_ref[...].astype(F32)
    mu = jnp.mean(z, axis=-1, keepdims=True)
    zc = z - mu
    var = jnp.mean(zc * zc, axis=-1, keepdims=True)
    out = zc * lax.rsqrt(var + LN_EPS) * g_ref[...] + b_ref[...]
    o_ref[...] = out
    tr = out.shape[0]
    ob = out.astype(BF16)
    dst = lax.broadcasted_iota(jnp.int32, (tr, tr), 0)
    src = lax.broadcasted_iota(jnp.int32, (tr, tr), 1)
    for d, s_ref in zip(dilations, stream_refs):
        if d == 1:
            s_ref[0] = ob
            continue
        per = tr // d
        perm = jnp.where(src == (dst & (per - 1)) * d + (dst >> (per.bit_length() - 1)), 1.0, 0.0).astype(BF16)
        sm = jnp.dot(perm, ob, preferred_element_type=F32).astype(BF16)
        for r in range(d):
            s_ref[r] = sm[r * per:(r + 1) * per]


def _deepnorm_ln(x3, y2, g, b, *, tr, dilations=()):
    B, S, D = x3.shape
    nI = S // tr
    assert S % tr == 0 and all(tr % (d * 16) == 0 and d & (d - 1) == 0 for d in dilations) and tr & (tr - 1) == 0
    row3 = pl.BlockSpec((None, tr, D), lambda b, i: (b, i, 0))
    vec = pl.BlockSpec((1, D), lambda b, i: (0, 0))
    out_shape = [jax.ShapeDtypeStruct((B, S, D), F32)]
    out_specs = [row3]
    for d in dilations:
        out_shape.append(jax.ShapeDtypeStruct((B, d, S // d, D), BF16))
        out_specs.append(pl.BlockSpec((None, d, tr // d, D), lambda b, i: (b, 0, i, 0)))
    return pl.pallas_call(
        functools.partial(_ln_kernel, dilations=tuple(dilations)),
        out_shape=tuple(out_shape),
        grid=(B, nI),
        in_specs=[row3, pl.BlockSpec((tr, D), lambda b, i: (b * nI + i, 0)), vec, vec],
        out_specs=tuple(out_specs),
        compiler_params=_params("parallel", "parallel"),
        name="deepnorm_ln",
    )(x3, y2, g.reshape(1, D), b.reshape(1, D))


def _compress_kernel(zk_ref, zv_ref, pe_k_ref, pe_v_ref, w1ka_ref, w1kb_ref, w1va_ref, w1vb_ref,
                     w2k_ref, w2vt_ref, cos_ref, sin_ref, kc_ref, vct_ref, uk_ref, vk_ref, uv_ref, vv_ref):
    p = pl.program_id(1)
    n = uk_ref.shape[0]

    @pl.when(p == 0)
    def _():
        for ref in (uk_ref, vk_ref, uv_ref, vv_ref):
            ref[...] = jnp.zeros_like(ref)

    def accumulate(z_ref, pe_ref, wa_ref, wb_ref, u_ref, v_ref):
        z = z_ref[...].astype(F32)
        za = (z + pe_ref[pl.ds(p, 1), :]).astype(BF16)
        zb = (z + pe_ref[pl.ds(p + NSA_CMP_STRIDE, 1), :]).astype(BF16)
        u_ref[...] += jnp.dot(za, wa_ref[...], preferred_element_type=F32)
        v_ref[...] += jnp.dot(zb, wb_ref[...], preferred_element_type=F32)

    accumulate(zk_ref, pe_k_ref, w1ka_ref, w1kb_ref, uk_ref, vk_ref)
    accumulate(zv_ref, pe_v_ref, w1va_ref, w1vb_ref, uv_ref, vv_ref)

    @pl.when(p == NSA_CMP_STRIDE - 1)
    def _():
        hid_k = _silu(uk_ref[...] + pltpu.roll(vk_ref[...], n - 1, 0)).astype(BF16)
        hid_v = _silu(uv_ref[...] + pltpu.roll(vv_ref[...], n - 1, 0)).astype(BF16)
        kc = jnp.dot(hid_k, w2k_ref[...], preferred_element_type=F32)
        kc = _rope(kc, cos_ref[...], sin_ref[...])
        row = lax.broadcasted_iota(jnp.int32, kc.shape, 0)
        kc_ref[...] = jnp.where(row < n - 1, kc, 0.0).astype(kc_ref.dtype)
        vct = lax.dot_general(w2vt_ref[...], hid_v, NT_DIMS, preferred_element_type=F32)
        col = lax.broadcasted_iota(jnp.int32, vct.shape, 1)
        vct_ref[...] = jnp.where(col < n - 1, vct, 0.0).astype(vct_ref.dtype)


def _nsa_compress(h2, B, S, pe_k, w1_k, w2_k, pe_v, w1_v, w2_v, kc_block, vc_block):
    G = NSA_KV_GROUPS
    n = S // NSA_CMP_STRIDE
    c0, c1 = kc_block * HEAD_DIM, (vc_block + G) * HEAD_DIM
    nb = (c1 - c0) // HEAD_DIM
    h3 = h2[:, c0:c1].reshape(B, n, NSA_CMP_STRIDE * (c1 - c0))
    kc_block, vc_block = 0, vc_block - kc_block
    cos, sin = _rope_tables(jnp.arange(n) * NSA_CMP_STRIDE + NSA_CMP_LEN - 1)

    def z_spec(block0):
        return pl.BlockSpec((None, n, HEAD_DIM), lambda bg, p: (bg // G, 0, p * nb + block0 + bg % G))

    full = lambda shape: pl.BlockSpec(shape, lambda bg, p: (0,) * len(shape))
    w1a = pl.BlockSpec((HEAD_DIM, HEAD_DIM), lambda bg, p: (p, 0))
    w1b = pl.BlockSpec((HEAD_DIM, HEAD_DIM), lambda bg, p: (p + NSA_CMP_STRIDE, 0))
    acc = pltpu.VMEM((n, HEAD_DIM), F32)
    return pl.pallas_call(
        _compress_kernel,
        out_shape=(jax.ShapeDtypeStruct((B * G, n, HEAD_DIM), BF16), jax.ShapeDtypeStruct((B * G, HEAD_DIM, n), BF16)),
        grid=(B * G, NSA_CMP_STRIDE),
        in_specs=[z_spec(kc_block), z_spec(vc_block), full((NSA_CMP_LEN, HEAD_DIM)), full((NSA_CMP_LEN, HEAD_DIM)),
                  w1a, w1b, w1a, w1b, full((HEAD_DIM, HEAD_DIM)), full((HEAD_DIM, HEAD_DIM)),
                  full((n, HEAD_DIM)), full((n, HEAD_DIM))],
        out_specs=(pl.BlockSpec((None, n, HEAD_DIM), lambda bg, p: (bg, 0, 0)),
                   pl.BlockSpec((None, HEAD_DIM, n), lambda bg, p: (bg, 0, 0))),
        scratch_shapes=[acc, acc, acc, acc],
        compiler_params=_params("parallel", "arbitrary"),
        name="nsa_compress",
    )(h3, h3, pe_k, pe_v, w1_k.astype(BF16), w1_k.astype(BF16), w1_v.astype(BF16), w1_v.astype(BF16),
      w2_k.astype(BF16), w2_v.T.astype(BF16), cos, sin)


def _split3(x):
    hi = x.astype(BF16)
    r1 = x - hi.astype(F32)
    mid = r1.astype(BF16)
    lo = (r1 - mid.astype(F32)).astype(BF16)
    return hi, mid, lo


def _vt_kernel(blk_ref, v_ref, o_ref):
    extra = VT_ROWS - HEAD_DIM
    ones_row = jnp.where(lax.broadcasted_iota(jnp.int32, (extra, HEAD_DIM), 0) == 0, 1.0, 0.0).astype(o_ref.dtype)
    for c in range(o_ref.shape[0]):
        o_ref[c, 0:HEAD_DIM, :] = v_ref[c * HEAD_DIM:(c + 1) * HEAD_DIM, :].astype(F32).T.astype(o_ref.dtype)
        o_ref[c, HEAD_DIM:VT_ROWS, :] = ones_row


def _transpose_values(h2, B, S, blocks, *, rows):
    ns = S // rows
    per = rows // HEAD_DIM
    nv = len(blocks)
    grid_spec = pltpu.PrefetchScalarGridSpec(
        num_scalar_prefetch=1,
        grid=(B, nv, ns),
        in_specs=[pl.BlockSpec((rows, HEAD_DIM), lambda b, i, s, blk: (b * ns + s, blk[i]))],
        out_specs=pl.BlockSpec((None, None, per, VT_ROWS, HEAD_DIM), lambda b, i, s, blk: (b, i, s, 0, 0)),
    )
    return pl.pallas_call(
        _vt_kernel,
        out_shape=jax.ShapeDtypeStruct((B, nv, S // HEAD_DIM, VT_ROWS, HEAD_DIM), BF16),
        grid_spec=grid_spec,
        compiler_params=_params("parallel", "parallel", "arbitrary"),
        name="transpose_values",
    )(jnp.asarray(blocks, jnp.int32), h2)


def _nsa_kernel(q_ref, kc_ref, vct_ref, ks_ref, vst_ref, kw_ref, vwt_ref, gl_ref, gate_ref, ovt_ref, blk_ref,
                o_ref, sig_ref, *, tq, tk, n_sel, n_sub):
    g = pl.program_id(1)
    hg = NSA_HEADS_PER_GROUP
    n_cmp = kc_ref.shape[0]
    nbp = NSA_BLK_PAD
    head = lambda h: slice(h * HEAD_DIM, (h + 1) * HEAD_DIM)
    lanes = lambda h: slice(h * tq, (h + 1) * tq)
    per_head = lambda a: jnp.concatenate([a] * hg, axis=1)
    chunk = HEAD_DIM
    n_pairs = hg // 2

    def softmax_rows(s, m_floor):
        m = jnp.maximum(jnp.max(s, axis=0, keepdims=True), m_floor)
        e = jnp.exp2(s - m)
        den = jnp.sum(e, axis=0, keepdims=True)
        return e, jnp.where(den > 0.0, 1.0 / den, 0.0)

    def prologue(u):
        qi = pl.program_id(2) * n_sub + u
        t0 = qi * tq
        rows = slice(u * tq, (u + 1) * tq)
        q4 = q_ref[rows, :]
        qs = jnp.concatenate([q4[:, head(h)] for h in range(hg)], axis=0)

        t_lane = t0 + lax.broadcasted_iota(jnp.int32, (n_cmp, tq), 1)
        cmp_end = lax.broadcasted_iota(jnp.int32, (n_cmp, tq), 0) * NSA_CMP_STRIDE + (NSA_CMP_LEN - 1)
        cbias = per_head(jnp.where(cmp_end <= t_lane, 0.0, NEG_INF))
        s = lax.dot_general(kc_ref[...], qs, NT_DIMS, preferred_element_type=F32) + cbias
        e, inv = softmax_rows(s, M_INIT)
        p = e * inv
        psum = sum(p[:, lanes(h)] for h in range(hg))
        o_ct = jnp.dot(vct_ref[...], p.astype(BF16), preferred_element_type=F32)

        ovt = ovt_ref[...]
        imp = sum(jnp.dot(ovt, part, preferred_element_type=F32) for part in _split3(psum))
        j_idx = lax.broadcasted_iota(jnp.int32, (nbp, tq), 0)
        cur = (t0 + lax.broadcasted_iota(jnp.int32, (nbp, tq), 1)) >> 6
        forced = (j_idx == 0) | (j_idx == cur) | (j_idx == cur - 1)
        imp = jnp.where(forced, imp + NSA_FORCE_BONUS, imp)
        imp = jnp.where(j_idx <= cur, imp, NEG_INF)

        band = NSA_WINDOW + tq
        wc = jnp.maximum(qi - NSA_WINDOW // tq, 0)
        w0 = pl.multiple_of(wc * tq, tq)
        kb = kw_ref[pl.ds(w0, band), :]
        vbt = jnp.concatenate([vwt_ref[wc + c, 0:HEAD_DIM, :] for c in range(band // chunk)], axis=1)
        rel = ((t0 + lax.broadcasted_iota(jnp.int32, (band, tq), 1))
               - (w0 + lax.broadcasted_iota(jnp.int32, (band, tq), 0)))
        wbias = per_head(jnp.where((rel >= 0) & (rel < NSA_WINDOW), 0.0, NEG_INF))
        s = lax.dot_general(kb, qs, NT_DIMS, preferred_element_type=F32) + wbias
        e, inv = softmax_rows(s, M_INIT)
        o_wt = jnp.dot(vbt, e.astype(BF16), preferred_element_type=F32) * inv

        j_f = j_idx.astype(F32)
        picked = jnp.zeros((nbp, tq), F32)
        for _ in range(n_sel):
            top = jnp.max(imp, axis=0, keepdims=True)
            first = jnp.min(jnp.where(imp == top, j_f, float(nbp)), axis=0, keepdims=True)
            hit = j_f == first
            picked = jnp.where(hit, 1.0, picked)
            imp = jnp.where(hit, TAKEN, imp)
        sel_t = jnp.where(j_idx <= cur, picked, 0.0)

        sel_bias = jnp.where(sel_t.T > 0.5, 0.0, NEG_INF).astype(BF16)
        sel_bias2 = jnp.concatenate([sel_bias, sel_bias], axis=0)
        q_aug = [jnp.concatenate([qs[pr * 2 * tq:(pr + 1) * 2 * tq], sel_bias2], axis=1) for pr in range(n_pairs)]
        return t0, q_aug, o_ct, o_wt

    def sweep(t0, q_aug):
        key_i = lax.broadcasted_iota(jnp.int32, (SEL_SUB, tq), 0)
        t_i = t0 + lax.broadcasted_iota(jnp.int32, (SEL_SUB, tq), 1)

        def sel_scores(kt, causal):
            k0 = pl.multiple_of(kt * tk, tk)
            scores = []
            for sub in range(tk // SEL_SUB):
                ks0 = pl.multiple_of(k0 + sub * SEL_SUB, SEL_SUB)
                k_aug = jnp.concatenate([ks_ref[pl.ds(ks0, SEL_SUB), :], blk_ref[pl.ds(ks0, SEL_SUB), :]], axis=1)
                if causal:
                    cbias = jnp.where(key_i + ks0 <= t_i, 0.0, NEG_INF)
                    cbias2 = jnp.concatenate([cbias, cbias], axis=1)
                for pr in range(n_pairs):
                    s = lax.dot_general(k_aug, q_aug[pr], NT_DIMS, preferred_element_type=F32)
                    scores.append(s + cbias2 if causal else s)
            return tuple(scores)

        def sel_tiles(kt0, count, carry, diagonal_last=False):
            m, acc = list(carry[0]), list(carry[1])
            scores = [sel_scores(kt0 + u, diagonal_last and u == count - 1) for u in range(count)]
            for u in range(count):
                for sub in range(tk // SEL_SUB):
                    c0 = (kt0 + u) * (tk // chunk) + sub * (SEL_SUB // chunk)
                    vt = jnp.concatenate([vst_ref[c0 + c] for c in range(SEL_SUB // chunk)], axis=1)
                    for pr in range(n_pairs):
                        s = scores[u][sub * n_pairs + pr]
                        m_new = jnp.maximum(m[pr], jnp.max(s, axis=0, keepdims=True))
                        p = jnp.exp2(s - m_new).astype(BF16)
                        acc[pr] = jnp.exp2(m[pr] - m_new) * acc[pr] + jnp.dot(vt, p, preferred_element_type=F32)
                        m[pr] = m_new
            return tuple(m), tuple(acc)

        n_tiles = (t0 + tq + tk - 1) // tk
        n_groups = (n_tiles - 1) // SEL_UNROLL
        n_last = n_tiles - n_groups * SEL_UNROLL
        carry = (tuple(jnp.full((1, 2 * tq), M_INIT, F32) for _ in range(n_pairs)),
                 tuple(jnp.zeros((VT_ROWS, 2 * tq), F32) for _ in range(n_pairs)))
        carry = lax.fori_loop(0, n_groups, lambda it, c: sel_tiles(it * SEL_UNROLL, SEL_UNROLL, c), carry)
        last = [functools.partial(sel_tiles, n_groups * SEL_UNROLL, count, diagonal_last=True)
                for count in range(1, SEL_UNROLL + 1)]
        _, acc_pairs = lax.switch(n_last - 1, last, carry)
        acc = jnp.concatenate(acc_pairs, axis=1)
        l_s = acc[HEAD_DIM:HEAD_DIM + 1, :]
        return acc[:HEAD_DIM, :] * jnp.where(l_s > 0.0, 1.0 / l_s, 0.0)

    def epilogue(u, o_ct, o_st, o_wt):
        rows = slice(u * tq, (u + 1) * tq)
        sig_ref[u] = (1.0 / (1.0 + jnp.exp(-gl_ref[rows, :]))).T
        gate = gate_ref[rows, :].astype(F32)
        for h in range(hg):
            base = (g * hg + h) * 3
            gc, gs, gw = (sig_ref[u, pl.ds(base + c, 1), :] for c in range(3))
            o_t = gc * o_ct[:, lanes(h)] + gs * o_st[:, lanes(h)] + gw * o_wt[:, lanes(h)]
            o_ref[rows, head(h)] = (o_t.T * _silu(gate[:, head(h)])).astype(o_ref.dtype)

    pro = [prologue(u) for u in range(n_sub)]
    swept = [sweep(t0, q_aug) for t0, q_aug, _, _ in pro]
    for u in range(n_sub):
        epilogue(u, pro[u][2], swept[u], pro[u][3])


def _nsa_attention(h2, small, kc, vct, vt, B, S, *, q_tile0, kv_block0, gate_tile0, tq, tk, n_sub):
    T = h2.shape[0]
    G, hg = NSA_KV_GROUPS, NSA_HEADS_PER_GROUP
    n_cmp = S // NSA_CMP_STRIDE
    n_blk = S // NSA_SEL_LEN
    nQ = S // (tq * n_sub)
    gw = hg * HEAD_DIM
    assert n_blk <= NSA_BLK_PAD and S % tk == 0 and S >= NSA_WINDOW + tq and tq == HEAD_DIM and S % (tq * n_sub) == 0
    assert tk % SEL_SUB == 0 and (tk // NSA_SEL_LEN) % 8 == 0 and hg % 2 == 0

    nn = np.arange(n_cmp)[None, :]
    jj = np.arange(NSA_BLK_PAD)[:, None]
    ovt = ((nn * NSA_CMP_STRIDE + NSA_CMP_LEN - 1 >= jj * NSA_SEL_LEN) & (nn * NSA_CMP_STRIDE < (jj + 1) * NSA_SEL_LEN)
           & (nn < n_cmp - 1) & (jj < n_blk))

    block_onehot = (np.arange(S)[:, None] // NSA_SEL_LEN) == np.arange(NSA_BLK_PAD)[None, :]

    def k_spec(c):
        return pl.BlockSpec((S, HEAD_DIM), lambda b, g, i: (b, kv_block0 + 4 * c + g))

    def vt_spec(which):
        return pl.BlockSpec((None, None, S // HEAD_DIM, VT_ROWS, HEAD_DIM), lambda b, g, i: (b, which * G + g, 0, 0, 0))

    rows = lambda width, col: pl.BlockSpec((tq * n_sub, width), lambda b, g, i: (b * nQ + i, col(g)))
    per_bg = lambda shape: pl.BlockSpec((None,) + shape, lambda b, g, i: (b * G + g, 0, 0))
    const = lambda shape: pl.BlockSpec(shape, lambda b, g, i: (0,) * len(shape))
    return pl.pallas_call(
        functools.partial(_nsa_kernel, tq=tq, tk=tk, n_sel=min(NSA_N_SEL, n_blk), n_sub=n_sub),
        out_shape=jax.ShapeDtypeStruct((T, G * gw), BF16),
        grid=(B, G, nQ),
        in_specs=[rows(gw, lambda g: q_tile0 + g), per_bg((n_cmp, HEAD_DIM)), per_bg((HEAD_DIM, n_cmp)),
                  k_spec(2), vt_spec(0), k_spec(4), vt_spec(1),
                  rows(HEAD_DIM, lambda g: 0), rows(gw, lambda g: gate_tile0 + g),
                  const((NSA_BLK_PAD, n_cmp)), const((S, NSA_BLK_PAD))],
        out_specs=rows(gw, lambda g: g),
        scratch_shapes=[pltpu.VMEM((n_sub, HEAD_DIM, tq), F32)],
        compiler_params=_params("parallel", "parallel", "arbitrary"),
        name="nsa_attention",
    )(h2, kc, vct, h2, vt, h2, vt, small, h2, jnp.asarray(ovt, BF16), jnp.asarray(block_onehot, BF16))


def _gla_kernel(q_ref, k_ref, v_ref, a_ref, gate_ref, wa_ref, ba_ref, ng_ref, o_ref, state_ref, intra_ref,
                *, n_chunks, n_heads):
    C = GLA_CHUNK

    @pl.when(pl.program_id(2) == 0)
    def _():
        state_ref[...] = jnp.zeros_like(state_ref)

    r_i = lax.broadcasted_iota(jnp.int32, (C, C), 0)
    c_i = lax.broadcasted_iota(jnp.int32, (C, C), 1)
    causal = r_i >= c_i
    tri = jnp.where(causal, 1.0, 0.0).astype(BF16)
    ng = ng_ref[...]
    heads = range(n_heads)
    kcols = lambda h: slice(h * GLA_DK, (h + 1) * GLA_DK)
    vcols = lambda h: slice(h * GLA_DV, (h + 1) * GLA_DV)

    qe_all, kd_all, decay_all = [], [], []
    for c in range(n_chunks):
        rows = slice(c * C, (c + 1) * C)
        a = a_ref[rows, :]
        pre = [jnp.dot(a, wa_ref[h], preferred_element_type=F32) + ba_ref[h] for h in heads]
        log_a = [(jnp.minimum(p, 0.0) - jnp.log(1.0 + jnp.exp(-jnp.abs(p)))) * (1.0 / GLA_TAU) for p in pre]
        bcum = [sum(jnp.dot(tri, part, preferred_element_type=F32) for part in _split3(la)) for la in log_a]
        b_last = [bc[C - 1:C, :] for bc in bcum]
        q = [q_ref[rows, kcols(h)].astype(F32) for h in heads]
        k = [k_ref[rows, kcols(h)].astype(F32) for h in heads]
        qe = [(q[h] * jnp.exp(bcum[h])).astype(BF16) for h in heads]
        ke = [(k[h] * jnp.exp(-bcum[h])).astype(BF16) for h in heads]
        kd = [(k[h] * jnp.exp(b_last[h] - bcum[h])).astype(BF16) for h in heads]
        attn = [lax.dot_general(qe[h], ke[h], NT_DIMS, preferred_element_type=F32) for h in heads]
        attn = [jnp.where(causal, at, 0.0).astype(BF16) for at in attn]
        for h in heads:
            intra_ref[rows, vcols(h)] = jnp.dot(attn[h], v_ref[rows, vcols(h)], preferred_element_type=F32)
        qe_all.append(qe)
        kd_all.append(kd)
        decay_all.append([jnp.exp(bl) for bl in b_last])

    for c in range(n_chunks):
        rows = slice(c * C, (c + 1) * C)
        state_t = [state_ref[h] for h in heads]
        inter = [lax.dot_general(qe_all[c][h], state_t[h].astype(BF16), NT_DIMS, preferred_element_type=F32)
                 for h in heads]
        upd = [lax.dot_general(v_ref[rows, vcols(h)], kd_all[c][h], TN_DIMS, preferred_element_type=F32) for h in heads]
        for h in heads:
            state_ref[h] = state_t[h] * decay_all[c][h] + upd[h]
        for h in heads:
            o = intra_ref[rows, vcols(h)] + inter[h]
            o = o * lax.rsqrt(jnp.mean(o * o, axis=-1, keepdims=True) + NORM_EPS) * ng
            o_ref[rows, vcols(h)] = (o * _silu(gate_ref[rows, vcols(h)].astype(F32))).astype(o_ref.dtype)


def _gla(h2, small, w_a2, b_a, norm_g, B, S, *, q_col0, k_col0, v_col0, gate_col0, rows, heads_per_step):
    T = h2.shape[0]
    H, hp = GLA_HEADS, heads_per_step
    nR = S // rows
    a_lo = 3 * NSA_KV_GROUPS * NSA_HEADS_PER_GROUP
    assert H % hp == 0 and all(c % (hp * w) == 0 for c, w in ((q_col0, GLA_DK), (k_col0, GLA_DK), (v_col0, GLA_DV),
                                                              (gate_col0, GLA_DV)))
    wa = jnp.zeros((H, HEAD_DIM, GLA_DK), F32).at[:, a_lo:a_lo + GLA_RANK, :].set(
        w_a2.reshape(GLA_RANK, H, GLA_DK).transpose(1, 0, 2))
    blk = lambda width, col0: pl.BlockSpec((rows, hp * width), lambda b, g, i: (b * nR + i, col0 // (hp * width) + g))
    per_head = lambda shape: pl.BlockSpec((hp,) + shape, lambda b, g, i: (g, 0, 0))
    return pl.pallas_call(
        functools.partial(_gla_kernel, n_chunks=rows // GLA_CHUNK, n_heads=hp),
        out_shape=jax.ShapeDtypeStruct((T, H * GLA_DV), BF16),
        grid=(B, H // hp, nR),
        in_specs=[blk(GLA_DK, q_col0), blk(GLA_DK, k_col0), blk(GLA_DV, v_col0),
                  pl.BlockSpec((rows, HEAD_DIM), lambda b, g, i: (b * nR + i, 0)), blk(GLA_DV, gate_col0),
                  per_head((HEAD_DIM, GLA_DK)), per_head((1, GLA_DK)),
                  pl.BlockSpec((1, GLA_DV), lambda b, g, i: (0, 0))],
        out_specs=blk(GLA_DV, 0),
        scratch_shapes=[pltpu.VMEM((hp, GLA_DV, GLA_DK), F32), pltpu.VMEM((rows, hp * GLA_DV), F32)],
        compiler_params=_params("parallel", "parallel", "arbitrary"),
        name="gla",
    )(h2, h2, h2, small, h2, wa, b_a.reshape(H, 1, GLA_DK), norm_g.reshape(1, GLA_DV))


def _dilated_stream(i, q, kp, kc, vp, vc, prev, gate, o_scr, m_scr, d_scr):
    tq = o_scr.shape[0]
    head = lambda h: slice(h * HEAD_DIM, (h + 1) * HEAD_DIM)
    row = lax.broadcasted_iota(jnp.int32, (tq, 2 * tq), 0)
    col = lax.broadcasted_iota(jnp.int32, (tq, 2 * tq), 1)
    valid = (col >= row + (tq - DIL_BACK)) & (col <= row + tq) & ((col >= tq) | (i > 0))
    m_scr[...] = jnp.zeros(m_scr.shape, F32)
    d_scr[...] = jnp.ones(d_scr.shape, F32)
    scores = [lax.dot_general(q[:, head(h)], jnp.concatenate([kp[:, head(h)], kc[:, head(h)]], axis=0), NT_DIMS,
                              preferred_element_type=F32) for h in range(DIL_HEADS)]
    for h in range(DIL_HEADS):
        v = jnp.concatenate([vp[:, head(h)], vc[:, head(h)]], axis=0)
        s = jnp.where(valid, scores[h], NEG_INF)
        m = jnp.max(s, axis=1, keepdims=True)
        e = jnp.exp(s - m)
        o_scr[:, head(h)] = jnp.dot(e.astype(BF16), v, preferred_element_type=F32)
        m_scr[:, h:h + 1] = m
        d_scr[:, h:h + 1] = jnp.sum(e, axis=1, keepdims=True)
    den = d_scr[...]
    lse = m_scr[...] + jnp.log(den)
    w_cur = 1.0 / den
    if prev is not None:
        prev_o, lse_p = prev
        lse_n = jnp.maximum(lse_p, lse) + jnp.log(1.0 + jnp.exp(-jnp.abs(lse_p - lse)))
        w_prev = jnp.exp(lse_p - lse_n)
        w_cur = jnp.exp(lse - lse_n) * w_cur
        lse = lse_n
    outs = []
    for h in range(DIL_HEADS):
        o = o_scr[:, head(h)] * w_cur[:, h:h + 1]
        if prev is not None:
            o = o + prev_o[:, head(h)].astype(F32) * w_prev[:, h:h + 1]
        if gate is not None:
            o = o * _silu(gate[:, head(h)].astype(F32))
        outs.append(o)
    return outs, lse


def _dilated_kernel(*refs, has_prev, is_last):
    q_ref, kp_ref, kc_ref, vp_ref, vc_ref = refs[:5]
    rest = list(refs[5:])
    prev = (rest.pop(0), rest.pop(0)) if has_prev else None
    gate_ref = rest.pop(0) if is_last else None
    o_ref = rest.pop(0)
    lse_ref = None if is_last else rest.pop(0)
    o_scr, m_scr, d_scr = rest
    if prev is not None:
        prev = (prev[0], prev[1][...])
    outs, lse = _dilated_stream(pl.program_id(2), q_ref, kp_ref, kc_ref, vp_ref, vc_ref, prev, gate_ref,
                                o_scr, m_scr, d_scr)
    for h, o in enumerate(outs):
        o_ref[:, h * HEAD_DIM:(h + 1) * HEAD_DIM] = o.astype(o_ref.dtype)
    if lse_ref is not None:
        lse_ref[...] = lse


def _dilated_scratch(tq, W):
    return [pltpu.VMEM((tq, W), F32), pltpu.VMEM((tq, HEAD_DIM), F32), pltpu.VMEM((tq, HEAD_DIM), F32)]


def _dilated_group(hg, prev_o, prev_lse, gate, *, tq):
    B, d, L, W3 = hg.shape
    W = W3 // 3
    has_prev, is_last = prev_o is not None, gate is not None
    assert tq == DIL_BACK and L % tq == 0
    cur = lambda col: pl.BlockSpec((None, None, tq, W), lambda b, r, i: (b, r, i, col))
    prv = lambda col: pl.BlockSpec((None, None, tq, W), lambda b, r, i: (b, r, jnp.maximum(i - 1, 0), col))
    tok = lambda width: pl.BlockSpec((None, tq, width), lambda b, r, i: (b, i, r))
    in_specs = [cur(0), prv(1), cur(1), prv(2), cur(2)]
    args = [hg] * 5
    if has_prev:
        in_specs += [tok(W), tok(HEAD_DIM)]
        args += [prev_o, prev_lse]
    if is_last:
        in_specs += [tok(W)]
        args += [gate]
    out_shape = [jax.ShapeDtypeStruct((B, L, d * W), BF16)]
    out_specs = [tok(W)]
    if not is_last:
        out_shape += [jax.ShapeDtypeStruct((B, L, d * HEAD_DIM), F32)]
        out_specs += [tok(HEAD_DIM)]
    res = pl.pallas_call(
        functools.partial(_dilated_kernel, has_prev=has_prev, is_last=is_last),
        out_shape=tuple(out_shape),
        grid=(B, d, L // tq),
        in_specs=in_specs,
        out_specs=tuple(out_specs),
        scratch_shapes=_dilated_scratch(tq, W),
        compiler_params=_params("parallel", "parallel", "arbitrary"),
        name=f"dilated_d{d}",
    )(*args)
    return res[0] if is_last else (res[0], res[1])


def _regroup_kernel(*refs, d, ratio):
    q_ref, kp_ref, kc_ref, vp_ref, vc_ref = refs[:5]
    po_refs = refs[5:5 + ratio]
    pl_refs = refs[5 + ratio:5 + 2 * ratio]
    o_ref, lse_ref, o_scr, m_scr, d_scr, so_scr, sl_scr = refs[5 + 2 * ratio:]
    i = pl.program_id(1)
    r = pl.program_id(2)
    tq, W = o_scr.shape
    rows_c = tq // ratio
    n = d * tq
    shift = lambda v: v.bit_length() - 1
    dst = lax.broadcasted_iota(jnp.int32, (tq, tq), 0)
    src = lax.broadcasted_iota(jnp.int32, (tq, tq), 1)
    gather = jnp.where(src == (dst & (ratio - 1)) * rows_c + (dst >> shift(ratio)), 1.0, 0.0).astype(BF16)
    prev_o = jnp.dot(gather, jnp.concatenate([p[...] for p in po_refs], axis=0), preferred_element_type=F32)
    prev_l = sum(jnp.dot(gather, part, preferred_element_type=F32)
                 for part in _split3(jnp.concatenate([p[...] for p in pl_refs], axis=0)))
    outs, lse = _dilated_stream(i, q_ref, kp_ref, kc_ref, vp_ref, vc_ref, (prev_o, prev_l), None,
                                o_scr, m_scr, d_scr)
    so_scr[r] = jnp.concatenate(outs, axis=1).astype(BF16)
    sl_scr[r] = lse

    @pl.when(r == d - 1)
    def _():
        dst = lax.broadcasted_iota(jnp.int32, (n, n), 0)
        src = lax.broadcasted_iota(jnp.int32, (n, n), 1)
        scatter = jnp.where(src == (dst & (d - 1)) * tq + (dst >> shift(d)), 1.0, 0.0).astype(BF16)
        o_ref[...] = jnp.dot(scatter, so_scr[...].reshape(n, W), preferred_element_type=F32).astype(o_ref.dtype)
        lse_ref[...] = sum(jnp.dot(scatter, part, preferred_element_type=F32)
                           for part in _split3(sl_scr[...].reshape(n, HEAD_DIM)))


def _dilated_regroup(hg, prev_o, prev_lse, *, fine, tq):
    B, d, L, W3 = hg.shape
    W = W3 // 3
    S = L * d
    ratio = fine // d
    assert tq == DIL_BACK and L % tq == 0 and fine % d == 0 and tq % ratio == 0
    assert d & (d - 1) == 0 and ratio & (ratio - 1) == 0
    rows_c = tq // ratio
    cur = lambda col: pl.BlockSpec((None, None, tq, W), lambda b, i, r: (b, r, i, col))
    prv = lambda col: pl.BlockSpec((None, None, tq, W), lambda b, i, r: (b, r, jnp.maximum(i - 1, 0), col))
    piece = lambda width: [pl.BlockSpec((None, rows_c, width), lambda b, i, r, j=j: (b, i, j * d + r))
                           for j in range(ratio)]
    return pl.pallas_call(
        functools.partial(_regroup_kernel, d=d, ratio=ratio),
        out_shape=(jax.ShapeDtypeStruct((B, S, W), BF16), jax.ShapeDtypeStruct((B, S, HEAD_DIM), F32)),
        grid=(B, L // tq, d),
        in_specs=[cur(0), prv(1), cur(1), prv(2), cur(2)] + piece(W) + piece(HEAD_DIM),
        out_specs=(pl.BlockSpec((None, d * tq, W), lambda b, i, r: (b, i, 0)),
                   pl.BlockSpec((None, d * tq, HEAD_DIM), lambda b, i, r: (b, i, 0))),
        scratch_shapes=_dilated_scratch(tq, W) + [pltpu.VMEM((d, tq, W), BF16), pltpu.VMEM((d, tq, HEAD_DIM), F32)],
        compiler_params=_params("parallel", "arbitrary", "arbitrary"),
        name=f"dilated_regroup_d{d}",
    )(*([hg] * 5 + [prev_o] * ratio + [prev_lse] * ratio))


def _row_tile(n, target):
    t = min(n, target)
    assert n % t == 0
    return t


def _even_layer(xb4, w_in, pe_k, w1_k, w2_k, pe_v, w1_v, w2_v, gla_w_a2, gla_b_a, gla_norm_g, w_out):
    B, _, S, D = xb4.shape
    T = B * S
    nq = NSA_KV_GROUPS * NSA_HEADS_PER_GROUP * HEAD_DIM
    nkv = 6 * NSA_KV_GROUPS * HEAD_DIM
    ng = 3 * NSA_KV_GROUPS * NSA_HEADS_PER_GROUP
    gk = GLA_HEADS * GLA_DK
    gv = GLA_HEADS * GLA_DV
    o_g = nq + nkv
    o_a = o_g + ng + nq + 2 * gk + gv
    w16 = w_in.astype(BF16)
    w_main = jnp.concatenate([w16[:, :o_g], w16[:, o_g + ng:o_a], w16[:, o_a + GLA_RANK:]], axis=1)
    w_small = jnp.concatenate([w16[:, o_g:o_g + ng], w16[:, o_a:o_a + GLA_RANK],
                               jnp.zeros((D, HEAD_DIM - ng - GLA_RANK), BF16)], axis=1)
    tn = 512
    plain, rope = (False, 1.0), (True, 1.0)
    groups = [("nsa_q", [(True, Q_SCALE * LOG2E)] * (nq // tn)),
              ("nsa_kv", [plain, plain, rope, plain, rope, plain]),
              ("nsa_gate", [plain] * (nq // tn)), ("gla_q", [(False, GLA_Q_SCALE)] * (gk // tn)),
              ("gla_k", [plain] * (gk // tn)), ("gla_v", [plain] * (gv // tn)), ("gla_gate", [plain] * (gv // tn))]
    w_block, n_seen = {}, 0
    for name, kinds in groups:
        w_block[name] = n_seen
        n_seen += len(kinds)
    order = ["nsa_q", "gla_v", "gla_gate", "nsa_gate", "gla_q", "gla_k", "nsa_kv"]
    tiles, col = [], {}
    for name in order:
        col[name] = len(tiles) * tn
        tiles += [(w_block[name] + t,) + kind for t, kind in enumerate(dict(groups)[name])]
    pos = jnp.arange(S)[None]
    tm = _row_tile(S, 1024)
    h2 = _project(xb4, w_main, tiles, pos, tm=tm, tn=tn, out_dtype=BF16).reshape(T, -1)
    small = _project(xb4, w_small, [(0,) + plain], pos, tm=tm, tn=HEAD_DIM, out_dtype=F32).reshape(T, HEAD_DIM)

    kv_block0 = col["nsa_kv"] // HEAD_DIM
    kc, vct = _nsa_compress(h2, B, S, pe_k, w1_k, w2_k, pe_v, w1_v, w2_v, kv_block0, kv_block0 + NSA_KV_GROUPS)
    v_blocks = [kv_block0 + 4 * c + g for c in (3, 5) for g in range(NSA_KV_GROUPS)]
    vt = _transpose_values(h2, B, S, v_blocks, rows=_row_tile(S, 1024))
    o_nsa = _nsa_attention(h2, small, kc, vct, vt, B, S, q_tile0=col["nsa_q"] // tn, kv_block0=kv_block0,
                           gate_tile0=col["nsa_gate"] // tn, tq=128, tk=_row_tile(S, 512), n_sub=2)
    o_gla = _gla(h2, small, gla_w_a2, gla_b_a, gla_norm_g, B, S, q_col0=col["gla_q"], k_col0=col["gla_k"],
                 v_col0=col["gla_v"], gate_col0=col["gla_gate"], rows=_row_tile(S, 512), heads_per_step=GLA_HEADS)
    return _out_project([o_nsa, o_gla], w_out, tm=_row_tile(T, 1024), tn=512)


def _odd_layer(streams, w_in, w_out):
    B, _, S, D = streams[1].shape
    T = B * S
    n_g = len(DIL_GROUPS)
    W = DIL_HEADS * HEAD_DIM
    tn = 512
    blocks = lambda col0: [col0 // tn + t for t in range(W // tn)]
    gate = _project(streams[1], w_in, [(blk, False, 1.0) for blk in blocks(3 * n_g * W)], jnp.arange(S)[None],
                    tm=_row_tile(S, 1024), tn=tn, out_dtype=BF16).reshape(B, S, W)

    def group_qkv(gi, d):
        L = S // d
        tiles = ([(blk, True, Q_SCALE) for blk in blocks(gi * W)] + [(blk, True, 1.0) for blk in blocks((n_g + gi) * W)]
                 + [(blk, False, 1.0) for blk in blocks((2 * n_g + gi) * W)])
        tm = _row_tile(L, 1024)
        return _project(streams[d], w_in, tiles, jnp.arange(S).reshape(L, d).T, tm=tm, tn=tn, out_dtype=BF16,
                        streams_per_step=min(d, 1024 // tm) if tm == L else 1, n_split=max(1, tm // 256))

    order = sorted(range(n_g), key=lambda gi: -DIL_GROUPS[gi][1])
    assert len(order) == 3 and DIL_GROUPS[order[-1]][1] == 1
    for gi in order:
        assert DIL_GROUPS[gi][0] // DIL_GROUPS[gi][1] == DIL_BACK
    d_hi, d_mid = DIL_GROUPS[order[0]][1], DIL_GROUPS[order[1]][1]
    o, lse = _dilated_group(group_qkv(order[0], d_hi), None, None, None, tq=DIL_BACK)
    o, lse = _dilated_regroup(group_qkv(order[1], d_mid), o, lse, fine=d_hi, tq=DIL_BACK)
    o = _dilated_group(group_qkv(order[2], 1), o, lse, gate, tq=DIL_BACK)
    return _out_project([o.reshape(T, W)], w_out, tm=_row_tile(T, 1024), tn=512)


def kernel(x, l0_w_in, l0_nsa_pe_k, l0_nsa_w1_k, l0_nsa_w2_k, l0_nsa_pe_v, l0_nsa_w1_v, l0_nsa_w2_v, l0_gla_w_a2, l0_gla_b_a, l0_gla_norm_g, l0_w_out, l0_ln_g, l0_ln_b, l1_w_in, l1_w_out, l1_ln_g, l1_ln_b):
    B, S, D = x.shape
    tr = _row_tile(S, 256)
    dils = sorted({d for _, d in DIL_GROUPS} | {1})
    y0 = _even_layer(x.astype(BF16)[:, None], l0_w_in, l0_nsa_pe_k, l0_nsa_w1_k, l0_nsa_w2_k, l0_nsa_pe_v, l0_nsa_w1_v,
                     l0_nsa_w2_v, l0_gla_w_a2, l0_gla_b_a, l0_gla_norm_g, l0_w_out)
    x1, *x1_streams = _deepnorm_ln(x, y0, l0_ln_g, l0_ln_b, tr=tr, dilations=dils)
    y1 = _odd_layer(dict(zip(dils, x1_streams)), l1_w_in, l1_w_out)
    return _deepnorm_ln(x1, y1, l1_ln_g, l1_ln_b, tr=tr)[0]
```

```python
import functools

import numpy as np
import jax
import jax.numpy as jnp
from jax import lax
from jax.experimental import pallas as pl
from jax.experimental.pallas import tpu as pltpu

F32 = jnp.float32
BF16 = jnp.bfloat16

HEAD_DIM = 128
ROPE_THETA = 10000.0
LN_EPS = 1e-5
NORM_EPS = 1e-6
NEG_INF = -1e30
M_INIT = -1e29
TAKEN = -3e38
SEL_UNROLL = 4
SEL_SUB = 256
VT_ROWS = HEAD_DIM + 16

NSA_KV_GROUPS = 4
NSA_HEADS_PER_GROUP = 4
NSA_CMP_LEN = 32
NSA_CMP_STRIDE = 16
NSA_SEL_LEN = 64
NSA_N_SEL = 16
NSA_WINDOW = 512
NSA_FORCE_BONUS = 1e4
NSA_BLK_PAD = 128

GLA_HEADS = 4
GLA_DK = 256
GLA_DV = 512
GLA_RANK = 16
GLA_TAU = 16.0
GLA_CHUNK = 64

DIL_GROUPS = ((128, 1), (512, 4), (2048, 16))
DIL_HEADS = 16
DIL_BACK = 128

DEPTH = 2
DEEPNORM_ALPHA = (2.0 * DEPTH) ** 0.25

VMEM_LIMIT_BYTES = 56 * 1024 * 1024

Q_SCALE = HEAD_DIM ** -0.5
LOG2E = 1.4426950408889634
GLA_Q_SCALE = GLA_DK ** -0.5

NT_DIMS = (((1,), (1,)), ((), ()))
TN_DIMS = (((0,), (0,)), ((), ()))


def _params(*sem):
    return pltpu.CompilerParams(dimension_semantics=sem, vmem_limit_bytes=VMEM_LIMIT_BYTES)


def _silu(x):
    return x * (1.0 / (1.0 + jnp.exp(-x)))


def _rope_tables(pos):
    half = HEAD_DIM // 2
    inv_freq = ROPE_THETA ** (-jnp.arange(half, dtype=F32) / half)
    ang = pos.astype(F32)[:, None] * inv_freq[None, :]
    cos = jnp.cos(ang)
    sin = jnp.sin(ang)
    return jnp.concatenate([cos, cos], axis=-1), jnp.concatenate([-sin, sin], axis=-1)


def _rope(a, cos, sin):
    return a * cos + pltpu.roll(a, HEAD_DIM // 2, 1) * sin


def _proj_kernel(wblk_ref, tbase_ref, tstep_ref, a_ref, w_ref, c_ref, s_ref, o_ref, *, n_split):
    w = w_ref[...].astype(BF16)
    sb, tm, _ = a_ref.shape
    chunk = tm // n_split
    for st in range(sb):
        for h in range(n_split):
            rows = slice(h * chunk, (h + 1) * chunk)
            trows = slice(st * tm + h * chunk, st * tm + (h + 1) * chunk)
            acc = jnp.dot(a_ref[st, rows, :], w, preferred_element_type=F32)
            c = c_ref[trows, :]
            s = s_ref[trows, :]
            for k in range(acc.shape[1] // HEAD_DIM):
                sl = slice(k * HEAD_DIM, (k + 1) * HEAD_DIM)
                o_ref[st, rows, sl] = (acc[:, sl] * c + pltpu.roll(acc[:, sl], HEAD_DIM // 2, 1) * s).astype(o_ref.dtype)


def _project(act4, w, tiles, pos, *, tm, tn, out_dtype, streams_per_step=1, n_split=4):
    B, d, L, K = act4.shape
    sb = streams_per_step
    nI, nJ, nR = L // tm, len(tiles), d // sb
    assert L % tm == 0 and pos.shape == (d, L) and d % sb == 0 and (sb == 1 or nI == 1) and tm % n_split == 0
    tr = sb * tm
    kinds = sorted({(rope, scale) for _, rope, scale in tiles})
    cos, sin = _rope_tables(pos.reshape(-1))
    c_rows, s_rows, base = [], [], {}
    n_blocks = 0
    for rope, scale in kinds:
        base[(rope, scale)] = n_blocks
        if rope:
            c_rows += [cos * scale]
            s_rows += [sin * scale]
            n_blocks += nR * nI
        else:
            c_rows += [jnp.full((tr, HEAD_DIM), scale, F32)]
            s_rows += [jnp.zeros((tr, HEAD_DIM), F32)]
            n_blocks += 1
    c_tab, s_tab = jnp.concatenate(c_rows, axis=0), jnp.concatenate(s_rows, axis=0)
    wblk = jnp.asarray([t[0] for t in tiles], jnp.int32)
    tbase = jnp.asarray([base[(t[1], t[2])] for t in tiles], jnp.int32)
    tstep = jnp.asarray([int(t[1]) for t in tiles], jnp.int32)

    def split(m):
        return m // (nR * nI), (m // nI) % nR, m % nI

    def a_idx(m, j, wb, tb, ts):
        b, rg, i = split(m)
        return (b, rg, i, 0)

    def t_idx(m, j, wb, tb, ts):
        b, rg, i = split(m)
        return (tb[j] + ts[j] * (rg * nI + i), 0)

    def o_idx(m, j, wb, tb, ts):
        b, rg, i = split(m)
        return (b, rg, i, j)

    grid_spec = pltpu.PrefetchScalarGridSpec(
        num_scalar_prefetch=3,
        grid=(B * nR * nI, nJ),
        in_specs=[
            pl.BlockSpec((None, sb, tm, K), a_idx),
            pl.BlockSpec((K, tn), lambda m, j, wb, tb, ts: (0, wb[j])),
            pl.BlockSpec((tr, HEAD_DIM), t_idx),
            pl.BlockSpec((tr, HEAD_DIM), t_idx),
        ],
        out_specs=pl.BlockSpec((None, sb, tm, tn), o_idx),
    )
    return pl.pallas_call(
        functools.partial(_proj_kernel, n_split=n_split),
        out_shape=jax.ShapeDtypeStruct((B, d, L, nJ * tn), out_dtype),
        grid_spec=grid_spec,
        compiler_params=_params("parallel", "arbitrary"),
        name=f"proj_d{d}_n{nJ * tn}",
    )(wblk, tbase, tstep, act4, w, c_tab, s_tab)


def _outproj_kernel(*refs, n_split):
    o_ref = refs[-1]
    ws = [w_ref[...].astype(BF16) for w_ref in refs[1:-1:2]]
    chunk = o_ref.shape[0] // n_split
    for h in range(n_split):
        rows = slice(h * chunk, (h + 1) * chunk)
        acc = None
        for a_ref, w in zip(refs[0:-1:2], ws):
            part = jnp.dot(a_ref[rows, :], w, preferred_element_type=F32)
            acc = part if acc is None else acc + part
        o_ref[rows, :] = acc.astype(o_ref.dtype)


def _out_project(acts, w, *, tm, tn, n_split=4):
    T, K = acts[0].shape
    N = w.shape[1]
    assert all(a.shape == (T, K) for a in acts) and w.shape[0] == len(acts) * K
    in_specs, args = [], []
    for n, a in enumerate(acts):
        in_specs += [pl.BlockSpec((tm, K), lambda i, j: (i, 0)), pl.BlockSpec((K, tn), lambda i, j, n=n: (n, j))]
        args += [a, w]
    return pl.pallas_call(
        functools.partial(_outproj_kernel, n_split=n_split),
        out_shape=jax.ShapeDtypeStruct((T, N), BF16),
        grid=(T // tm, N // tn),
        in_specs=in_specs,
        out_specs=pl.BlockSpec((tm, tn), lambda i, j: (i, j)),
        compiler_params=_params("parallel", "arbitrary"),
        name=f"outproj_{len(acts)}",
    )(*args)


def _ln_kernel(x_ref, y_ref, g_ref, b_ref, o_ref, *stream_refs, dilations):
    z = DEEPNORM_ALPHA * x_ref[...] + y_ref[...].astype(F32)
    mu = jnp.mean(z, axis=-1, keepdims=True)
    zc = z - mu
    var = jnp.mean(zc * zc, axis=-1, keepdims=True)
    out = zc * lax.rsqrt(var + LN_EPS) * g_ref[...] + b_ref[...]
    o_ref[...] = out
    tr = out.shape[0]
    ob = out.astype(BF16)
    dst = lax.broadcasted_iota(jnp.int32, (tr, tr), 0)
    src = lax.broadcasted_iota(jnp.int32, (tr, tr), 1)
    for d, s_ref in zip(dilations, stream_refs):
        if d == 1:
            s_ref[0] = ob
            continue
        per = tr // d
        perm = jnp.where(src == (dst & (per - 1)) * d + (dst >> (per.bit_length() - 1)), 1.0, 0.0).astype(BF16)
        sm = jnp.dot(perm, ob, preferred_element_type=F32).astype(BF16)
        for r in range(d):
            s_ref[r] = sm[r * per:(r + 1) * per]


def _deepnorm_ln(x3, y2, g, b, *, tr, dilations=()):
    B, S, D = x3.shape
    nI = S // tr
    assert S % tr == 0 and all(tr % (d * 16) == 0 and d & (d - 1) == 0 for d in dilations) and tr & (tr - 1) == 0
    row3 = pl.BlockSpec((None, tr, D), lambda b, i: (b, i, 0))
    vec = pl.BlockSpec((1, D), lambda b, i: (0, 0))
    out_shape = [jax.ShapeDtypeStruct((B, S, D), F32)]
    out_specs = [row3]
    for d in dilations:
        out_shape.append(jax.ShapeDtypeStruct((B, d, S // d, D), BF16))
        out_specs.append(pl.BlockSpec((None, d, tr // d, D), lambda b, i: (b, 0, i, 0)))
    return pl.pallas_call(
        functools.partial(_ln_kernel, dilations=tuple(dilations)),
        out_shape=tuple(out_shape),
        grid=(B, nI),
        in_specs=[row3, pl.BlockSpec((tr, D), lambda b, i: (b * nI + i, 0)), vec, vec],
        out_specs=tuple(out_specs),
        compiler_params=_params("parallel", "parallel"),
        name="deepnorm_ln",
    )(x3, y2, g.reshape(1, D), b.reshape(1, D))


def _compress_kernel(zk_ref, zv_ref, pe_k_ref, pe_v_ref, w1ka_ref, w1kb_ref, w1va_ref, w1vb_ref,
                     w2k_ref, w2vt_ref, cos_ref, sin_ref, kc_ref, vct_ref, uk_ref, vk_ref, uv_ref, vv_ref):
    p = pl.program_id(1)
    n = uk_ref.shape[0]

    @pl.when(p == 0)
    def _():
        for ref in (uk_ref, vk_ref, uv_ref, vv_ref):
            ref[...] = jnp.zeros_like(ref)

    def accumulate(z_ref, pe_ref, wa_ref, wb_ref, u_ref, v_ref):
        z = z_ref[...].astype(F32)
        za = (z + pe_ref[pl.ds(p, 1), :]).astype(BF16)
        zb = (z + pe_ref[pl.ds(p + NSA_CMP_STRIDE, 1), :]).astype(BF16)
        u_ref[...] += jnp.dot(za, wa_ref[...], preferred_element_type=F32)
        v_ref[...] += jnp.dot(zb, wb_ref[...], preferred_element_type=F32)

    accumulate(zk_ref, pe_k_ref, w1ka_ref, w1kb_ref, uk_ref, vk_ref)
    accumulate(zv_ref, pe_v_ref, w1va_ref, w1vb_ref, uv_ref, vv_ref)

    @pl.when(p == NSA_CMP_STRIDE - 1)
    def _():
        hid_k = _silu(uk_ref[...] + pltpu.roll(vk_ref[...], n - 1, 0)).astype(BF16)
        hid_v = _silu(uv_ref[...] + pltpu.roll(vv_ref[...], n - 1, 0)).astype(BF16)
        kc = jnp.dot(hid_k, w2k_ref[...], preferred_element_type=F32)
        kc = _rope(kc, cos_ref[...], sin_ref[...])
        row = lax.broadcasted_iota(jnp.int32, kc.shape, 0)
        kc_ref[...] = jnp.where(row < n - 1, kc, 0.0).astype(kc_ref.dtype)
        vct = lax.dot_general(w2vt_ref[...], hid_v, NT_DIMS, preferred_element_type=F32)
        col = lax.broadcasted_iota(jnp.int32, vct.shape, 1)
        vct_ref[...] = jnp.where(col < n - 1, vct, 0.0).astype(vct_ref.dtype)


def _nsa_compress(h2, B, S, pe_k, w1_k, w2_k, pe_v, w1_v, w2_v, kc_block, vc_block):
    G = NSA_KV_GROUPS
    n = S // NSA_CMP_STRIDE
    c0, c1 = kc_block * HEAD_DIM, (vc_block + G) * HEAD_DIM
    nb = (c1 - c0) // HEAD_DIM
    h3 = h2[:, c0:c1].reshape(B, n, NSA_CMP_STRIDE * (c1 - c0))
    kc_block, vc_block = 0, vc_block - kc_block
    cos, sin = _rope_tables(jnp.arange(n) * NSA_CMP_STRIDE + NSA_CMP_LEN - 1)

    def z_spec(block0):
        return pl.BlockSpec((None, n, HEAD_DIM), lambda bg, p: (bg // G, 0, p * nb + block0 + bg % G))

    full = lambda shape: pl.BlockSpec(shape, lambda bg, p: (0,) * len(shape))
    w1a = pl.BlockSpec((HEAD_DIM, HEAD_DIM), lambda bg, p: (p, 0))
    w1b = pl.BlockSpec((HEAD_DIM, HEAD_DIM), lambda bg, p: (p + NSA_CMP_STRIDE, 0))
    acc = pltpu.VMEM((n, HEAD_DIM), F32)
    return pl.pallas_call(
        _compress_kernel,
        out_shape=(jax.ShapeDtypeStruct((B * G, n, HEAD_DIM), BF16), jax.ShapeDtypeStruct((B * G, HEAD_DIM, n), BF16)),
        grid=(B * G, NSA_CMP_STRIDE),
        in_specs=[z_spec(kc_block), z_spec(vc_block), full((NSA_CMP_LEN, HEAD_DIM)), full((NSA_CMP_LEN, HEAD_DIM)),
                  w1a, w1b, w1a, w1b, full((HEAD_DIM, HEAD_DIM)), full((HEAD_DIM, HEAD_DIM)),
                  full((n, HEAD_DIM)), full((n, HEAD_DIM))],
        out_specs=(pl.BlockSpec((None, n, HEAD_DIM), lambda bg, p: (bg, 0, 0)),
                   pl.BlockSpec((None, HEAD_DIM, n), lambda bg, p: (bg, 0, 0))),
        scratch_shapes=[acc, acc, acc, acc],
        compiler_params=_params("parallel", "arbitrary"),
        name="nsa_compress",
    )(h3, h3, pe_k, pe_v, w1_k.astype(BF16), w1_k.astype(BF16), w1_v.astype(BF16), w1_v.astype(BF16),
      w2_k.astype(BF16), w2_v.T.astype(BF16), cos, sin)


def _split3(x):
    hi = x.astype(BF16)
    r1 = x - hi.astype(F32)
    mid = r1.astype(BF16)
    lo = (r1 - mid.astype(F32)).astype(BF16)
    return hi, mid, lo


def _vt_kernel(blk_ref, v_ref, o_ref):
    extra = VT_ROWS - HEAD_DIM
    ones_row = jnp.where(lax.broadcasted_iota(jnp.int32, (extra, HEAD_DIM), 0) == 0, 1.0, 0.0).astype(o_ref.dtype)
    for c in range(o_ref.shape[0]):
        o_ref[c, 0:HEAD_DIM, :] = v_ref[c * HEAD_DIM:(c + 1) * HEAD_DIM, :].astype(F32).T.astype(o_ref.dtype)
        o_ref[c, HEAD_DIM:VT_ROWS, :] = ones_row


def _transpose_values(h2, B, S, blocks, *, rows):
    ns = S // rows
    per = rows // HEAD_DIM
    nv = len(blocks)
    grid_spec = pltpu.PrefetchScalarGridSpec(
        num_scalar_prefetch=1,
        grid=(B, nv, ns),
        in_specs=[pl.BlockSpec((rows, HEAD_DIM), lambda b, i, s, blk: (b * ns + s, blk[i]))],
        out_specs=pl.BlockSpec((None, None, per, VT_ROWS, HEAD_DIM), lambda b, i, s, blk: (b, i, s, 0, 0)),
    )
    return pl.pallas_call(
        _vt_kernel,
        out_shape=jax.ShapeDtypeStruct((B, nv, S // HEAD_DIM, VT_ROWS, HEAD_DIM), BF16),
        grid_spec=grid_spec,
        compiler_params=_params("parallel", "parallel", "arbitrary"),
        name="transpose_values",
    )(jnp.asarray(blocks, jnp.int32), h2)


def _nsa_kernel(q_ref, kc_ref, vct_ref, ks_ref, vst_ref, kw_ref, vwt_ref, gl_ref, gate_ref, ovt_ref, blk_ref,
                o_ref, sig_ref, *, tq, tk, n_sel, n_sub):
    g = pl.program_id(1)
    hg = NSA_HEADS_PER_GROUP
    n_cmp = kc_ref.shape[0]
    nbp = NSA_BLK_PAD
    head = lambda h: slice(h * HEAD_DIM, (h + 1) * HEAD_DIM)
    lanes = lambda h: slice(h * tq, (h + 1) * tq)
    per_head = lambda a: jnp.concatenate([a] * hg, axis=1)
    chunk = HEAD_DIM
    n_pairs = hg // 2

    def softmax_rows(s, m_floor):
        m = jnp.maximum(jnp.max(s, axis=0, keepdims=True), m_floor)
        e = jnp.exp2(s - m)
        den = jnp.sum(e, axis=0, keepdims=True)
        return e, jnp.where(den > 0.0, 1.0 / den, 0.0)

    def prologue(u):
        qi = pl.program_id(2) * n_sub + u
        t0 = qi * tq
        rows = slice(u * tq, (u + 1) * tq)
        q4 = q_ref[rows, :]
        qs = jnp.concatenate([q4[:, head(h)] for h in range(hg)], axis=0)

        t_lane = t0 + lax.broadcasted_iota(jnp.int32, (n_cmp, tq), 1)
        cmp_end = lax.broadcasted_iota(jnp.int32, (n_cmp, tq), 0) * NSA_CMP_STRIDE + (NSA_CMP_LEN - 1)
        cbias = per_head(jnp.where(cmp_end <= t_lane, 0.0, NEG_INF))
        s = lax.dot_general(kc_ref[...], qs, NT_DIMS, preferred_element_type=F32) + cbias
        e, inv = softmax_rows(s, M_INIT)
        p = e * inv
        psum = sum(p[:, lanes(h)] for h in range(hg))
        o_ct = jnp.dot(vct_ref[...], p.astype(BF16), preferred_element_type=F32)

        ovt = ovt_ref[...]
        imp = sum(jnp.dot(ovt, part, preferred_element_type=F32) for part in _split3(psum))
        j_idx = lax.broadcasted_iota(jnp.int32, (nbp, tq), 0)
        cur = (t0 + lax.broadcasted_iota(jnp.int32, (nbp, tq), 1)) >> 6
        forced = (j_idx == 0) | (j_idx == cur) | (j_idx == cur - 1)
        imp = jnp.where(forced, imp + NSA_FORCE_BONUS, imp)
        imp = jnp.where(j_idx <= cur, imp, NEG_INF)

        band = NSA_WINDOW + tq
        wc = jnp.maximum(qi - NSA_WINDOW // tq, 0)
        w0 = pl.multiple_of(wc * tq, tq)
        kb = kw_ref[pl.ds(w0, band), :]
        vbt = jnp.concatenate([vwt_ref[wc + c, 0:HEAD_DIM, :] for c in range(band // chunk)], axis=1)
        rel = ((t0 + lax.broadcasted_iota(jnp.int32, (band, tq), 1))
               - (w0 + lax.broadcasted_iota(jnp.int32, (band, tq), 0)))
        wbias = per_head(jnp.where((rel >= 0) & (rel < NSA_WINDOW), 0.0, NEG_INF))
        s = lax.dot_general(kb, qs, NT_DIMS, preferred_element_type=F32) + wbias
        e, inv = softmax_rows(s, M_INIT)
        o_wt = jnp.dot(vbt, e.astype(BF16), preferred_element_type=F32) * inv

        j_f = j_idx.astype(F32)
        picked = jnp.zeros((nbp, tq), F32)
        for _ in range(n_sel):
            top = jnp.max(imp, axis=0, keepdims=True)
            first = jnp.min(jnp.where(imp == top, j_f, float(nbp)), axis=0, keepdims=True)
            hit = j_f == first
            picked = jnp.where(hit, 1.0, picked)
            imp = jnp.where(hit, TAKEN, imp)
        sel_t = jnp.where(j_idx <= cur, picked, 0.0)

        sel_bias = jnp.where(sel_t.T > 0.5, 0.0, NEG_INF).astype(BF16)
        sel_bias2 = jnp.concatenate([sel_bias, sel_bias], axis=0)
        q_aug = [jnp.concatenate([qs[pr * 2 * tq:(pr + 1) * 2 * tq], sel_bias2], axis=1) for pr in range(n_pairs)]
        return t0, q_aug, o_ct, o_wt

    def sweep(t0, q_aug):
        key_i = lax.broadcasted_iota(jnp.int32, (SEL_SUB, tq), 0)
        t_i = t0 + lax.broadcasted_iota(jnp.int32, (SEL_SUB, tq), 1)

        def sel_scores(kt, causal):
            k0 = pl.multiple_of(kt * tk, tk)
            scores = []
            for sub in range(tk // SEL_SUB):
                ks0 = pl.multiple_of(k0 + sub * SEL_SUB, SEL_SUB)
                k_aug = jnp.concatenate([ks_ref[pl.ds(ks0, SEL_SUB), :], blk_ref[pl.ds(ks0, SEL_SUB), :]], axis=1)
                if causal:
                    cbias = jnp.where(key_i + ks0 <= t_i, 0.0, NEG_INF)
                    cbias2 = jnp.concatenate([cbias, cbias], axis=1)
                for pr in range(n_pairs):
                    s = lax.dot_general(k_aug, q_aug[pr], NT_DIMS, preferred_element_type=F32)
                    scores.append(s + cbias2 if causal else s)
            return tuple(scores)

        def sel_tiles(kt0, count, carry, diagonal_last=False):
            m, acc = list(carry[0]), list(carry[1])
            scores = [sel_scores(kt0 + u, diagonal_last and u == count - 1) for u in range(count)]
            for u in range(count):
                for sub in range(tk // SEL_SUB):
                    c0 = (kt0 + u) * (tk // chunk) + sub * (SEL_SUB // chunk)
                    vt = jnp.concatenate([vst_ref[c0 + c] for c in range(SEL_SUB // chunk)], axis=1)
                    for pr in range(n_pairs):
                        s = scores[u][sub * n_pairs + pr]
                        m_new = jnp.maximum(m[pr], jnp.max(s, axis=0, keepdims=True))
                        p = jnp.exp2(s - m_new).astype(BF16)
                        acc[pr] = jnp.exp2(m[pr] - m_new) * acc[pr] + jnp.dot(vt, p, preferred_element_type=F32)
                        m[pr] = m_new
            return tuple(m), tuple(acc)

        n_tiles = (t0 + tq + tk - 1) // tk
        n_groups = (n_tiles - 1) // SEL_UNROLL
        n_last = n_tiles - n_groups * SEL_UNROLL
        carry = (tuple(jnp.full((1, 2 * tq), M_INIT, F32) for _ in range(n_pairs)),
                 tuple(jnp.zeros((VT_ROWS, 2 * tq), F32) for _ in range(n_pairs)))
        carry = lax.fori_loop(0, n_groups, lambda it, c: sel_tiles(it * SEL_UNROLL, SEL_UNROLL, c), carry)
        last = [functools.partial(sel_tiles, n_groups * SEL_UNROLL, count, diagonal_last=True)
                for count in range(1, SEL_UNROLL + 1)]
        _, acc_pairs = lax.switch(n_last - 1, last, carry)
        acc = jnp.concatenate(acc_pairs, axis=1)
        l_s = acc[HEAD_DIM:HEAD_DIM + 1, :]
        return acc[:HEAD_DIM, :] * jnp.where(l_s > 0.0, 1.0 / l_s, 0.0)

    def epilogue(u, o_ct, o_st, o_wt):
        rows = slice(u * tq, (u + 1) * tq)
        sig_ref[u] = (1.0 / (1.0 + jnp.exp(-gl_ref[rows, :]))).T
        gate = gate_ref[rows, :].astype(F32)
        for h in range(hg):
            base = (g * hg + h) * 3
            gc, gs, gw = (sig_ref[u, pl.ds(base + c, 1), :] for c in range(3))
            o_t = gc * o_ct[:, lanes(h)] + gs * o_st[:, lanes(h)] + gw * o_wt[:, lanes(h)]
            o_ref[rows, head(h)] = (o_t.T * _silu(gate[:, head(h)])).astype(o_ref.dtype)

    pro = [prologue(u) for u in range(n_sub)]
    swept = [sweep(t0, q_aug) for t0, q_aug, _, _ in pro]
    for u in range(n_sub):
        epilogue(u, pro[u][2], swept[u], pro[u][3])


def _nsa_attention(h2, small, kc, vct, vt, B, S, *, q_tile0, kv_block0, gate_tile0, tq, tk, n_sub):
    T = h2.shape[0]
    G, hg = NSA_KV_GROUPS, NSA_HEADS_PER_GROUP
    n_cmp = S // NSA_CMP_STRIDE
    n_blk = S // NSA_SEL_LEN
    nQ = S // (tq * n_sub)
    gw = hg * HEAD_DIM
    assert n_blk <= NSA_BLK_PAD and S % tk == 0 and S >= NSA_WINDOW + tq and tq == HEAD_DIM and S % (tq * n_sub) == 0
    assert tk % SEL_SUB == 0 and (tk // NSA_SEL_LEN) % 8 == 0 and hg % 2 == 0

    nn = np.arange(n_cmp)[None, :]
    jj = np.arange(NSA_BLK_PAD)[:, None]
    ovt = ((nn * NSA_CMP_STRIDE + NSA_CMP_LEN - 1 >= jj * NSA_SEL_LEN) & (nn * NSA_CMP_STRIDE < (jj + 1) * NSA_SEL_LEN)
           & (nn < n_cmp - 1) & (jj < n_blk))

    block_onehot = (np.arange(S)[:, None] // NSA_SEL_LEN) == np.arange(NSA_BLK_PAD)[None, :]

    def k_spec(c):
        return pl.BlockSpec((S, HEAD_DIM), lambda b, g, i: (b, kv_block0 + 4 * c + g))

    def vt_spec(which):
        return pl.BlockSpec((None, None, S // HEAD_DIM, VT_ROWS, HEAD_DIM), lambda b, g, i: (b, which * G + g, 0, 0, 0))

    rows = lambda width, col: pl.BlockSpec((tq * n_sub, width), lambda b, g, i: (b * nQ + i, col(g)))
    per_bg = lambda shape: pl.BlockSpec((None,) + shape, lambda b, g, i: (b * G + g, 0, 0))
    const = lambda shape: pl.BlockSpec(shape, lambda b, g, i: (0,) * len(shape))
    return pl.pallas_call(
        functools.partial(_nsa_kernel, tq=tq, tk=tk, n_sel=min(NSA_N_SEL, n_blk), n_sub=n_sub),
        out_shape=jax.ShapeDtypeStruct((T, G * gw), BF16),
        grid=(B, G, nQ),
        in_specs=[rows(gw, lambda g: q_tile0 + g), per_bg((n_cmp, HEAD_DIM)), per_bg((HEAD_DIM, n_cmp)),
                  k_spec(2), vt_spec(0), k_spec(4), vt_spec(1),
                  rows(HEAD_DIM, lambda g: 0), rows(gw, lambda g: gate_tile0 + g),
                  const((NSA_BLK_PAD, n_cmp)), const((S, NSA_BLK_PAD))],
        out_specs=rows(gw, lambda g: g),
        scratch_shapes=[pltpu.VMEM((n_sub, HEAD_DIM, tq), F32)],
        compiler_params=_params("parallel", "parallel", "arbitrary"),
        name="nsa_attention",
    )(h2, kc, vct, h2, vt, h2, vt, small, h2, jnp.asarray(ovt, BF16), jnp.asarray(block_onehot, BF16))


def _gla_kernel(q_ref, k_ref, v_ref, a_ref, gate_ref, wa_ref, ba_ref, ng_ref, o_ref, state_ref, intra_ref,
                *, n_chunks, n_heads):
    C = GLA_CHUNK

    @pl.when(pl.program_id(2) == 0)
    def _():
        state_ref[...] = jnp.zeros_like(state_ref)

    r_i = lax.broadcasted_iota(jnp.int32, (C, C), 0)
    c_i = lax.broadcasted_iota(jnp.int32, (C, C), 1)
    causal = r_i >= c_i
    tri = jnp.where(causal, 1.0, 0.0).astype(BF16)
    ng = ng_ref[...]
    heads = range(n_heads)
    kcols = lambda h: slice(h * GLA_DK, (h + 1) * GLA_DK)
    vcols = lambda h: slice(h * GLA_DV, (h + 1) * GLA_DV)

    qe_all, kd_all, decay_all = [], [], []
    for c in range(n_chunks):
        rows = slice(c * C, (c + 1) * C)
        a = a_ref[rows, :]
        pre = [jnp.dot(a, wa_ref[h], preferred_element_type=F32) + ba_ref[h] for h in heads]
        log_a = [(jnp.minimum(p, 0.0) - jnp.log(1.0 + jnp.exp(-jnp.abs(p)))) * (1.0 / GLA_TAU) for p in pre]
        bcum = [sum(jnp.dot(tri, part, preferred_element_type=F32) for part in _split3(la)) for la in log_a]
        b_last = [bc[C - 1:C, :] for bc in bcum]
        q = [q_ref[rows, kcols(h)].astype(F32) for h in heads]
        k = [k_ref[rows, kcols(h)].astype(F32) for h in heads]
        qe = [(q[h] * jnp.exp(bcum[h])).astype(BF16) for h in heads]
        ke = [(k[h] * jnp.exp(-bcum[h])).astype(BF16) for h in heads]
        kd = [(k[h] * jnp.exp(b_last[h] - bcum[h])).astype(BF16) for h in heads]
        attn = [lax.dot_general(qe[h], ke[h], NT_DIMS, preferred_element_type=F32) for h in heads]
        attn = [jnp.where(causal, at, 0.0).astype(BF16) for at in attn]
        for h in heads:
            intra_ref[rows, vcols(h)] = jnp.dot(attn[h], v_ref[rows, vcols(h)], preferred_element_type=F32)
        qe_all.append(qe)
        kd_all.append(kd)
        decay_all.append([jnp.exp(bl) for bl in b_last])

    for c in range(n_chunks):
        rows = slice(c * C, (c + 1) * C)
        state_t = [state_ref[h] for h in heads]
        inter = [lax.dot_general(qe_all[c][h], state_t[h].astype(BF16), NT_DIMS, preferred_element_type=F32)
                 for h in heads]
        upd = [lax.dot_general(v_ref[rows, vcols(h)], kd_all[c][h], TN_DIMS, preferred_element_type=F32) for h in heads]
        for h in heads:
            state_ref[h] = state_t[h] * decay_all[c][h] + upd[h]
        for h in heads:
            o = intra_ref[rows, vcols(h)] + inter[h]
            o = o * lax.rsqrt(jnp.mean(o * o, axis=-1, keepdims=True) + NORM_EPS) * ng
            o_ref[rows, vcols(h)] = (o * _silu(gate_ref[rows, vcols(h)].astype(F32))).astype(o_ref.dtype)


def _gla(h2, small, w_a2, b_a, norm_g, B, S, *, q_col0, k_col0, v_col0, gate_col0, rows, heads_per_step):
    T = h2.shape[0]
    H, hp = GLA_HEADS, heads_per_step
    nR = S // rows
    a_lo = 3 * NSA_KV_GROUPS * NSA_HEADS_PER_GROUP
    assert H % hp == 0 and all(c % (hp * w) == 0 for c, w in ((q_col0, GLA_DK), (k_col0, GLA_DK), (v_col0, GLA_DV),
                                                              (gate_col0, GLA_DV)))
    wa = jnp.zeros((H, HEAD_DIM, GLA_DK), F32).at[:, a_lo:a_lo + GLA_RANK, :].set(
        w_a2.reshape(GLA_RANK, H, GLA_DK).transpose(1, 0, 2))
    blk = lambda width, col0: pl.BlockSpec((rows, hp * width), lambda b, g, i: (b * nR + i, col0 // (hp * width) + g))
    per_head = lambda shape: pl.BlockSpec((hp,) + shape, lambda b, g, i: (g, 0, 0))
    return pl.pallas_call(
        functools.partial(_gla_kernel, n_chunks=rows // GLA_CHUNK, n_heads=hp),
        out_shape=jax.ShapeDtypeStruct((T, H * GLA_DV), BF16),
        grid=(B, H // hp, nR),
        in_specs=[blk(GLA_DK, q_col0), blk(GLA_DK, k_col0), blk(GLA_DV, v_col0),
                  pl.BlockSpec((rows, HEAD_DIM), lambda b, g, i: (b * nR + i, 0)), blk(GLA_DV, gate_col0),
                  per_head((HEAD_DIM, GLA_DK)), per_head((1, GLA_DK)),
                  pl.BlockSpec((1, GLA_DV), lambda b, g, i: (0, 0))],
        out_specs=blk(GLA_DV, 0),
        scratch_shapes=[pltpu.VMEM((hp, GLA_DV, GLA_DK), F32), pltpu.VMEM((rows, hp * GLA_DV), F32)],
        compiler_params=_params("parallel", "parallel", "arbitrary"),
        name="gla",
    )(h2, h2, h2, small, h2, wa, b_a.reshape(H, 1, GLA_DK), norm_g.reshape(1, GLA_DV))


def _dilated_scores(q, kp, kc):
    head = lambda h: slice(h * HEAD_DIM, (h + 1) * HEAD_DIM)
    return [lax.dot_general(q[:, head(h)], jnp.concatenate([kp[:, head(h)], kc[:, head(h)]], axis=0), NT_DIMS,
                            preferred_element_type=F32) for h in range(DIL_HEADS)]


def _dilated_finish(i, scores, vp, vc, prev, gate, o_scr, m_scr, d_scr):
    tq = o_scr.shape[0]
    head = lambda h: slice(h * HEAD_DIM, (h + 1) * HEAD_DIM)
    row = lax.broadcasted_iota(jnp.int32, (tq, 2 * tq), 0)
    col = lax.broadcasted_iota(jnp.int32, (tq, 2 * tq), 1)
    valid = (col >= row + (tq - DIL_BACK)) & (col <= row + tq) & ((col >= tq) | (i > 0))
    m_scr[...] = jnp.zeros(m_scr.shape, F32)
    d_scr[...] = jnp.ones(d_scr.shape, F32)
    for h in range(DIL_HEADS):
        v = jnp.concatenate([vp[:, head(h)], vc[:, head(h)]], axis=0)
        s = jnp.where(valid, scores[h], NEG_INF)
        m = jnp.max(s, axis=1, keepdims=True)
        e = jnp.exp(s - m)
        o_scr[:, head(h)] = jnp.dot(e.astype(BF16), v, preferred_element_type=F32)
        m_scr[:, h:h + 1] = m
        d_scr[:, h:h + 1] = jnp.sum(e, axis=1, keepdims=True)
    den = d_scr[...]
    lse = m_scr[...] + jnp.log(den)
    w_cur = 1.0 / den
    if prev is not None:
        prev_o, lse_p = prev
        lse_n = jnp.maximum(lse_p, lse) + jnp.log(1.0 + jnp.exp(-jnp.abs(lse_p - lse)))
        w_prev = jnp.exp(lse_p - lse_n)
        w_cur = jnp.exp(lse - lse_n) * w_cur
        lse = lse_n
    outs = []
    for h in range(DIL_HEADS):
        o = o_scr[:, head(h)] * w_cur[:, h:h + 1]
        if prev is not None:
            o = o + prev_o[:, head(h)].astype(F32) * w_prev[:, h:h + 1]
        if gate is not None:
            o = o * _silu(gate[:, head(h)].astype(F32))
        outs.append(o)
    return outs, lse


def _dilated_kernel(*refs, has_prev, is_last, n_sub):
    q_ref, kp_ref, kc_ref, vp_ref, vc_ref = refs[:5]
    rest = list(refs[5:])
    po_ref, pl_ref = (rest.pop(0), rest.pop(0)) if has_prev else (None, None)
    gate_ref = rest.pop(0) if is_last else None
    o_ref = rest.pop(0)
    lse_ref = None if is_last else rest.pop(0)
    o_scr, m_scr, d_scr = rest
    tq = o_scr.shape[1]
    tile = lambda ref, u: ref.at[pl.ds(u * tq, tq)]
    before = lambda prev_ref, cur_ref, u: prev_ref if u == 0 else tile(cur_ref, u - 1)
    scores = [_dilated_scores(tile(q_ref, u), before(kp_ref, kc_ref, u), tile(kc_ref, u)) for u in range(n_sub)]
    for u in range(n_sub):
        prev = (tile(po_ref, u), tile(pl_ref, u)[...]) if has_prev else None
        outs, lse = _dilated_finish(pl.program_id(2) * n_sub + u, scores[u], before(vp_ref, vc_ref, u),
                                    tile(vc_ref, u), prev, tile(gate_ref, u) if is_last else None,
                                    o_scr.at[u], m_scr.at[u], d_scr.at[u])
        for h, o in enumerate(outs):
            o_ref[pl.ds(u * tq, tq), h * HEAD_DIM:(h + 1) * HEAD_DIM] = o.astype(o_ref.dtype)
        if lse_ref is not None:
            lse_ref[pl.ds(u * tq, tq), :] = lse


def _dilated_scratch(tq, W, n=None):
    lead = () if n is None else (n,)
    return [pltpu.VMEM(lead + (tq, W), F32), pltpu.VMEM(lead + (tq, HEAD_DIM), F32),
            pltpu.VMEM(lead + (tq, HEAD_DIM), F32)]


def _dilated_group(hg, prev_o, prev_lse, gate, *, tq, n_sub=1):
    B, d, L, W3 = hg.shape
    W = W3 // 3
    has_prev, is_last = prev_o is not None, gate is not None
    n_sub = n_sub if L % (n_sub * tq) == 0 else 1
    ts = n_sub * tq
    assert tq == DIL_BACK and L % ts == 0
    cur = lambda col: pl.BlockSpec((None, None, ts, W), lambda b, r, i: (b, r, i, col))
    prv = lambda col: pl.BlockSpec((None, None, tq, W), lambda b, r, i: (b, r, jnp.maximum(n_sub * i - 1, 0), col))
    tok = lambda width: pl.BlockSpec((None, ts, width), lambda b, r, i: (b, i, r))
    in_specs = [cur(0), prv(1), cur(1), prv(2), cur(2)]
    args = [hg] * 5
    if has_prev:
        in_specs += [tok(W), tok(HEAD_DIM)]
        args += [prev_o, prev_lse]
    if is_last:
        in_specs += [tok(W)]
        args += [gate]
    out_shape = [jax.ShapeDtypeStruct((B, L, d * W), BF16)]
    out_specs = [tok(W)]
    if not is_last:
        out_shape += [jax.ShapeDtypeStruct((B, L, d * HEAD_DIM), F32)]
        out_specs += [tok(HEAD_DIM)]
    res = pl.pallas_call(
        functools.partial(_dilated_kernel, has_prev=has_prev, is_last=is_last, n_sub=n_sub),
        out_shape=tuple(out_shape),
        grid=(B, d, L // ts),
        in_specs=in_specs,
        out_specs=tuple(out_specs),
        scratch_shapes=_dilated_scratch(tq, W, n_sub),
        compiler_params=_params("parallel", "parallel", "arbitrary"),
        name=f"dilated_d{d}",
    )(*args)
    return res[0] if is_last else (res[0], res[1])


def _regroup_kernel(*refs, d, ratio):
    q_ref, kp_ref, kc_ref, vp_ref, vc_ref = refs[:5]
    po_refs = refs[5:5 + ratio]
    pl_refs = refs[5 + ratio:5 + 2 * ratio]
    o_ref, lse_ref, o_scr, m_scr, d_scr, so_scr, sl_scr = refs[5 + 2 * ratio:]
    i = pl.program_id(1)
    r = pl.program_id(2)
    tq, W = o_scr.shape
    rows_c = tq // ratio
    n = d * tq
    shift = lambda v: v.bit_length() - 1
    dst = lax.broadcasted_iota(jnp.int32, (tq, tq), 0)
    src = lax.broadcasted_iota(jnp.int32, (tq, tq), 1)
    gather = jnp.where(src == (dst & (ratio - 1)) * rows_c + (dst >> shift(ratio)), 1.0, 0.0).astype(BF16)
    prev_o = jnp.dot(gather, jnp.concatenate([p[...] for p in po_refs], axis=0), preferred_element_type=F32)
    prev_l = sum(jnp.dot(gather, part, preferred_element_type=F32)
                 for part in _split3(jnp.concatenate([p[...] for p in pl_refs], axis=0)))
    outs, lse = _dilated_finish(i, _dilated_scores(q_ref, kp_ref, kc_ref), vp_ref, vc_ref, (prev_o, prev_l), None,
                                o_scr, m_scr, d_scr)
    so_scr[r] = jnp.concatenate(outs, axis=1).astype(BF16)
    sl_scr[r] = lse

    @pl.when(r == d - 1)
    def _():
        dst = lax.broadcasted_iota(jnp.int32, (n, n), 0)
        src = lax.broadcasted_iota(jnp.int32, (n, n), 1)
        scatter = jnp.where(src == (dst & (d - 1)) * tq + (dst >> shift(d)), 1.0, 0.0).astype(BF16)
        o_ref[...] = jnp.dot(scatter, so_scr[...].reshape(n, W), preferred_element_type=F32).astype(o_ref.dtype)
        lse_ref[...] = sum(jnp.dot(scatter, part, preferred_element_type=F32)
                           for part in _split3(sl_scr[...].reshape(n, HEAD_DIM)))


def _dilated_regroup(hg, prev_o, prev_lse, *, fine, tq):
    B, d, L, W3 = hg.shape
    W = W3 // 3
    S = L * d
    ratio = fine // d
    assert tq == DIL_BACK and L % tq == 0 and fine % d == 0 and tq % ratio == 0
    assert d & (d - 1) == 0 and ratio & (ratio - 1) == 0
    rows_c = tq // ratio
    cur = lambda col: pl.BlockSpec((None, None, tq, W), lambda b, i, r: (b, r, i, col))
    prv = lambda col: pl.BlockSpec((None, None, tq, W), lambda b, i, r: (b, r, jnp.maximum(i - 1, 0), col))
    piece = lambda width: [pl.BlockSpec((None, rows_c, width), lambda b, i, r, j=j: (b, i, j * d + r))
                           for j in range(ratio)]
    return pl.pallas_call(
        functools.partial(_regroup_kernel, d=d, ratio=ratio),
        out_shape=(jax.ShapeDtypeStruct((B, S, W), BF16), jax.ShapeDtypeStruct((B, S, HEAD_DIM), F32)),
        grid=(B, L // tq, d),
        in_specs=[cur(0), prv(1), cur(1), prv(2), cur(2)] + piece(W) + piece(HEAD_DIM),
        out_specs=(pl.BlockSpec((None, d * tq, W), lambda b, i, r: (b, i, 0)),
                   pl.BlockSpec((None, d * tq, HEAD_DIM), lambda b, i, r: (b, i, 0))),
        scratch_shapes=_dilated_scratch(tq, W) + [pltpu.VMEM((d, tq, W), BF16), pltpu.VMEM((d, tq, HEAD_DIM), F32)],
        compiler_params=_params("parallel", "arbitrary", "arbitrary"),
        name=f"dilated_regroup_d{d}",
    )(*([hg] * 5 + [prev_o] * ratio + [prev_lse] * ratio))


def _row_tile(n, target):
    t = min(n, target)
    assert n % t == 0
    return t


def _even_layer(xb4, w_in, pe_k, w1_k, w2_k, pe_v, w1_v, w2_v, gla_w_a2, gla_b_a, gla_norm_g, w_out):
    B, _, S, D = xb4.shape
    T = B * S
    nq = NSA_KV_GROUPS * NSA_HEADS_PER_GROUP * HEAD_DIM
    nkv = 6 * NSA_KV_GROUPS * HEAD_DIM
    ng = 3 * NSA_KV_GROUPS * NSA_HEADS_PER_GROUP
    gk = GLA_HEADS * GLA_DK
    gv = GLA_HEADS * GLA_DV
    o_g = nq + nkv
    o_a = o_g + ng + nq + 2 * gk + gv
    w16 = w_in.astype(BF16)
    w_main = jnp.concatenate([w16[:, :o_g], w16[:, o_g + ng:o_a], w16[:, o_a + GLA_RANK:]], axis=1)
    w_small = jnp.concatenate([w16[:, o_g:o_g + ng], w16[:, o_a:o_a + GLA_RANK],
                               jnp.zeros((D, HEAD_DIM - ng - GLA_RANK), BF16)], axis=1)
    tn = 512
    plain, rope = (False, 1.0), (True, 1.0)
    groups = [("nsa_q", [(True, Q_SCALE * LOG2E)] * (nq // tn)),
              ("nsa_kv", [plain, plain, rope, plain, rope, plain]),
              ("nsa_gate", [plain] * (nq // tn)), ("gla_q", [(False, GLA_Q_SCALE)] * (gk // tn)),
              ("gla_k", [plain] * (gk // tn)), ("gla_v", [plain] * (gv // tn)), ("gla_gate", [plain] * (gv // tn))]
    w_block, n_seen = {}, 0
    for name, kinds in groups:
        w_block[name] = n_seen
        n_seen += len(kinds)
    order = ["nsa_q", "gla_v", "gla_gate", "nsa_gate", "gla_q", "gla_k", "nsa_kv"]
    tiles, col = [], {}
    for name in order:
        col[name] = len(tiles) * tn
        tiles += [(w_block[name] + t,) + kind for t, kind in enumerate(dict(groups)[name])]
    pos = jnp.arange(S)[None]
    tm = _row_tile(S, 1024)
    h2 = _project(xb4, w_main, tiles, pos, tm=tm, tn=tn, out_dtype=BF16).reshape(T, -1)
    small = _project(xb4, w_small, [(0,) + plain], pos, tm=tm, tn=HEAD_DIM, out_dtype=F32).reshape(T, HEAD_DIM)

    kv_block0 = col["nsa_kv"] // HEAD_DIM
    kc, vct = _nsa_compress(h2, B, S, pe_k, w1_k, w2_k, pe_v, w1_v, w2_v, kv_block0, kv_block0 + NSA_KV_GROUPS)
    v_blocks = [kv_block0 + 4 * c + g for c in (3, 5) for g in range(NSA_KV_GROUPS)]
    vt = _transpose_values(h2, B, S, v_blocks, rows=_row_tile(S, 1024))
    o_nsa = _nsa_attention(h2, small, kc, vct, vt, B, S, q_tile0=col["nsa_q"] // tn, kv_block0=kv_block0,
                           gate_tile0=col["nsa_gate"] // tn, tq=128, tk=_row_tile(S, 512), n_sub=4)
    o_gla = _gla(h2, small, gla_w_a2, gla_b_a, gla_norm_g, B, S, q_col0=col["gla_q"], k_col0=col["gla_k"],
                 v_col0=col["gla_v"], gate_col0=col["gla_gate"], rows=_row_tile(S, 512), heads_per_step=GLA_HEADS)
    return _out_project([o_nsa, o_gla], w_out, tm=_row_tile(T, 1024), tn=512)


def _odd_layer(streams, w_in, w_out):
    B, _, S, D = streams[1].shape
    T = B * S
    n_g = len(DIL_GROUPS)
    W = DIL_HEADS * HEAD_DIM
    tn = 512
    blocks = lambda col0: [col0 // tn + t for t in range(W // tn)]
    gate = _project(streams[1], w_in, [(blk, False, 1.0) for blk in blocks(3 * n_g * W)], jnp.arange(S)[None],
                    tm=_row_tile(S, 1024), tn=tn, out_dtype=BF16).reshape(B, S, W)

    def group_qkv(gi, d):
        L = S // d
        tiles = ([(blk, True, Q_SCALE) for blk in blocks(gi * W)] + [(blk, True, 1.0) for blk in blocks((n_g + gi) * W)]
                 + [(blk, False, 1.0) for blk in blocks((2 * n_g + gi) * W)])
        tm = _row_tile(L, 1024)
        return _project(streams[d], w_in, tiles, jnp.arange(S).reshape(L, d).T, tm=tm, tn=tn, out_dtype=BF16,
                        streams_per_step=min(d, 1024 // tm) if tm == L else 1, n_split=max(1, tm // 256))

    order = sorted(range(n_g), key=lambda gi: -DIL_GROUPS[gi][1])
    assert len(order) == 3 and DIL_GROUPS[order[-1]][1] == 1
    for gi in order:
        assert DIL_GROUPS[gi][0] // DIL_GROUPS[gi][1] == DIL_BACK
    d_hi, d_mid = DIL_GROUPS[order[0]][1], DIL_GROUPS[order[1]][1]
    o, lse = _dilated_group(group_qkv(order[0], d_hi), None, None, None, tq=DIL_BACK)
    o, lse = _dilated_regroup(group_qkv(order[1], d_mid), o, lse, fine=d_hi, tq=DIL_BACK)
    o = _dilated_group(group_qkv(order[2], 1), o, lse, gate, tq=DIL_BACK)
    return _out_project([o.reshape(T, W)], w_out, tm=_row_tile(T, 1024), tn=512)


def kernel(x, l0_w_in, l0_nsa_pe_k, l0_nsa_w1_k, l0_nsa_w2_k, l0_nsa_pe_v, l0_nsa_w1_v, l0_nsa_w2_v, l0_gla_w_a2, l0_gla_b_a, l0_gla_norm_g, l0_w_out, l0_ln_g, l0_ln_b, l1_w_in, l1_w_out, l1_ln_g, l1_ln_b):
    B, S, D = x.shape
    tr = _row_tile(S, 256)
    dils = sorted({d for _, d in DIL_GROUPS} | {1})
    y0 = _even_layer(x.astype(BF16)[:, None], l0_w_in, l0_nsa_pe_k, l0_nsa_w1_k, l0_nsa_w2_k, l0_nsa_pe_v, l0_nsa_w1_v,
                     l0_nsa_w2_v, l0_gla_w_a2, l0_gla_b_a, l0_gla_norm_g, l0_w_out)
    x1, *x1_streams = _deepnorm_ln(x, y0, l0_ln_g, l0_ln_b, tr=tr, dilations=dils)
    y1 = _odd_layer(dict(zip(dils, x1_streams)), l1_w_in, l1_w_out)
    return _deepnorm_ln(x1, y1, l1_ln_g, l1_ln_b, tr=tr)[0]
```

```python
import functools

import numpy as np
import jax
import jax.numpy as jnp
from jax import lax
from jax.experimental import pallas as pl
from jax.experimental.pallas import tpu as pltpu

F32 = jnp.float32
BF16 = jnp.bfloat16

HEAD_DIM = 128
ROPE_THETA = 10000.0
LN_EPS = 1e-5
NORM_EPS = 1e-6
NEG_INF = -1e30
M_INIT = -1e29
TAKEN = -3e38
SEL_UNROLL = 4
SEL_SUB = 256
VT_ROWS = HEAD_DIM + 16

NSA_KV_GROUPS = 4
NSA_HEADS_PER_GROUP = 4
NSA_CMP_LEN = 32
NSA_CMP_STRIDE = 16
NSA_SEL_LEN = 64
NSA_N_SEL = 16
NSA_WINDOW = 512
NSA_FORCE_BONUS = 1e4
NSA_BLK_PAD = 128

GLA_HEADS = 4
GLA_DK = 256
GLA_DV = 512
GLA_RANK = 16
GLA_TAU = 16.0
GLA_CHUNK = 64

DIL_GROUPS = ((128, 1), (512, 4), (2048, 16))
DIL_HEADS = 16
DIL_BACK = 128

DEPTH = 2
DEEPNORM_ALPHA = (2.0 * DEPTH) ** 0.25

VMEM_LIMIT_BYTES = 56 * 1024 * 1024

Q_SCALE = HEAD_DIM ** -0.5
LOG2E = 1.4426950408889634
GLA_Q_SCALE = GLA_DK ** -0.5

NT_DIMS = (((1,), (1,)), ((), ()))
TN_DIMS = (((0,), (0,)), ((), ()))


def _params(*sem):
    return pltpu.CompilerParams(dimension_semantics=sem, vmem_limit_bytes=VMEM_LIMIT_BYTES)


def _silu(x):
    return x * (1.0 / (1.0 + jnp.exp(-x)))


def _rope_tables(pos):
    half = HEAD_DIM // 2
    inv_freq = ROPE_THETA ** (-jnp.arange(half, dtype=F32) / half)
    ang = pos.astype(F32)[:, None] * inv_freq[None, :]
    cos = jnp.cos(ang)
    sin = jnp.sin(ang)
    return jnp.concatenate([cos, cos], axis=-1), jnp.concatenate([-sin, sin], axis=-1)


def _rope(a, cos, sin):
    return a * cos + pltpu.roll(a, HEAD_DIM // 2, 1) * sin


def _proj_kernel(wblk_ref, tbase_ref, tstep_ref, a_ref, w_ref, c_ref, s_ref, o_ref, *, n_split):
    w = w_ref[...].astype(BF16)
    sb, tm, _ = a_ref.shape
    chunk = tm // n_split
    for st in range(sb):
        for h in range(n_split):
            rows = slice(h * chunk, (h + 1) * chunk)
            trows = slice(st * tm + h * chunk, st * tm + (h + 1) * chunk)
            acc = jnp.dot(a_ref[st, rows, :], w, preferred_element_type=F32)
            c = c_ref[trows, :]
            s = s_ref[trows, :]
            for k in range(acc.shape[1] // HEAD_DIM):
                sl = slice(k * HEAD_DIM, (k + 1) * HEAD_DIM)
                o_ref[st, rows, sl] = (acc[:, sl] * c + pltpu.roll(acc[:, sl], HEAD_DIM // 2, 1) * s).astype(o_ref.dtype)


def _project(act4, w, tiles, pos, *, tm, tn, out_dtype, streams_per_step=1, n_split=4):
    B, d, L, K = act4.shape
    sb = streams_per_step
    nI, nJ, nR = L // tm, len(tiles), d // sb
    assert L % tm == 0 and pos.shape == (d, L) and d % sb == 0 and (sb == 1 or nI == 1) and tm % n_split == 0
    tr = sb * tm
    kinds = sorted({(rope, scale) for _, rope, scale in tiles})
    cos, sin = _rope_tables(pos.reshape(-1))
    c_rows, s_rows, base = [], [], {}
    n_blocks = 0
    for rope, scale in kinds:
        base[(rope, scale)] = n_blocks
        if rope:
            c_rows += [cos * scale]
            s_rows += [sin * scale]
            n_blocks += nR * nI
        else:
            c_rows += [jnp.full((tr, HEAD_DIM), scale, F32)]
            s_rows += [jnp.zeros((tr, HEAD_DIM), F32)]
            n_blocks += 1
    c_tab, s_tab = jnp.concatenate(c_rows, axis=0), jnp.concatenate(s_rows, axis=0)
    wblk = jnp.asarray([t[0] for t in tiles], jnp.int32)
    tbase = jnp.asarray([base[(t[1], t[2])] for t in tiles], jnp.int32)
    tstep = jnp.asarray([int(t[1]) for t in tiles], jnp.int32)

    def split(m):
        return m // (nR * nI), (m // nI) % nR, m % nI

    def a_idx(m, j, wb, tb, ts):
        b, rg, i = split(m)
        return (b, rg, i, 0)

    def t_idx(m, j, wb, tb, ts):
        b, rg, i = split(m)
        return (tb[j] + ts[j] * (rg * nI + i), 0)

    def o_idx(m, j, wb, tb, ts):
        b, rg, i = split(m)
        return (b, rg, i, j)

    grid_spec = pltpu.PrefetchScalarGridSpec(
        num_scalar_prefetch=3,
        grid=(B * nR * nI, nJ),
        in_specs=[
            pl.BlockSpec((None, sb, tm, K), a_idx),
            pl.BlockSpec((K, tn), lambda m, j, wb, tb, ts: (0, wb[j])),
            pl.BlockSpec((tr, HEAD_DIM), t_idx),
            pl.BlockSpec((tr, HEAD_DIM), t_idx),
        ],
        out_specs=pl.BlockSpec((None, sb, tm, tn), o_idx),
    )
    return pl.pallas_call(
        functools.partial(_proj_kernel, n_split=n_split),
        out_shape=jax.ShapeDtypeStruct((B, d, L, nJ * tn), out_dtype),
        grid_spec=grid_spec,
        compiler_params=_params("parallel", "arbitrary"),
        name=f"proj_d{d}_n{nJ * tn}",
    )(wblk, tbase, tstep, act4, w, c_tab, s_tab)


def _outproj_kernel(*refs, n_split):
    o_ref = refs[-1]
    ws = [w_ref[...].astype(BF16) for w_ref in refs[1:-1:2]]
    chunk = o_ref.shape[0] // n_split
    for h in range(n_split):
        rows = slice(h * chunk, (h + 1) * chunk)
        acc = None
        for a_ref, w in zip(refs[0:-1:2], ws):
            part = jnp.dot(a_ref[rows, :], w, preferred_element_type=F32)
            acc = part if acc is None else acc + part
        o_ref[rows, :] = acc.astype(o_ref.dtype)


def _out_project(acts, w, *, tm, tn, n_split=4):
    T, K = acts[0].shape
    N = w.shape[1]
    assert all(a.shape == (T, K) for a in acts) and w.shape[0] == len(acts) * K
    in_specs, args = [], []
    for n, a in enumerate(acts):
        in_specs += [pl.BlockSpec((tm, K), lambda i, j: (i, 0)), pl.BlockSpec((K, tn), lambda i, j, n=n: (n, j))]
        args += [a, w]
    return pl.pallas_call(
        functools.partial(_outproj_kernel, n_split=n_split),
        out_shape=jax.ShapeDtypeStruct((T, N), BF16),
        grid=(T // tm, N // tn),
        in_specs=in_specs,
        out_specs=pl.BlockSpec((tm, tn), lambda i, j: (i, j)),
        compiler_params=_params("parallel", "arbitrary"),
        name=f"outproj_{len(acts)}",
    )(*args)


def _ln_kernel(x_ref, y_ref, g_ref, b_ref, o_ref, *stream_refs, dilations):
    z = DEEPNORM_ALPHA * x_ref[...] + y_ref[...].astype(F32)
    mu = jnp.mean(z, axis=-1, keepdims=True)
    zc = z - mu
    var = jnp.mean(zc * zc, axis=-1, keepdims=True)
    out = zc * lax.rsqrt(var + LN_EPS) * g_ref[...] + b_ref[...]
    o_ref[...] = out
    tr = out.shape[0]
    ob = out.astype(BF16)
    dst = lax.broadcasted_iota(jnp.int32, (tr, tr), 0)
    src = lax.broadcasted_iota(jnp.int32, (tr, tr), 1)
    for d, s_ref in zip(dilations, stream_refs):
        if d == 1:
            s_ref[0] = ob
            continue
        per = tr // d
        perm = jnp.where(src == (dst & (per - 1)) * d + (dst >> (per.bit_length() - 1)), 1.0, 0.0).astype(BF16)
        sm = jnp.dot(perm, ob, preferred_element_type=F32).astype(BF16)
        for r in range(d):
            s_ref[r] = sm[r * per:(r + 1) * per]


def _deepnorm_ln(x3, y2, g, b, *, tr, dilations=()):
    B, S, D = x3.shape
    nI = S // tr
    assert S % tr == 0 and all(tr % (d * 16) == 0 and d & (d - 1) == 0 for d in dilations) and tr & (tr - 1) == 0
    row3 = pl.BlockSpec((None, tr, D), lambda b, i: (b, i, 0))
    vec = pl.BlockSpec((1, D), lambda b, i: (0, 0))
    out_shape = [jax.ShapeDtypeStruct((B, S, D), F32)]
    out_specs = [row3]
    for d in dilations:
        out_shape.append(jax.ShapeDtypeStruct((B, d, S // d, D), BF16))
        out_specs.append(pl.BlockSpec((None, d, tr // d, D), lambda b, i: (b, 0, i, 0)))
    return pl.pallas_call(
        functools.partial(_ln_kernel, dilations=tuple(dilations)),
        out_shape=tuple(out_shape),
        grid=(B, nI),
        in_specs=[row3, pl.BlockSpec((tr, D), lambda b, i: (b * nI + i, 0)), vec, vec],
        out_specs=tuple(out_specs),
        compiler_params=_params("parallel", "parallel"),
        name="deepnorm_ln",
    )(x3, y2, g.reshape(1, D), b.reshape(1, D))


def _compress_kernel(zk_ref, zv_ref, pe_k_ref, pe_v_ref, w1ka_ref, w1kb_ref, w1va_ref, w1vb_ref,
                     w2k_ref, w2vt_ref, cos_ref, sin_ref, kc_ref, vct_ref, uk_ref, vk_ref, uv_ref, vv_ref):
    p = pl.program_id(1)
    n = uk_ref.shape[0]

    @pl.when(p == 0)
    def _():
        for ref in (uk_ref, vk_ref, uv_ref, vv_ref):
            ref[...] = jnp.zeros_like(ref)

    def accumulate(z_ref, pe_ref, wa_ref, wb_ref, u_ref, v_ref):
        z = z_ref[...].astype(F32)
        za = (z + pe_ref[pl.ds(p, 1), :]).astype(BF16)
        zb = (z + pe_ref[pl.ds(p + NSA_CMP_STRIDE, 1), :]).astype(BF16)
        u_ref[...] += jnp.dot(za, wa_ref[...], preferred_element_type=F32)
        v_ref[...] += jnp.dot(zb, wb_ref[...], preferred_element_type=F32)

    accumulate(zk_ref, pe_k_ref, w1ka_ref, w1kb_ref, uk_ref, vk_ref)
    accumulate(zv_ref, pe_v_ref, w1va_ref, w1vb_ref, uv_ref, vv_ref)

    @pl.when(p == NSA_CMP_STRIDE - 1)
    def _():
        hid_k = _silu(uk_ref[...] + pltpu.roll(vk_ref[...], n - 1, 0)).astype(BF16)
        hid_v = _silu(uv_ref[...] + pltpu.roll(vv_ref[...], n - 1, 0)).astype(BF16)
        kc = jnp.dot(hid_k, w2k_ref[...], preferred_element_type=F32)
        kc = _rope(kc, cos_ref[...], sin_ref[...])
        row = lax.broadcasted_iota(jnp.int32, kc.shape, 0)
        kc_ref[...] = jnp.where(row < n - 1, kc, 0.0).astype(kc_ref.dtype)
        vct = lax.dot_general(w2vt_ref[...], hid_v, NT_DIMS, preferred_element_type=F32)
        col = lax.broadcasted_iota(jnp.int32, vct.shape, 1)
        vct_ref[...] = jnp.where(col < n - 1, vct, 0.0).astype(vct_ref.dtype)


def _nsa_compress(h2, B, S, pe_k, w1_k, w2_k, pe_v, w1_v, w2_v, kc_block, vc_block):
    G = NSA_KV_GROUPS
    n = S // NSA_CMP_STRIDE
    c0, c1 = kc_block * HEAD_DIM, (vc_block + G) * HEAD_DIM
    nb = (c1 - c0) // HEAD_DIM
    h3 = h2[:, c0:c1].reshape(B, n, NSA_CMP_STRIDE * (c1 - c0))
    kc_block, vc_block = 0, vc_block - kc_block
    cos, sin = _rope_tables(jnp.arange(n) * NSA_CMP_STRIDE + NSA_CMP_LEN - 1)

    def z_spec(block0):
        return pl.BlockSpec((None, n, HEAD_DIM), lambda bg, p: (bg // G, 0, p * nb + block0 + bg % G))

    full = lambda shape: pl.BlockSpec(shape, lambda bg, p: (0,) * len(shape))
    w1a = pl.BlockSpec((HEAD_DIM, HEAD_DIM), lambda bg, p: (p, 0))
    w1b = pl.BlockSpec((HEAD_DIM, HEAD_DIM), lambda bg, p: (p + NSA_CMP_STRIDE, 0))
    acc = pltpu.VMEM((n, HEAD_DIM), F32)
    return pl.pallas_call(
        _compress_kernel,
        out_shape=(jax.ShapeDtypeStruct((B * G, n, HEAD_DIM), BF16), jax.ShapeDtypeStruct((B * G, HEAD_DIM, n), BF16)),
        grid=(B * G, NSA_CMP_STRIDE),
        in_specs=[z_spec(kc_block), z_spec(vc_block), full((NSA_CMP_LEN, HEAD_DIM)), full((NSA_CMP_LEN, HEAD_DIM)),
                  w1a, w1b, w1a, w1b, full((HEAD_DIM, HEAD_DIM)), full((HEAD_DIM, HEAD_DIM)),
                  full((n, HEAD_DIM)), full((n, HEAD_DIM))],
        out_specs=(pl.BlockSpec((None, n, HEAD_DIM), lambda bg, p: (bg, 0, 0)),
                   pl.BlockSpec((None, HEAD_DIM, n), lambda bg, p: (bg, 0, 0))),
        scratch_shapes=[acc, acc, acc, acc],
        compiler_params=_params("parallel", "arbitrary"),
        name="nsa_compress",
    )(h3, h3, pe_k, pe_v, w1_k.astype(BF16), w1_k.astype(BF16), w1_v.astype(BF16), w1_v.astype(BF16),
      w2_k.astype(BF16), w2_v.T.astype(BF16), cos, sin)


def _split3(x):
    hi = x.astype(BF16)
    r1 = x - hi.astype(F32)
    mid = r1.astype(BF16)
    lo = (r1 - mid.astype(F32)).astype(BF16)
    return hi, mid, lo


def _vt_kernel(blk_ref, v_ref, o_ref):
    extra = VT_ROWS - HEAD_DIM
    ones_row = jnp.where(lax.broadcasted_iota(jnp.int32, (extra, HEAD_DIM), 0) == 0, 1.0, 0.0).astype(o_ref.dtype)
    for c in range(o_ref.shape[0]):
        o_ref[c, 0:HEAD_DIM, :] = v_ref[c * HEAD_DIM:(c + 1) * HEAD_DIM, :].astype(F32).T.astype(o_ref.dtype)
        o_ref[c, HEAD_DIM:VT_ROWS, :] = ones_row


def _transpose_values(h2, B, S, blocks, *, rows):
    ns = S // rows
    per = rows // HEAD_DIM
    nv = len(blocks)
    grid_spec = pltpu.PrefetchScalarGridSpec(
        num_scalar_prefetch=1,
        grid=(B, nv, ns),
        in_specs=[pl.BlockSpec((rows, HEAD_DIM), lambda b, i, s, blk: (b * ns + s, blk[i]))],
        out_specs=pl.BlockSpec((None, None, per, VT_ROWS, HEAD_DIM), lambda b, i, s, blk: (b, i, s, 0, 0)),
    )
    return pl.pallas_call(
        _vt_kernel,
        out_shape=jax.ShapeDtypeStruct((B, nv, S // HEAD_DIM, VT_ROWS, HEAD_DIM), BF16),
        grid_spec=grid_spec,
        compiler_params=_params("parallel", "parallel", "arbitrary"),
        name="transpose_values",
    )(jnp.asarray(blocks, jnp.int32), h2)


def _nsa_kernel(q_ref, kc_ref, vct_ref, ks_ref, vst_ref, kw_ref, vwt_ref, gl_ref, gate_ref, ovt_ref, blk_ref,
                o_ref, sig_ref, *, tq, tk, n_sel, n_sub):
    g = pl.program_id(1)
    hg = NSA_HEADS_PER_GROUP
    n_cmp = kc_ref.shape[0]
    nbp = NSA_BLK_PAD
    head = lambda h: slice(h * HEAD_DIM, (h + 1) * HEAD_DIM)
    lanes = lambda h: slice(h * tq, (h + 1) * tq)
    per_head = lambda a: jnp.concatenate([a] * hg, axis=1)
    chunk = HEAD_DIM
    n_pairs = hg // 2

    def softmax_rows(s, m_floor):
        m = jnp.maximum(jnp.max(s, axis=0, keepdims=True), m_floor)
        e = jnp.exp2(s - m)
        den = jnp.sum(e, axis=0, keepdims=True)
        return e, jnp.where(den > 0.0, 1.0 / den, 0.0)

    def prologue(u):
        qi = pl.program_id(2) * n_sub + u
        t0 = qi * tq
        rows = slice(u * tq, (u + 1) * tq)
        q4 = q_ref[rows, :]
        qs = jnp.concatenate([q4[:, head(h)] for h in range(hg)], axis=0)

        t_lane = t0 + lax.broadcasted_iota(jnp.int32, (n_cmp, tq), 1)
        cmp_end = lax.broadcasted_iota(jnp.int32, (n_cmp, tq), 0) * NSA_CMP_STRIDE + (NSA_CMP_LEN - 1)
        cbias = per_head(jnp.where(cmp_end <= t_lane, 0.0, NEG_INF))
        s = lax.dot_general(kc_ref[...], qs, NT_DIMS, preferred_element_type=F32) + cbias
        e, inv = softmax_rows(s, M_INIT)
        p = e * inv
        psum = sum(p[:, lanes(h)] for h in range(hg))
        o_ct = jnp.dot(vct_ref[...], p.astype(BF16), preferred_element_type=F32)

        ovt = ovt_ref[...]
        imp = sum(jnp.dot(ovt, part, preferred_element_type=F32) for part in _split3(psum))
        j_idx = lax.broadcasted_iota(jnp.int32, (nbp, tq), 0)
        cur = (t0 + lax.broadcasted_iota(jnp.int32, (nbp, tq), 1)) >> 6
        forced = (j_idx == 0) | (j_idx == cur) | (j_idx == cur - 1)
        imp = jnp.where(forced, imp + NSA_FORCE_BONUS, imp)
        imp = jnp.where(j_idx <= cur, imp, NEG_INF)

        band = NSA_WINDOW + tq
        wc = jnp.maximum(qi - NSA_WINDOW // tq, 0)
        w0 = pl.multiple_of(wc * tq, tq)
        kb = kw_ref[pl.ds(w0, band), :]
        vbt = jnp.concatenate([vwt_ref[wc + c, 0:HEAD_DIM, :] for c in range(band // chunk)], axis=1)
        rel = ((t0 + lax.broadcasted_iota(jnp.int32, (band, tq), 1))
               - (w0 + lax.broadcasted_iota(jnp.int32, (band, tq), 0)))
        wbias = per_head(jnp.where((rel >= 0) & (rel < NSA_WINDOW), 0.0, NEG_INF))
        s = lax.dot_general(kb, qs, NT_DIMS, preferred_element_type=F32) + wbias
        e, inv = softmax_rows(s, M_INIT)
        o_wt = jnp.dot(vbt, e.astype(BF16), preferred_element_type=F32) * inv

        j_f = j_idx.astype(F32)
        picked = jnp.zeros((nbp, tq), F32)
        for _ in range(n_sel):
            top = jnp.max(imp, axis=0, keepdims=True)
            first = jnp.min(jnp.where(imp == top, j_f, float(nbp)), axis=0, keepdims=True)
            hit = j_f == first
            picked = jnp.where(hit, 1.0, picked)
            imp = jnp.where(hit, TAKEN, imp)
        sel_t = jnp.where(j_idx <= cur, picked, 0.0)

        sel_bias = jnp.where(sel_t.T > 0.5, 0.0, NEG_INF).astype(BF16)
        sel_bias2 = jnp.concatenate([sel_bias, sel_bias], axis=0)
        q_aug = [jnp.concatenate([qs[pr * 2 * tq:(pr + 1) * 2 * tq], sel_bias2], axis=1) for pr in range(n_pairs)]
        return t0, q_aug, o_ct, o_wt

    def sweep(t0, q_aug):
        key_i = lax.broadcasted_iota(jnp.int32, (SEL_SUB, tq), 0)
        t_i = t0 + lax.broadcasted_iota(jnp.int32, (SEL_SUB, tq), 1)

        def sel_scores(kt, causal, pr):
            k0 = pl.multiple_of(kt * tk, tk)
            scores = []
            for sub in range(tk // SEL_SUB):
                ks0 = pl.multiple_of(k0 + sub * SEL_SUB, SEL_SUB)
                k_aug = jnp.concatenate([ks_ref[pl.ds(ks0, SEL_SUB), :], blk_ref[pl.ds(ks0, SEL_SUB), :]], axis=1)
                s = lax.dot_general(k_aug, q_aug[pr], NT_DIMS, preferred_element_type=F32)
                if causal:
                    cbias = jnp.where(key_i + ks0 <= t_i, 0.0, NEG_INF)
                    s = s + jnp.concatenate([cbias, cbias], axis=1)
                scores.append(s)
            return scores

        def sel_tiles(kt0, count, carry, diagonal_last=False):
            m, acc = list(carry[0]), list(carry[1])
            by_pair = [[sel_scores(kt0 + u, diagonal_last and u == count - 1, pr) for u in range(count)]
                       for pr in range(n_pairs)]
            scores = [[by_pair[pr][u][sub] for sub in range(tk // SEL_SUB) for pr in range(n_pairs)]
                      for u in range(count)]
            for u in range(count):
                for sub in range(tk // SEL_SUB):
                    c0 = (kt0 + u) * (tk // chunk) + sub * (SEL_SUB // chunk)
                    vt = jnp.concatenate([vst_ref[c0 + c] for c in range(SEL_SUB // chunk)], axis=1)
                    for pr in range(n_pairs):
                        s = scores[u][sub * n_pairs + pr]
                        m_new = jnp.maximum(m[pr], jnp.max(s, axis=0, keepdims=True))
                        p = jnp.exp2(s - m_new).astype(BF16)
                        acc[pr] = jnp.exp2(m[pr] - m_new) * acc[pr] + jnp.dot(vt, p, preferred_element_type=F32)
                        m[pr] = m_new
            return tuple(m), tuple(acc)

        n_tiles = (t0 + tq + tk - 1) // tk
        n_groups = (n_tiles - 1) // SEL_UNROLL
        n_last = n_tiles - n_groups * SEL_UNROLL
        carry = (tuple(jnp.full((1, 2 * tq), M_INIT, F32) for _ in range(n_pairs)),
                 tuple(jnp.zeros((VT_ROWS, 2 * tq), F32) for _ in range(n_pairs)))
        carry = lax.fori_loop(0, n_groups, lambda it, c: sel_tiles(it * SEL_UNROLL, SEL_UNROLL, c), carry)
        last = [functools.partial(sel_tiles, n_groups * SEL_UNROLL, count, diagonal_last=True)
                for count in range(1, SEL_UNROLL + 1)]
        _, acc_pairs = lax.switch(n_last - 1, last, carry)
        acc = jnp.concatenate(acc_pairs, axis=1)
        l_s = acc[HEAD_DIM:HEAD_DIM + 1, :]
        return acc[:HEAD_DIM, :] * jnp.where(l_s > 0.0, 1.0 / l_s, 0.0)

    def epilogue(u, o_ct, o_st, o_wt):
        rows = slice(u * tq, (u + 1) * tq)
        sig_ref[u] = (1.0 / (1.0 + jnp.exp(-gl_ref[rows, :]))).T
        gate = gate_ref[rows, :].astype(F32)
        for h in range(hg):
            base = (g * hg + h) * 3
            gc, gs, gw = (sig_ref[u, pl.ds(base + c, 1), :] for c in range(3))
            o_t = gc * o_ct[:, lanes(h)] + gs * o_st[:, lanes(h)] + gw * o_wt[:, lanes(h)]
            o_ref[rows, head(h)] = (o_t.T * _silu(gate[:, head(h)])).astype(o_ref.dtype)

    pro = [prologue(u) for u in range(n_sub)]
    swept = [sweep(t0, q_aug) for t0, q_aug, _, _ in pro]
    for u in range(n_sub):
        epilogue(u, pro[u][2], swept[u], pro[u][3])


def _nsa_attention(h2, small, kc, vct, vt, B, S, *, q_tile0, kv_block0, gate_tile0, tq, tk, n_sub):
    T = h2.shape[0]
    G, hg = NSA_KV_GROUPS, NSA_HEADS_PER_GROUP
    n_cmp = S // NSA_CMP_STRIDE
    n_blk = S // NSA_SEL_LEN
    nQ = S // (tq * n_sub)
    gw = hg * HEAD_DIM
    assert n_blk <= NSA_BLK_PAD and S % tk == 0 and S >= NSA_WINDOW + tq and tq == HEAD_DIM and S % (tq * n_sub) == 0
    assert tk % SEL_SUB == 0 and (tk // NSA_SEL_LEN) % 8 == 0 and hg % 2 == 0

    nn = np.arange(n_cmp)[None, :]
    jj = np.arange(NSA_BLK_PAD)[:, None]
    ovt = ((nn * NSA_CMP_STRIDE + NSA_CMP_LEN - 1 >= jj * NSA_SEL_LEN) & (nn * NSA_CMP_STRIDE < (jj + 1) * NSA_SEL_LEN)
           & (nn < n_cmp - 1) & (jj < n_blk))

    block_onehot = (np.arange(S)[:, None] // NSA_SEL_LEN) == np.arange(NSA_BLK_PAD)[None, :]

    def k_spec(c):
        return pl.BlockSpec((S, HEAD_DIM), lambda b, g, i: (b, kv_block0 + 4 * c + g))

    def vt_spec(which):
        return pl.BlockSpec((None, None, S // HEAD_DIM, VT_ROWS, HEAD_DIM), lambda b, g, i: (b, which * G + g, 0, 0, 0))

    rows = lambda width, col: pl.BlockSpec((tq * n_sub, width), lambda b, g, i: (b * nQ + i, col(g)))
    per_bg = lambda shape: pl.BlockSpec((None,) + shape, lambda b, g, i: (b * G + g, 0, 0))
    const = lambda shape: pl.BlockSpec(shape, lambda b, g, i: (0,) * len(shape))
    return pl.pallas_call(
        functools.partial(_nsa_kernel, tq=tq, tk=tk, n_sel=min(NSA_N_SEL, n_blk), n_sub=n_sub),
        out_shape=jax.ShapeDtypeStruct((T, G * gw), BF16),
        grid=(B, G, nQ),
        in_specs=[rows(gw, lambda g: q_tile0 + g), per_bg((n_cmp, HEAD_DIM)), per_bg((HEAD_DIM, n_cmp)),
                  k_spec(2), vt_spec(0), k_spec(4), vt_spec(1),
                  rows(HEAD_DIM, lambda g: 0), rows(gw, lambda g: gate_tile0 + g),
                  const((NSA_BLK_PAD, n_cmp)), const((S, NSA_BLK_PAD))],
        out_specs=rows(gw, lambda g: g),
        scratch_shapes=[pltpu.VMEM((n_sub, HEAD_DIM, tq), F32)],
        compiler_params=_params("parallel", "parallel", "arbitrary"),
        name="nsa_attention",
    )(h2, kc, vct, h2, vt, h2, vt, small, h2, jnp.asarray(ovt, BF16), jnp.asarray(block_onehot, BF16))


def _gla_kernel(q_ref, k_ref, v_ref, a_ref, gate_ref, wa_ref, ba_ref, ng_ref, o_ref, state_ref, intra_ref,
                *, n_chunks, n_heads):
    C = GLA_CHUNK

    @pl.when(pl.program_id(2) == 0)
    def _():
        state_ref[...] = jnp.zeros_like(state_ref)

    r_i = lax.broadcasted_iota(jnp.int32, (C, C), 0)
    c_i = lax.broadcasted_iota(jnp.int32, (C, C), 1)
    causal = r_i >= c_i
    tri = jnp.where(causal, 1.0, 0.0).astype(BF16)
    ng = ng_ref[...]
    heads = range(n_heads)
    kcols = lambda h: slice(h * GLA_DK, (h + 1) * GLA_DK)
    vcols = lambda h: slice(h * GLA_DV, (h + 1) * GLA_DV)

    qe_all, kd_all, decay_all = [], [], []
    for c in range(n_chunks):
        rows = slice(c * C, (c + 1) * C)
        a = a_ref[rows, :]
        pre = [jnp.dot(a, wa_ref[h], preferred_element_type=F32) + ba_ref[h] for h in heads]
        log_a = [(jnp.minimum(p, 0.0) - jnp.log(1.0 + jnp.exp(-jnp.abs(p)))) * (1.0 / GLA_TAU) for p in pre]
        bcum = [sum(jnp.dot(tri, part, preferred_element_type=F32) for part in _split3(la)) for la in log_a]
        b_last = [bc[C - 1:C, :] for bc in bcum]
        q = [q_ref[rows, kcols(h)].astype(F32) for h in heads]
        k = [k_ref[rows, kcols(h)].astype(F32) for h in heads]
        qe = [(q[h] * jnp.exp(bcum[h])).astype(BF16) for h in heads]
        ke = [(k[h] * jnp.exp(-bcum[h])).astype(BF16) for h in heads]
        kd = [(k[h] * jnp.exp(b_last[h] - bcum[h])).astype(BF16) for h in heads]
        attn = [lax.dot_general(qe[h], ke[h], NT_DIMS, preferred_element_type=F32) for h in heads]
        attn = [jnp.where(causal, at, 0.0).astype(BF16) for at in attn]
        for h in heads:
            intra_ref[rows, vcols(h)] = jnp.dot(attn[h], v_ref[rows, vcols(h)], preferred_element_type=F32)
        qe_all.append(qe)
        kd_all.append(kd)
        decay_all.append([jnp.exp(bl) for bl in b_last])

    for c in range(n_chunks):
        rows = slice(c * C, (c + 1) * C)
        state_t = [state_ref[h] for h in heads]
        inter = [lax.dot_general(qe_all[c][h], state_t[h].astype(BF16), NT_DIMS, preferred_element_type=F32)
                 for h in heads]
        upd = [lax.dot_general(v_ref[rows, vcols(h)], kd_all[c][h], TN_DIMS, preferred_element_type=F32) for h in heads]
        for h in heads:
            state_ref[h] = state_t[h] * decay_all[c][h] + upd[h]
        for h in heads:
            o = intra_ref[rows, vcols(h)] + inter[h]
            o = o * lax.rsqrt(jnp.mean(o * o, axis=-1, keepdims=True) + NORM_EPS) * ng
            o_ref[rows, vcols(h)] = (o * _silu(gate_ref[rows, vcols(h)].astype(F32))).astype(o_ref.dtype)


def _gla(h2, small, w_a2, b_a, norm_g, B, S, *, q_col0, k_col0, v_col0, gate_col0, rows, heads_per_step):
    T = h2.shape[0]
    H, hp = GLA_HEADS, heads_per_step
    nR = S // rows
    a_lo = 3 * NSA_KV_GROUPS * NSA_HEADS_PER_GROUP
    assert H % hp == 0 and all(c % (hp * w) == 0 for c, w in ((q_col0, GLA_DK), (k_col0, GLA_DK), (v_col0, GLA_DV),
                                                              (gate_col0, GLA_DV)))
    wa = jnp.zeros((H, HEAD_DIM, GLA_DK), F32).at[:, a_lo:a_lo + GLA_RANK, :].set(
        w_a2.reshape(GLA_RANK, H, GLA_DK).transpose(1, 0, 2))
    blk = lambda width, col0: pl.BlockSpec((rows, hp * width), lambda b, g, i: (b * nR + i, col0 // (hp * width) + g))
    per_head = lambda shape: pl.BlockSpec((hp,) + shape, lambda b, g, i: (g, 0, 0))
    return pl.pallas_call(
        functools.partial(_gla_kernel, n_chunks=rows // GLA_CHUNK, n_heads=hp),
        out_shape=jax.ShapeDtypeStruct((T, H * GLA_DV), BF16),
        grid=(B, H // hp, nR),
        in_specs=[blk(GLA_DK, q_col0), blk(GLA_DK, k_col0), blk(GLA_DV, v_col0),
                  pl.BlockSpec((rows, HEAD_DIM), lambda b, g, i: (b * nR + i, 0)), blk(GLA_DV, gate_col0),
                  per_head((HEAD_DIM, GLA_DK)), per_head((1, GLA_DK)),
                  pl.BlockSpec((1, GLA_DV), lambda b, g, i: (0, 0))],
        out_specs=blk(GLA_DV, 0),
        scratch_shapes=[pltpu.VMEM((hp, GLA_DV, GLA_DK), F32), pltpu.VMEM((rows, hp * GLA_DV), F32)],
        compiler_params=_params("parallel", "parallel", "arbitrary"),
        name="gla",
    )(h2, h2, h2, small, h2, wa, b_a.reshape(H, 1, GLA_DK), norm_g.reshape(1, GLA_DV))


def _dilated_scores(q, kp, kc):
    head = lambda h: slice(h * HEAD_DIM, (h + 1) * HEAD_DIM)
    return [lax.dot_general(q[:, head(h)], jnp.concatenate([kp[:, head(h)], kc[:, head(h)]], axis=0), NT_DIMS,
                            preferred_element_type=F32) for h in range(DIL_HEADS)]


def _dilated_finish(i, scores, vp, vc, prev, gate, o_scr, m_scr, d_scr):
    tq = o_scr.shape[0]
    head = lambda h: slice(h * HEAD_DIM, (h + 1) * HEAD_DIM)
    row = lax.broadcasted_iota(jnp.int32, (tq, 2 * tq), 0)
    col = lax.broadcasted_iota(jnp.int32, (tq, 2 * tq), 1)
    valid = (col >= row + (tq - DIL_BACK)) & (col <= row + tq) & ((col >= tq) | (i > 0))
    m_scr[...] = jnp.zeros(m_scr.shape, F32)
    d_scr[...] = jnp.ones(d_scr.shape, F32)
    for h in range(DIL_HEADS):
        v = jnp.concatenate([vp[:, head(h)], vc[:, head(h)]], axis=0)
        s = jnp.where(valid, scores[h], NEG_INF)
        m = jnp.max(s, axis=1, keepdims=True)
        e = jnp.exp(s - m)
        o_scr[:, head(h)] = jnp.dot(e.astype(BF16), v, preferred_element_type=F32)
        m_scr[:, h:h + 1] = m
        d_scr[:, h:h + 1] = jnp.sum(e, axis=1, keepdims=True)
    den = d_scr[...]
    lse = m_scr[...] + jnp.log(den)
    w_cur = 1.0 / den
    if prev is not None:
        prev_o, lse_p = prev
        lse_n = jnp.maximum(lse_p, lse) + jnp.log(1.0 + jnp.exp(-jnp.abs(lse_p - lse)))
        w_prev = jnp.exp(lse_p - lse_n)
        w_cur = jnp.exp(lse - lse_n) * w_cur
        lse = lse_n
    outs = []
    for h in range(DIL_HEADS):
        o = o_scr[:, head(h)] * w_cur[:, h:h + 1]
        if prev is not None:
            o = o + prev_o[:, head(h)].astype(F32) * w_prev[:, h:h + 1]
        if gate is not None:
            o = o * _silu(gate[:, head(h)].astype(F32))
        outs.append(o)
    return outs, lse


def _dilated_kernel(*refs, has_prev, is_last, n_sub):
    q_ref, kp_ref, kc_ref, vp_ref, vc_ref = refs[:5]
    rest = list(refs[5:])
    po_ref, pl_ref = (rest.pop(0), rest.pop(0)) if has_prev else (None, None)
    gate_ref = rest.pop(0) if is_last else None
    o_ref = rest.pop(0)
    lse_ref = None if is_last else rest.pop(0)
    o_scr, m_scr, d_scr = rest
    tq = o_scr.shape[1]
    tile = lambda ref, u: ref.at[pl.ds(u * tq, tq)]
    before = lambda prev_ref, cur_ref, u: prev_ref if u == 0 else tile(cur_ref, u - 1)
    scores = [_dilated_scores(tile(q_ref, u), before(kp_ref, kc_ref, u), tile(kc_ref, u)) for u in range(n_sub)]
    for u in range(n_sub):
        prev = (tile(po_ref, u), tile(pl_ref, u)[...]) if has_prev else None
        outs, lse = _dilated_finish(pl.program_id(2) * n_sub + u, scores[u], before(vp_ref, vc_ref, u),
                                    tile(vc_ref, u), prev, tile(gate_ref, u) if is_last else None,
                                    o_scr.at[u], m_scr.at[u], d_scr.at[u])
        for h, o in enumerate(outs):
            o_ref[pl.ds(u * tq, tq), h * HEAD_DIM:(h + 1) * HEAD_DIM] = o.astype(o_ref.dtype)
        if lse_ref is not None:
            lse_ref[pl.ds(u * tq, tq), :] = lse


def _dilated_scratch(tq, W, n=None):
    lead = () if n is None else (n,)
    return [pltpu.VMEM(lead + (tq, W), F32), pltpu.VMEM(lead + (tq, HEAD_DIM), F32),
            pltpu.VMEM(lead + (tq, HEAD_DIM), F32)]


def _dilated_group(hg, prev_o, prev_lse, gate, *, tq, n_sub=1):
    B, d, L, W3 = hg.shape
    W = W3 // 3
    has_prev, is_last = prev_o is not None, gate is not None
    n_sub = n_sub if L % (n_sub * tq) == 0 else 1
    ts = n_sub * tq
    assert tq == DIL_BACK and L % ts == 0
    cur = lambda col: pl.BlockSpec((None, None, ts, W), lambda b, r, i: (b, r, i, col))
    prv = lambda col: pl.BlockSpec((None, None, tq, W), lambda b, r, i: (b, r, jnp.maximum(n_sub * i - 1, 0), col))
    tok = lambda width: pl.BlockSpec((None, ts, width), lambda b, r, i: (b, i, r))
    in_specs = [cur(0), prv(1), cur(1), prv(2), cur(2)]
    args = [hg] * 5
    if has_prev:
        in_specs += [tok(W), tok(HEAD_DIM)]
        args += [prev_o, prev_lse]
    if is_last:
        in_specs += [tok(W)]
        args += [gate]
    out_shape = [jax.ShapeDtypeStruct((B, L, d * W), BF16)]
    out_specs = [tok(W)]
    if not is_last:
        out_shape += [jax.ShapeDtypeStruct((B, L, d * HEAD_DIM), F32)]
        out_specs += [tok(HEAD_DIM)]
    res = pl.pallas_call(
        functools.partial(_dilated_kernel, has_prev=has_prev, is_last=is_last, n_sub=n_sub),
        out_shape=tuple(out_shape),
        grid=(B, d, L // ts),
        in_specs=in_specs,
        out_specs=tuple(out_specs),
        scratch_shapes=_dilated_scratch(tq, W, n_sub),
        compiler_params=_params("parallel", "parallel", "arbitrary"),
        name=f"dilated_d{d}",
    )(*args)
    return res[0] if is_last else (res[0], res[1])


def _regroup_kernel(*refs, d, ratio):
    q_ref, kp_ref, kc_ref, vp_ref, vc_ref = refs[:5]
    po_refs = refs[5:5 + ratio]
    pl_refs = refs[5 + ratio:5 + 2 * ratio]
    o_ref, lse_ref, o_scr, m_scr, d_scr, so_scr, sl_scr = refs[5 + 2 * ratio:]
    i = pl.program_id(1)
    r = pl.program_id(2)
    tq, W = o_scr.shape
    rows_c = tq // ratio
    n = d * tq
    shift = lambda v: v.bit_length() - 1
    dst = lax.broadcasted_iota(jnp.int32, (tq, tq), 0)
    src = lax.broadcasted_iota(jnp.int32, (tq, tq), 1)
    gather = jnp.where(src == (dst & (ratio - 1)) * rows_c + (dst >> shift(ratio)), 1.0, 0.0).astype(BF16)
    prev_o = jnp.dot(gather, jnp.concatenate([p[...] for p in po_refs], axis=0), preferred_element_type=F32)
    prev_l = sum(jnp.dot(gather, part, preferred_element_type=F32)
                 for part in _split3(jnp.concatenate([p[...] for p in pl_refs], axis=0)))
    outs, lse = _dilated_finish(i, _dilated_scores(q_ref, kp_ref, kc_ref), vp_ref, vc_ref, (prev_o, prev_l), None,
                                o_scr, m_scr, d_scr)
    so_scr[r] = jnp.concatenate(outs, axis=1).astype(BF16)
    sl_scr[r] = lse

    @pl.when(r == d - 1)
    def _():
        dst = lax.broadcasted_iota(jnp.int32, (n, n), 0)
        src = lax.broadcasted_iota(jnp.int32, (n, n), 1)
        scatter = jnp.where(src == (dst & (d - 1)) * tq + (dst >> shift(d)), 1.0, 0.0).astype(BF16)
        o_ref[...] = jnp.dot(scatter, so_scr[...].reshape(n, W), preferred_element_type=F32).astype(o_ref.dtype)
        lse_ref[...] = sum(jnp.dot(scatter, part, preferred_element_type=F32)
                           for part in _split3(sl_scr[...].reshape(n, HEAD_DIM)))


def _dilated_regroup(hg, prev_o, prev_lse, *, fine, tq):
    B, d, L, W3 = hg.shape
    W = W3 // 3
    S = L * d
    ratio = fine // d
    assert tq == DIL_BACK and L % tq == 0 and fine % d == 0 and tq % ratio == 0
    assert d & (d - 1) == 0 and ratio & (ratio - 1) == 0
    rows_c = tq // ratio
    cur = lambda col: pl.BlockSpec((None, None, tq, W), lambda b, i, r: (b, r, i, col))
    prv = lambda col: pl.BlockSpec((None, None, tq, W), lambda b, i, r: (b, r, jnp.maximum(i - 1, 0), col))
    piece = lambda width: [pl.BlockSpec((None, rows_c, width), lambda b, i, r, j=j: (b, i, j * d + r))
                           for j in range(ratio)]
    return pl.pallas_call(
        functools.partial(_regroup_kernel, d=d, ratio=ratio),
        out_shape=(jax.ShapeDtypeStruct((B, S, W), BF16), jax.ShapeDtypeStruct((B, S, HEAD_DIM), F32)),
        grid=(B, L // tq, d),
        in_specs=[cur(0), prv(1), cur(1), prv(2), cur(2)] + piece(W) + piece(HEAD_DIM),
        out_specs=(pl.BlockSpec((None, d * tq, W), lambda b, i, r: (b, i, 0)),
                   pl.BlockSpec((None, d * tq, HEAD_DIM), lambda b, i, r: (b, i, 0))),
        scratch_shapes=_dilated_scratch(tq, W) + [pltpu.VMEM((d, tq, W), BF16), pltpu.VMEM((d, tq, HEAD_DIM), F32)],
        compiler_params=_params("parallel", "arbitrary", "arbitrary"),
        name=f"dilated_regroup_d{d}",
    )(*([hg] * 5 + [prev_o] * ratio + [prev_lse] * ratio))


def _row_tile(n, target):
    t = min(n, target)
    assert n % t == 0
    return t


def _even_layer(xb4, w_in, pe_k, w1_k, w2_k, pe_v, w1_v, w2_v, gla_w_a2, gla_b_a, gla_norm_g, w_out):
    B, _, S, D = xb4.shape
    T = B * S
    nq = NSA_KV_GROUPS * NSA_HEADS_PER_GROUP * HEAD_DIM
    nkv = 6 * NSA_KV_GROUPS * HEAD_DIM
    ng = 3 * NSA_KV_GROUPS * NSA_HEADS_PER_GROUP
    gk = GLA_HEADS * GLA_DK
    gv = GLA_HEADS * GLA_DV
    o_g = nq + nkv
    o_a = o_g + ng + nq + 2 * gk + gv
    w16 = w_in.astype(BF16)
    w_main = jnp.concatenate([w16[:, :o_g], w16[:, o_g + ng:o_a], w16[:, o_a + GLA_RANK:]], axis=1)
    w_small = jnp.concatenate([w16[:, o_g:o_g + ng], w16[:, o_a:o_a + GLA_RANK],
                               jnp.zeros((D, HEAD_DIM - ng - GLA_RANK), BF16)], axis=1)
    tn = 512
    plain, rope = (False, 1.0), (True, 1.0)
    groups = [("nsa_q", [(True, Q_SCALE * LOG2E)] * (nq // tn)),
              ("nsa_kv", [plain, plain, rope, plain, rope, plain]),
              ("nsa_gate", [plain] * (nq // tn)), ("gla_q", [(False, GLA_Q_SCALE)] * (gk // tn)),
              ("gla_k", [plain] * (gk // tn)), ("gla_v", [plain] * (gv // tn)), ("gla_gate", [plain] * (gv // tn))]
    w_block, n_seen = {}, 0
    for name, kinds in groups:
        w_block[name] = n_seen
        n_seen += len(kinds)
    order = ["nsa_q", "gla_v", "gla_gate", "nsa_gate", "gla_q", "gla_k", "nsa_kv"]
    tiles, col = [], {}
    for name in order:
        col[name] = len(tiles) * tn
        tiles += [(w_block[name] + t,) + kind for t, kind in enumerate(dict(groups)[name])]
    pos = jnp.arange(S)[None]
    tm = _row_tile(S, 1024)
    h2 = _project(xb4, w_main, tiles, pos, tm=tm, tn=tn, out_dtype=BF16).reshape(T, -1)
    small = _project(xb4, w_small, [(0,) + plain], pos, tm=tm, tn=HEAD_DIM, out_dtype=F32).reshape(T, HEAD_DIM)

    kv_block0 = col["nsa_kv"] // HEAD_DIM
    kc, vct = _nsa_compress(h2, B, S, pe_k, w1_k, w2_k, pe_v, w1_v, w2_v, kv_block0, kv_block0 + NSA_KV_GROUPS)
    v_blocks = [kv_block0 + 4 * c + g for c in (3, 5) for g in range(NSA_KV_GROUPS)]
    vt = _transpose_values(h2, B, S, v_blocks, rows=_row_tile(S, 1024))
    o_nsa = _nsa_attention(h2, small, kc, vct, vt, B, S, q_tile0=col["nsa_q"] // tn, kv_block0=kv_block0,
                           gate_tile0=col["nsa_gate"] // tn, tq=128, tk=_row_tile(S, 512), n_sub=4)
    o_gla = _gla(h2, small, gla_w_a2, gla_b_a, gla_norm_g, B, S, q_col0=col["gla_q"], k_col0=col["gla_k"],
                 v_col0=col["gla_v"], gate_col0=col["gla_gate"], rows=_row_tile(S, 512), heads_per_step=GLA_HEADS)
    return _out_project([o_nsa, o_gla], w_out, tm=_row_tile(T, 1024), tn=512)


def _odd_layer(streams, w_in, w_out):
    B, _, S, D = streams[1].shape
    T = B * S
    n_g = len(DIL_GROUPS)
    W = DIL_HEADS * HEAD_DIM
    tn = 512
    blocks = lambda col0: [col0 // tn + t for t in range(W // tn)]
    gate = _project(streams[1], w_in, [(blk, False, 1.0) for blk in blocks(3 * n_g * W)], jnp.arange(S)[None],
                    tm=_row_tile(S, 1024), tn=tn, out_dtype=BF16).reshape(B, S, W)

    def group_qkv(gi, d):
        L = S // d
        tiles = ([(blk, True, Q_SCALE) for blk in blocks(gi * W)] + [(blk, True, 1.0) for blk in blocks((n_g + gi) * W)]
                 + [(blk, False, 1.0) for blk in blocks((2 * n_g + gi) * W)])
        tm = _row_tile(L, 1024)
        return _project(streams[d], w_in, tiles, jnp.arange(S).reshape(L, d).T, tm=tm, tn=tn, out_dtype=BF16,
                        streams_per_step=min(d, 1024 // tm) if tm == L else 1, n_split=max(1, tm // 256))

    order = sorted(range(n_g), key=lambda gi: -DIL_GROUPS[gi][1])
    assert len(order) == 3 and DIL_GROUPS[order[-1]][1] == 1
    for gi in order:
        assert DIL_GROUPS[gi][0] // DIL_GROUPS[gi][1] == DIL_BACK
    d_hi, d_mid = DIL_GROUPS[order[0]][1], DIL_GROUPS[order[1]][1]
    o, lse = _dilated_group(group_qkv(order[0], d_hi), None, None, None, tq=DIL_BACK)
    o, lse = _dilated_regroup(group_qkv(order[1], d_mid), o, lse, fine=d_hi, tq=DIL_BACK)
    o = _dilated_group(group_qkv(order[2], 1), o, lse, gate, tq=DIL_BACK)
    return _out_project([o.reshape(T, W)], w_out, tm=_row_tile(T, 1024), tn=512)


def kernel(x, l0_w_in, l0_nsa_pe_k, l0_nsa_w1_k, l0_nsa_w2_k, l0_nsa_pe_v, l0_nsa_w1_v, l0_nsa_w2_v, l0_gla_w_a2, l0_gla_b_a, l0_gla_norm_g, l0_w_out, l0_ln_g, l0_ln_b, l1_w_in, l1_w_out, l1_ln_g, l1_ln_b):
    B, S, D = x.shape
    tr = _row_tile(S, 256)
    dils = sorted({d for _, d in DIL_GROUPS} | {1})
    y0 = _even_layer(x.astype(BF16)[:, None], l0_w_in, l0_nsa_pe_k, l0_nsa_w1_k, l0_nsa_w2_k, l0_nsa_pe_v, l0_nsa_w1_v,
                     l0_nsa_w2_v, l0_gla_w_a2, l0_gla_b_a, l0_gla_norm_g, l0_w_out)
    x1, *x1_streams = _deepnorm_ln(x, y0, l0_ln_g, l0_ln_b, tr=tr, dilations=dils)
    y1 = _odd_layer(dict(zip(dils, x1_streams)), l1_w_in, l1_w_out)
    return _deepnorm_ln(x1, y1, l1_ln_g, l1_ln_b, tr=tr)[0]
```

```python
import functools

import numpy as np
import jax
import jax.numpy as jnp
from jax import lax
from jax.experimental import pallas as pl
from jax.experimental.pallas import tpu as pltpu

F32 = jnp.float32
BF16 = jnp.bfloat16

HEAD_DIM = 128
ROPE_THETA = 10000.0
LN_EPS = 1e-5
NORM_EPS = 1e-6
NEG_INF = -1e30
M_INIT = -1e29
TAKEN = -3e38
SEL_UNROLL = 4
SEL_SUB = 256
VT_ROWS = HEAD_DIM + 16

NSA_KV_GROUPS = 4
NSA_HEADS_PER_GROUP = 4
NSA_CMP_LEN = 32
NSA_CMP_STRIDE = 16
NSA_SEL_LEN = 64
NSA_N_SEL = 16
NSA_WINDOW = 512
NSA_FORCE_BONUS = 1e4
NSA_BLK_PAD = 128

GLA_HEADS = 4
GLA_DK = 256
GLA_DV = 512
GLA_RANK = 16
GLA_TAU = 16.0
GLA_CHUNK = 64

DIL_GROUPS = ((128, 1), (512, 4), (2048, 16))
DIL_HEADS = 16
DIL_BACK = 128

DEPTH = 2
DEEPNORM_ALPHA = (2.0 * DEPTH) ** 0.25

VMEM_LIMIT_BYTES = 56 * 1024 * 1024

Q_SCALE = HEAD_DIM ** -0.5
LOG2E = 1.4426950408889634
GLA_Q_SCALE = GLA_DK ** -0.5

NT_DIMS = (((1,), (1,)), ((), ()))
TN_DIMS = (((0,), (0,)), ((), ()))


def _params(*sem):
    return pltpu.CompilerParams(dimension_semantics=sem, vmem_limit_bytes=VMEM_LIMIT_BYTES)


def _silu(x):
    return x * (1.0 / (1.0 + jnp.exp(-x)))


def _rope_tables(pos):
    half = HEAD_DIM // 2
    inv_freq = ROPE_THETA ** (-jnp.arange(half, dtype=F32) / half)
    ang = pos.astype(F32)[:, None] * inv_freq[None, :]
    cos = jnp.cos(ang)
    sin = jnp.sin(ang)
    return jnp.concatenate([cos, cos], axis=-1), jnp.concatenate([-sin, sin], axis=-1)


def _rope(a, cos, sin):
    return a * cos + pltpu.roll(a, HEAD_DIM // 2, 1) * sin


def _proj_kernel(wblk_ref, tbase_ref, tstep_ref, a_ref, w_ref, c_ref, s_ref, o_ref, *, n_split):
    w = w_ref[...].astype(BF16)
    sb, tm, _ = a_ref.shape
    chunk = tm // n_split
    for st in range(sb):
        for h in range(n_split):
            rows = slice(h * chunk, (h + 1) * chunk)
            trows = slice(st * tm + h * chunk, st * tm + (h + 1) * chunk)
            acc = jnp.dot(a_ref[st, rows, :], w, preferred_element_type=F32)
            c = c_ref[trows, :]
            s = s_ref[trows, :]
            for k in range(acc.shape[1] // HEAD_DIM):
                sl = slice(k * HEAD_DIM, (k + 1) * HEAD_DIM)
                o_ref[st, rows, sl] = (acc[:, sl] * c + pltpu.roll(acc[:, sl], HEAD_DIM // 2, 1) * s).astype(o_ref.dtype)


def _project(act4, w, tiles, pos, *, tm, tn, out_dtype, streams_per_step=1, n_split=4):
    B, d, L, K = act4.shape
    sb = streams_per_step
    nI, nJ, nR = L // tm, len(tiles), d // sb
    assert L % tm == 0 and pos.shape == (d, L) and d % sb == 0 and (sb == 1 or nI == 1) and tm % n_split == 0
    tr = sb * tm
    kinds = sorted({(rope, scale) for _, rope, scale in tiles})
    cos, sin = _rope_tables(pos.reshape(-1))
    c_rows, s_rows, base = [], [], {}
    n_blocks = 0
    for rope, scale in kinds:
        base[(rope, scale)] = n_blocks
        if rope:
            c_rows += [cos * scale]
            s_rows += [sin * scale]
            n_blocks += nR * nI
        else:
            c_rows += [jnp.full((tr, HEAD_DIM), scale, F32)]
            s_rows += [jnp.zeros((tr, HEAD_DIM), F32)]
            n_blocks += 1
    c_tab, s_tab = jnp.concatenate(c_rows, axis=0), jnp.concatenate(s_rows, axis=0)
    wblk = jnp.asarray([t[0] for t in tiles], jnp.int32)
    tbase = jnp.asarray([base[(t[1], t[2])] for t in tiles], jnp.int32)
    tstep = jnp.asarray([int(t[1]) for t in tiles], jnp.int32)

    def split(m):
        return m // (nR * nI), (m // nI) % nR, m % nI

    def a_idx(m, j, wb, tb, ts):
        b, rg, i = split(m)
        return (b, rg, i, 0)

    def t_idx(m, j, wb, tb, ts):
        b, rg, i = split(m)
        return (tb[j] + ts[j] * (rg * nI + i), 0)

    def o_idx(m, j, wb, tb, ts):
        b, rg, i = split(m)
        return (b, rg, i, j)

    grid_spec = pltpu.PrefetchScalarGridSpec(
        num_scalar_prefetch=3,
        grid=(B * nR * nI, nJ),
        in_specs=[
            pl.BlockSpec((None, sb, tm, K), a_idx),
            pl.BlockSpec((K, tn), lambda m, j, wb, tb, ts: (0, wb[j])),
            pl.BlockSpec((tr, HEAD_DIM), t_idx),
            pl.BlockSpec((tr, HEAD_DIM), t_idx),
        ],
        out_specs=pl.BlockSpec((None, sb, tm, tn), o_idx),
    )
    return pl.pallas_call(
        functools.partial(_proj_kernel, n_split=n_split),
        out_shape=jax.ShapeDtypeStruct((B, d, L, nJ * tn), out_dtype),
        grid_spec=grid_spec,
        compiler_params=_params("parallel", "arbitrary"),
        name=f"proj_d{d}_n{nJ * tn}",
    )(wblk, tbase, tstep, act4, w, c_tab, s_tab)


def _outproj_kernel(*refs, n_split):
    o_ref = refs[-1]
    ws = [w_ref[...].astype(BF16) for w_ref in refs[1:-1:2]]
    chunk = o_ref.shape[0] // n_split
    for h in range(n_split):
        rows = slice(h * chunk, (h + 1) * chunk)
        acc = None
        for a_ref, w in zip(refs[0:-1:2], ws):
            part = jnp.dot(a_ref[rows, :], w, preferred_element_type=F32)
            acc = part if acc is None else acc + part
        o_ref[rows, :] = acc.astype(o_ref.dtype)


def _out_project(acts, w, *, tm, tn, n_split=4):
    T, K = acts[0].shape
    N = w.shape[1]
    assert all(a.shape == (T, K) for a in acts) and w.shape[0] == len(acts) * K
    in_specs, args = [], []
    for n, a in enumerate(acts):
        in_specs += [pl.BlockSpec((tm, K), lambda i, j: (i, 0)), pl.BlockSpec((K, tn), lambda i, j, n=n: (n, j))]
        args += [a, w]
    return pl.pallas_call(
        functools.partial(_outproj_kernel, n_split=n_split),
        out_shape=jax.ShapeDtypeStruct((T, N), BF16),
        grid=(T // tm, N // tn),
        in_specs=in_specs,
        out_specs=pl.BlockSpec((tm, tn), lambda i, j: (i, j)),
        compiler_params=_params("parallel", "arbitrary"),
        name=f"outproj_{len(acts)}",
    )(*args)


def _ln_kernel(x_ref, y_ref, g_ref, b_ref, o_ref, *stream_refs, dilations):
    z = DEEPNORM_ALPHA * x_ref[...] + y_ref[...].astype(F32)
    mu = jnp.mean(z, axis=-1, keepdims=True)
    zc = z - mu
    var = jnp.mean(zc * zc, axis=-1, keepdims=True)
    out = zc * lax.rsqrt(var + LN_EPS) * g_ref[...] + b_ref[...]
    o_ref[...] = out
    tr = out.shape[0]
    ob = out.astype(BF16)
    dst = lax.broadcasted_iota(jnp.int32, (tr, tr), 0)
    src = lax.broadcasted_iota(jnp.int32, (tr, tr), 1)
    for d, s_ref in zip(dilations, stream_refs):
        if d == 1:
            s_ref[0] = ob
            continue
        per = tr // d
        perm = jnp.where(src == (dst & (per - 1)) * d + (dst >> (per.bit_length() - 1)), 1.0, 0.0).astype(BF16)
        sm = jnp.dot(perm, ob, preferred_element_type=F32).astype(BF16)
        for r in range(d):
            s_ref[r] = sm[r * per:(r + 1) * per]


def _deepnorm_ln(x3, y2, g, b, *, tr, dilations=()):
    B, S, D = x3.shape
    nI = S // tr
    assert S % tr == 0 and all(tr % (d * 16) == 0 and d & (d - 1) == 0 for d in dilations) and tr & (tr - 1) == 0
    row3 = pl.BlockSpec((None, tr, D), lambda b, i: (b, i, 0))
    vec = pl.BlockSpec((1, D), lambda b, i: (0, 0))
    out_shape = [jax.ShapeDtypeStruct((B, S, D), F32)]
    out_specs = [row3]
    for d in dilations:
        out_shape.append(jax.ShapeDtypeStruct((B, d, S // d, D), BF16))
        out_specs.append(pl.BlockSpec((None, d, tr // d, D), lambda b, i: (b, 0, i, 0)))
    return pl.pallas_call(
        functools.partial(_ln_kernel, dilations=tuple(dilations)),
        out_shape=tuple(out_shape),
        grid=(B, nI),
        in_specs=[row3, pl.BlockSpec((tr, D), lambda b, i: (b * nI + i, 0)), vec, vec],
        out_specs=tuple(out_specs),
        compiler_params=_params("parallel", "parallel"),
        name="deepnorm_ln",
    )(x3, y2, g.reshape(1, D), b.reshape(1, D))


def _compress_kernel(zk_ref, zv_ref, pe_k_ref, pe_v_ref, w1ka_ref, w1kb_ref, w1va_ref, w1vb_ref,
                     w2k_ref, w2vt_ref, cos_ref, sin_ref, kc_ref, vct_ref, uk_ref, vk_ref, uv_ref, vv_ref):
    p = pl.program_id(1)
    n = uk_ref.shape[0]

    @pl.when(p == 0)
    def _():
        for ref in (uk_ref, vk_ref, uv_ref, vv_ref):
            ref[...] = jnp.zeros_like(ref)

    def accumulate(z_ref, pe_ref, wa_ref, wb_ref, u_ref, v_ref):
        z = z_ref[...].astype(F32)
        za = (z + pe_ref[pl.ds(p, 1), :]).astype(BF16)
        zb = (z + pe_ref[pl.ds(p + NSA_CMP_STRIDE, 1), :]).astype(BF16)
        u_ref[...] += jnp.dot(za, wa_ref[...], preferred_element_type=F32)
        v_ref[...] += jnp.dot(zb, wb_ref[...], preferred_element_type=F32)

    accumulate(zk_ref, pe_k_ref, w1ka_ref, w1kb_ref, uk_ref, vk_ref)
    accumulate(zv_ref, pe_v_ref, w1va_ref, w1vb_ref, uv_ref, vv_ref)

    @pl.when(p == NSA_CMP_STRIDE - 1)
    def _():
        hid_k = _silu(uk_ref[...] + pltpu.roll(vk_ref[...], n - 1, 0)).astype(BF16)
        hid_v = _silu(uv_ref[...] + pltpu.roll(vv_ref[...], n - 1, 0)).astype(BF16)
        kc = jnp.dot(hid_k, w2k_ref[...], preferred_element_type=F32)
        kc = _rope(kc, cos_ref[...], sin_ref[...])
        row = lax.broadcasted_iota(jnp.int32, kc.shape, 0)
        kc_ref[...] = jnp.where(row < n - 1, kc, 0.0).astype(kc_ref.dtype)
        vct = lax.dot_general(w2vt_ref[...], hid_v, NT_DIMS, preferred_element_type=F32)
        col = lax.broadcasted_iota(jnp.int32, vct.shape, 1)
        vct_ref[...] = jnp.where(col < n - 1, vct, 0.0).astype(vct_ref.dtype)


def _nsa_compress(h2, B, S, pe_k, w1_k, w2_k, pe_v, w1_v, w2_v, kc_block, vc_block):
    G = NSA_KV_GROUPS
    n = S // NSA_CMP_STRIDE
    c0, c1 = kc_block * HEAD_DIM, (vc_block + G) * HEAD_DIM
    nb = (c1 - c0) // HEAD_DIM
    h3 = h2[:, c0:c1].reshape(B, n, NSA_CMP_STRIDE * (c1 - c0))
    kc_block, vc_block = 0, vc_block - kc_block
    cos, sin = _rope_tables(jnp.arange(n) * NSA_CMP_STRIDE + NSA_CMP_LEN - 1)

    def z_spec(block0):
        return pl.BlockSpec((None, n, HEAD_DIM), lambda bg, p: (bg // G, 0, p * nb + block0 + bg % G))

    full = lambda shape: pl.BlockSpec(shape, lambda bg, p: (0,) * len(shape))
    w1a = pl.BlockSpec((HEAD_DIM, HEAD_DIM), lambda bg, p: (p, 0))
    w1b = pl.BlockSpec((HEAD_DIM, HEAD_DIM), lambda bg, p: (p + NSA_CMP_STRIDE, 0))
    acc = pltpu.VMEM((n, HEAD_DIM), F32)
    return pl.pallas_call(
        _compress_kernel,
        out_shape=(jax.ShapeDtypeStruct((B * G, n, HEAD_DIM), BF16), jax.ShapeDtypeStruct((B * G, HEAD_DIM, n), BF16)),
        grid=(B * G, NSA_CMP_STRIDE),
        in_specs=[z_spec(kc_block), z_spec(vc_block), full((NSA_CMP_LEN, HEAD_DIM)), full((NSA_CMP_LEN, HEAD_DIM)),
                  w1a, w1b, w1a, w1b, full((HEAD_DIM, HEAD_DIM)), full((HEAD_DIM, HEAD_DIM)),
                  full((n, HEAD_DIM)), full((n, HEAD_DIM))],
        out_specs=(pl.BlockSpec((None, n, HEAD_DIM), lambda bg, p: (bg, 0, 0)),
                   pl.BlockSpec((None, HEAD_DIM, n), lambda bg, p: (bg, 0, 0))),
        scratch_shapes=[acc, acc, acc, acc],
        compiler_params=_params("parallel", "arbitrary"),
        name="nsa_compress",
    )(h3, h3, pe_k, pe_v, w1_k.astype(BF16), w1_k.astype(BF16), w1_v.astype(BF16), w1_v.astype(BF16),
      w2_k.astype(BF16), w2_v.T.astype(BF16), cos, sin)


def _split3(x):
    hi = x.astype(BF16)
    r1 = x - hi.astype(F32)
    mid = r1.astype(BF16)
    lo = (r1 - mid.astype(F32)).astype(BF16)
    return hi, mid, lo


def _vt_kernel(blk_ref, v_ref, o_ref):
    extra = VT_ROWS - HEAD_DIM
    ones_row = jnp.where(lax.broadcasted_iota(jnp.int32, (extra, HEAD_DIM), 0) == 0, 1.0, 0.0).astype(o_ref.dtype)
    for c in range(o_ref.shape[0]):
        o_ref[c, 0:HEAD_DIM, :] = v_ref[c * HEAD_DIM:(c + 1) * HEAD_DIM, :].astype(F32).T.astype(o_ref.dtype)
        o_ref[c, HEAD_DIM:VT_ROWS, :] = ones_row


def _transpose_values(h2, B, S, blocks, *, rows):
    ns = S // rows
    per = rows // HEAD_DIM
    nv = len(blocks)
    grid_spec = pltpu.PrefetchScalarGridSpec(
        num_scalar_prefetch=1,
        grid=(B, nv, ns),
        in_specs=[pl.BlockSpec((rows, HEAD_DIM), lambda b, i, s, blk: (b * ns + s, blk[i]))],
        out_specs=pl.BlockSpec((None, None, per, VT_ROWS, HEAD_DIM), lambda b, i, s, blk: (b, i, s, 0, 0)),
    )
    return pl.pallas_call(
        _vt_kernel,
        out_shape=jax.ShapeDtypeStruct((B, nv, S // HEAD_DIM, VT_ROWS, HEAD_DIM), BF16),
        grid_spec=grid_spec,
        compiler_params=_params("parallel", "parallel", "arbitrary"),
        name="transpose_values",
    )(jnp.asarray(blocks, jnp.int32), h2)


def _nsa_kernel(q_ref, kc_ref, vct_ref, ks_ref, vst_ref, kw_ref, vwt_ref, gl_ref, gate_ref, ovt_ref, blk_ref,
                o_ref, sig_ref, *, tq, tk, n_sel, n_sub):
    g = pl.program_id(1)
    hg = NSA_HEADS_PER_GROUP
    n_cmp = kc_ref.shape[0]
    nbp = NSA_BLK_PAD
    head = lambda h: slice(h * HEAD_DIM, (h + 1) * HEAD_DIM)
    lanes = lambda h: slice(h * tq, (h + 1) * tq)
    per_head = lambda a: jnp.concatenate([a] * hg, axis=1)
    chunk = HEAD_DIM
    n_pairs = hg // 2

    def softmax_rows(s, m_floor):
        m = jnp.maximum(jnp.max(s, axis=0, keepdims=True), m_floor)
        e = jnp.exp2(s - m)
        den = jnp.sum(e, axis=0, keepdims=True)
        return e, jnp.where(den > 0.0, 1.0 / den, 0.0)

    def prologue(u):
        qi = pl.program_id(2) * n_sub + u
        t0 = qi * tq
        rows = slice(u * tq, (u + 1) * tq)
        q4 = q_ref[rows, :]
        qs = jnp.concatenate([q4[:, head(h)] for h in range(hg)], axis=0)

        t_lane = t0 + lax.broadcasted_iota(jnp.int32, (n_cmp, tq), 1)
        cmp_end = lax.broadcasted_iota(jnp.int32, (n_cmp, tq), 0) * NSA_CMP_STRIDE + (NSA_CMP_LEN - 1)
        cbias = per_head(jnp.where(cmp_end <= t_lane, 0.0, NEG_INF))
        s = lax.dot_general(kc_ref[...], qs, NT_DIMS, preferred_element_type=F32) + cbias
        e, inv = softmax_rows(s, M_INIT)
        p = e * inv
        psum = sum(p[:, lanes(h)] for h in range(hg))
        o_ct = jnp.dot(vct_ref[...], p.astype(BF16), preferred_element_type=F32)

        ovt = ovt_ref[...]
        imp = sum(jnp.dot(ovt, part, preferred_element_type=F32) for part in _split3(psum))
        j_idx = lax.broadcasted_iota(jnp.int32, (nbp, tq), 0)
        cur = (t0 + lax.broadcasted_iota(jnp.int32, (nbp, tq), 1)) >> 6
        forced = (j_idx == 0) | (j_idx == cur) | (j_idx == cur - 1)
        imp = jnp.where(forced, imp + NSA_FORCE_BONUS, imp)
        imp = jnp.where(j_idx <= cur, imp, NEG_INF)

        band = NSA_WINDOW + tq
        wc = jnp.maximum(qi - NSA_WINDOW // tq, 0)
        w0 = pl.multiple_of(wc * tq, tq)
        kb = kw_ref[pl.ds(w0, band), :]
        vbt = jnp.concatenate([vwt_ref[wc + c, 0:HEAD_DIM, :] for c in range(band // chunk)], axis=1)
        rel = ((t0 + lax.broadcasted_iota(jnp.int32, (band, tq), 1))
               - (w0 + lax.broadcasted_iota(jnp.int32, (band, tq), 0)))
        wbias = per_head(jnp.where((rel >= 0) & (rel < NSA_WINDOW), 0.0, NEG_INF))
        s = lax.dot_general(kb, qs, NT_DIMS, preferred_element_type=F32) + wbias
        e, inv = softmax_rows(s, M_INIT)
        o_wt = jnp.dot(vbt, e.astype(BF16), preferred_element_type=F32) * inv

        j_f = j_idx.astype(F32)
        picked = jnp.zeros((nbp, tq), F32)
        for _ in range(n_sel):
            top = jnp.max(imp, axis=0, keepdims=True)
            first = jnp.min(jnp.where(imp == top, j_f, float(nbp)), axis=0, keepdims=True)
            hit = j_f == first
            picked = jnp.where(hit, 1.0, picked)
            imp = jnp.where(hit, TAKEN, imp)
        sel_t = jnp.where(j_idx <= cur, picked, 0.0)

        sel_bias = jnp.where(sel_t.T > 0.5, 0.0, NEG_INF).astype(BF16)
        sel_bias2 = jnp.concatenate([sel_bias, sel_bias], axis=0)
        q_aug = [jnp.concatenate([qs[pr * 2 * tq:(pr + 1) * 2 * tq], sel_bias2], axis=1) for pr in range(n_pairs)]
        return t0, q_aug, o_ct, o_wt

    def sweep(t0, q_aug):
        key_i = lax.broadcasted_iota(jnp.int32, (SEL_SUB, tq), 0)
        t_i = t0 + lax.broadcasted_iota(jnp.int32, (SEL_SUB, tq), 1)

        def sel_scores(kt, causal, pr):
            k0 = pl.multiple_of(kt * tk, tk)
            scores = []
            for sub in range(tk // SEL_SUB):
                ks0 = pl.multiple_of(k0 + sub * SEL_SUB, SEL_SUB)
                k_aug = jnp.concatenate([ks_ref[pl.ds(ks0, SEL_SUB), :], blk_ref[pl.ds(ks0, SEL_SUB), :]], axis=1)
                s = lax.dot_general(k_aug, q_aug[pr], NT_DIMS, preferred_element_type=F32)
                if causal:
                    cbias = jnp.where(key_i + ks0 <= t_i, 0.0, NEG_INF)
                    s = s + jnp.concatenate([cbias, cbias], axis=1)
                scores.append(s)
            return scores

        def sel_tiles(kt0, count, carry, diagonal_last=False):
            m, acc = list(carry[0]), list(carry[1])
            by_pair = [[sel_scores(kt0 + u, diagonal_last and u == count - 1, pr) for u in range(count)]
                       for pr in range(n_pairs)]
            scores = [[by_pair[pr][u][sub] for sub in range(tk // SEL_SUB) for pr in range(n_pairs)]
                      for u in range(count)]
            for u in range(count):
                for sub in range(tk // SEL_SUB):
                    c0 = (kt0 + u) * (tk // chunk) + sub * (SEL_SUB // chunk)
                    vt = jnp.concatenate([vst_ref[c0 + c] for c in range(SEL_SUB // chunk)], axis=1)
                    for pr in range(n_pairs):
                        s = scores[u][sub * n_pairs + pr]
                        m_new = jnp.maximum(m[pr], jnp.max(s, axis=0, keepdims=True))
                        p = jnp.exp2(s - m_new).astype(BF16)
                        acc[pr] = jnp.exp2(m[pr] - m_new) * acc[pr] + jnp.dot(vt, p, preferred_element_type=F32)
                        m[pr] = m_new
            return tuple(m), tuple(acc)

        n_tiles = (t0 + tq + tk - 1) // tk
        n_groups = (n_tiles - 1) // SEL_UNROLL
        n_last = n_tiles - n_groups * SEL_UNROLL
        carry = (tuple(jnp.full((1, 2 * tq), M_INIT, F32) for _ in range(n_pairs)),
                 tuple(jnp.zeros((VT_ROWS, 2 * tq), F32) for _ in range(n_pairs)))
        carry = lax.fori_loop(0, n_groups, lambda it, c: sel_tiles(it * SEL_UNROLL, SEL_UNROLL, c), carry)
        last = [functools.partial(sel_tiles, n_groups * SEL_UNROLL, count, diagonal_last=True)
                for count in range(1, SEL_UNROLL + 1)]
        _, acc_pairs = lax.switch(n_last - 1, last, carry)
        acc = jnp.concatenate(acc_pairs, axis=1)
        l_s = acc[HEAD_DIM:HEAD_DIM + 1, :]
        return acc[:HEAD_DIM, :] * jnp.where(l_s > 0.0, 1.0 / l_s, 0.0)

    def epilogue(u, o_ct, o_st, o_wt):
        rows = slice(u * tq, (u + 1) * tq)
        sig_ref[u] = (1.0 / (1.0 + jnp.exp(-gl_ref[rows, :]))).T
        gate = gate_ref[rows, :].astype(F32)
        for h in range(hg):
            base = (g * hg + h) * 3
            gc, gs, gw = (sig_ref[u, pl.ds(base + c, 1), :] for c in range(3))
            o_t = gc * o_ct[:, lanes(h)] + gs * o_st[:, lanes(h)] + gw * o_wt[:, lanes(h)]
            o_ref[rows, head(h)] = (o_t.T * _silu(gate[:, head(h)])).astype(o_ref.dtype)

    pro = [prologue(u) for u in range(n_sub)]
    swept = [sweep(t0, q_aug) for t0, q_aug, _, _ in pro]
    for u in range(n_sub):
        epilogue(u, pro[u][2], swept[u], pro[u][3])


def _nsa_attention(h2, small, kc, vct, vt, B, S, *, q_tile0, kv_block0, gate_tile0, tq, tk, n_sub):
    T = h2.shape[0]
    G, hg = NSA_KV_GROUPS, NSA_HEADS_PER_GROUP
    n_cmp = S // NSA_CMP_STRIDE
    n_blk = S // NSA_SEL_LEN
    nQ = S // (tq * n_sub)
    gw = hg * HEAD_DIM
    assert n_blk <= NSA_BLK_PAD and S % tk == 0 and S >= NSA_WINDOW + tq and tq == HEAD_DIM and S % (tq * n_sub) == 0
    assert tk % SEL_SUB == 0 and (tk // NSA_SEL_LEN) % 8 == 0 and hg % 2 == 0

    nn = np.arange(n_cmp)[None, :]
    jj = np.arange(NSA_BLK_PAD)[:, None]
    ovt = ((nn * NSA_CMP_STRIDE + NSA_CMP_LEN - 1 >= jj * NSA_SEL_LEN) & (nn * NSA_CMP_STRIDE < (jj + 1) * NSA_SEL_LEN)
           & (nn < n_cmp - 1) & (jj < n_blk))

    block_onehot = (np.arange(S)[:, None] // NSA_SEL_LEN) == np.arange(NSA_BLK_PAD)[None, :]

    def k_spec(c):
        return pl.BlockSpec((S, HEAD_DIM), lambda b, g, i: (b, kv_block0 + 4 * c + g))

    def vt_spec(which):
        return pl.BlockSpec((None, None, S // HEAD_DIM, VT_ROWS, HEAD_DIM), lambda b, g, i: (b, which * G + g, 0, 0, 0))

    rows = lambda width, col: pl.BlockSpec((tq * n_sub, width), lambda b, g, i: (b * nQ + i, col(g)))
    per_bg = lambda shape: pl.BlockSpec((None,) + shape, lambda b, g, i: (b * G + g, 0, 0))
    const = lambda shape: pl.BlockSpec(shape, lambda b, g, i: (0,) * len(shape))
    return pl.pallas_call(
        functools.partial(_nsa_kernel, tq=tq, tk=tk, n_sel=min(NSA_N_SEL, n_blk), n_sub=n_sub),
        out_shape=jax.ShapeDtypeStruct((T, G * gw), BF16),
        grid=(B, G, nQ),
        in_specs=[rows(gw, lambda g: q_tile0 + g), per_bg((n_cmp, HEAD_DIM)), per_bg((HEAD_DIM, n_cmp)),
                  k_spec(2), vt_spec(0), k_spec(4), vt_spec(1),
                  rows(HEAD_DIM, lambda g: 0), rows(gw, lambda g: gate_tile0 + g),
                  const((NSA_BLK_PAD, n_cmp)), const((S, NSA_BLK_PAD))],
        out_specs=rows(gw, lambda g: g),
        scratch_shapes=[pltpu.VMEM((n_sub, HEAD_DIM, tq), F32)],
        compiler_params=_params("parallel", "parallel", "arbitrary"),
        name="nsa_attention",
    )(h2, kc, vct, h2, vt, h2, vt, small, h2, jnp.asarray(ovt, BF16), jnp.asarray(block_onehot, BF16))


def _gla_kernel(q_ref, k_ref, v_ref, a_ref, gate_ref, wa_ref, ba_ref, ng_ref, o_ref, state_ref, intra_ref,
                *, n_chunks, n_heads):
    C = GLA_CHUNK

    @pl.when(pl.program_id(2) == 0)
    def _():
        state_ref[...] = jnp.zeros_like(state_ref)

    r_i = lax.broadcasted_iota(jnp.int32, (C, C), 0)
    c_i = lax.broadcasted_iota(jnp.int32, (C, C), 1)
    causal = r_i >= c_i
    tri = jnp.where(causal, 1.0, 0.0).astype(BF16)
    ng = ng_ref[...]
    heads = range(n_heads)
    kcols = lambda h: slice(h * GLA_DK, (h + 1) * GLA_DK)
    vcols = lambda h: slice(h * GLA_DV, (h + 1) * GLA_DV)

    qe_all, kd_all, decay_all = [], [], []
    for c in range(n_chunks):
        rows = slice(c * C, (c + 1) * C)
        a = a_ref[rows, :]
        pre = [jnp.dot(a, wa_ref[h], preferred_element_type=F32) + ba_ref[h] for h in heads]
        log_a = [(jnp.minimum(p, 0.0) - jnp.log(1.0 + jnp.exp(-jnp.abs(p)))) * (1.0 / GLA_TAU) for p in pre]
        bcum = [sum(jnp.dot(tri, part, preferred_element_type=F32) for part in _split3(la)) for la in log_a]
        b_last = [bc[C - 1:C, :] for bc in bcum]
        q = [q_ref[rows, kcols(h)].astype(F32) for h in heads]
        k = [k_ref[rows, kcols(h)].astype(F32) for h in heads]
        qe = [(q[h] * jnp.exp(bcum[h])).astype(BF16) for h in heads]
        ke = [(k[h] * jnp.exp(-bcum[h])).astype(BF16) for h in heads]
        kd = [(k[h] * jnp.exp(b_last[h] - bcum[h])).astype(BF16) for h in heads]
        attn = [lax.dot_general(qe[h], ke[h], NT_DIMS, preferred_element_type=F32) for h in heads]
        attn = [jnp.where(causal, at, 0.0).astype(BF16) for at in attn]
        for h in heads:
            intra_ref[rows, vcols(h)] = jnp.dot(attn[h], v_ref[rows, vcols(h)], preferred_element_type=F32)
        qe_all.append(qe)
        kd_all.append(kd)
        decay_all.append([jnp.exp(bl) for bl in b_last])

    for c in range(n_chunks):
        rows = slice(c * C, (c + 1) * C)
        state_t = [state_ref[h] for h in heads]
        inter = [lax.dot_general(qe_all[c][h], state_t[h].astype(BF16), NT_DIMS, preferred_element_type=F32)
                 for h in heads]
        upd = [lax.dot_general(v_ref[rows, vcols(h)], kd_all[c][h], TN_DIMS, preferred_element_type=F32) for h in heads]
        for h in heads:
            state_ref[h] = state_t[h] * decay_all[c][h] + upd[h]
        for h in heads:
            o = intra_ref[rows, vcols(h)] + inter[h]
            o = o * lax.rsqrt(jnp.mean(o * o, axis=-1, keepdims=True) + NORM_EPS) * ng
            o_ref[rows, vcols(h)] = (o * _silu(gate_ref[rows, vcols(h)].astype(F32))).astype(o_ref.dtype)


def _gla(h2, small, w_a2, b_a, norm_g, B, S, *, q_col0, k_col0, v_col0, gate_col0, rows, heads_per_step):
    T = h2.shape[0]
    H, hp = GLA_HEADS, heads_per_step
    nR = S // rows
    a_lo = 3 * NSA_KV_GROUPS * NSA_HEADS_PER_GROUP
    assert H % hp == 0 and all(c % (hp * w) == 0 for c, w in ((q_col0, GLA_DK), (k_col0, GLA_DK), (v_col0, GLA_DV),
                                                              (gate_col0, GLA_DV)))
    wa = jnp.zeros((H, HEAD_DIM, GLA_DK), F32).at[:, a_lo:a_lo + GLA_RANK, :].set(
        w_a2.reshape(GLA_RANK, H, GLA_DK).transpose(1, 0, 2))
    blk = lambda width, col0: pl.BlockSpec((rows, hp * width), lambda b, g, i: (b * nR + i, col0 // (hp * width) + g))
    per_head = lambda shape: pl.BlockSpec((hp,) + shape, lambda b, g, i: (g, 0, 0))
    return pl.pallas_call(
        functools.partial(_gla_kernel, n_chunks=rows // GLA_CHUNK, n_heads=hp),
        out_shape=jax.ShapeDtypeStruct((T, H * GLA_DV), BF16),
        grid=(B, H // hp, nR),
        in_specs=[blk(GLA_DK, q_col0), blk(GLA_DK, k_col0), blk(GLA_DV, v_col0),
                  pl.BlockSpec((rows, HEAD_DIM), lambda b, g, i: (b * nR + i, 0)), blk(GLA_DV, gate_col0),
                  per_head((HEAD_DIM, GLA_DK)), per_head((1, GLA_DK)),
                  pl.BlockSpec((1, GLA_DV), lambda b, g, i: (0, 0))],
        out_specs=blk(GLA_DV, 0),
        scratch_shapes=[pltpu.VMEM((hp, GLA_DV, GLA_DK), F32), pltpu.VMEM((rows, hp * GLA_DV), F32)],
        compiler_params=_params("parallel", "parallel", "arbitrary"),
        name="gla",
    )(h2, h2, h2, small, h2, wa, b_a.reshape(H, 1, GLA_DK), norm_g.reshape(1, GLA_DV))


def _dilated_scores(q, kp, kc):
    head = lambda h: slice(h * HEAD_DIM, (h + 1) * HEAD_DIM)
    return [lax.dot_general(q[:, head(h)], jnp.concatenate([kp[:, head(h)], kc[:, head(h)]], axis=0), NT_DIMS,
                            preferred_element_type=F32) for h in range(DIL_HEADS)]


def _dilated_finish(i, scores, vp, vc, prev, gate, o_scr, m_scr, d_scr):
    tq = o_scr.shape[0]
    head = lambda h: slice(h * HEAD_DIM, (h + 1) * HEAD_DIM)
    row = lax.broadcasted_iota(jnp.int32, (tq, 2 * tq), 0)
    col = lax.broadcasted_iota(jnp.int32, (tq, 2 * tq), 1)
    valid = (col >= row + (tq - DIL_BACK)) & (col <= row + tq) & ((col >= tq) | (i > 0))
    m_scr[...] = jnp.zeros(m_scr.shape, F32)
    d_scr[...] = jnp.ones(d_scr.shape, F32)
    for h in range(DIL_HEADS):
        v = jnp.concatenate([vp[:, head(h)], vc[:, head(h)]], axis=0)
        s = jnp.where(valid, scores[h], NEG_INF)
        m = jnp.max(s, axis=1, keepdims=True)
        e = jnp.exp(s - m)
        o_scr[:, head(h)] = jnp.dot(e.astype(BF16), v, preferred_element_type=F32)
        m_scr[:, h:h + 1] = m
        d_scr[:, h:h + 1] = jnp.sum(e, axis=1, keepdims=True)
    den = d_scr[...]
    lse = m_scr[...] + jnp.log(den)
    w_cur = 1.0 / den
    if prev is not None:
        prev_o, lse_p = prev
        lse_n = jnp.maximum(lse_p, lse) + jnp.log(1.0 + jnp.exp(-jnp.abs(lse_p - lse)))
        w_prev = jnp.exp(lse_p - lse_n)
        w_cur = jnp.exp(lse - lse_n) * w_cur
        lse = lse_n
    outs = []
    for h in range(DIL_HEADS):
        o = o_scr[:, head(h)] * w_cur[:, h:h + 1]
        if prev is not None:
            o = o + prev_o[:, head(h)].astype(F32) * w_prev[:, h:h + 1]
        if gate is not None:
            o = o * _silu(gate[:, head(h)].astype(F32))
        outs.append(o)
    return outs, lse


def _dilated_kernel(*refs, has_prev, is_last, n_sub):
    q_ref, kp_ref, kc_ref, vp_ref, vc_ref = refs[:5]
    rest = list(refs[5:])
    po_ref, pl_ref = (rest.pop(0), rest.pop(0)) if has_prev else (None, None)
    gate_ref = rest.pop(0) if is_last else None
    o_ref = rest.pop(0)
    lse_ref = None if is_last else rest.pop(0)
    o_scr, m_scr, d_scr = rest
    tq = o_scr.shape[1]
    tile = lambda ref, u: ref.at[pl.ds(u * tq, tq)]
    before = lambda prev_ref, cur_ref, u: prev_ref if u == 0 else tile(cur_ref, u - 1)
    scores = [_dilated_scores(tile(q_ref, u), before(kp_ref, kc_ref, u), tile(kc_ref, u)) for u in range(n_sub)]
    for u in range(n_sub):
        prev = (tile(po_ref, u), tile(pl_ref, u)[...]) if has_prev else None
        outs, lse = _dilated_finish(pl.program_id(2) * n_sub + u, scores[u], before(vp_ref, vc_ref, u),
                                    tile(vc_ref, u), prev, tile(gate_ref, u) if is_last else None,
                                    o_scr.at[u], m_scr.at[u], d_scr.at[u])
        for h, o in enumerate(outs):
            o_ref[pl.ds(u * tq, tq), h * HEAD_DIM:(h + 1) * HEAD_DIM] = o.astype(o_ref.dtype)
        if lse_ref is not None:
            lse_ref[pl.ds(u * tq, tq), :] = lse


def _dilated_scratch(tq, W, n=None):
    lead = () if n is None else (n,)
    return [pltpu.VMEM(lead + (tq, W), F32), pltpu.VMEM(lead + (tq, HEAD_DIM), F32),
            pltpu.VMEM(lead + (tq, HEAD_DIM), F32)]


def _dilated_group(hg, prev_o, prev_lse, *, tq, is_last, n_sub=1):
    B, d, L, WN = hg.shape
    W = WN // (4 if is_last else 3)
    has_prev = prev_o is not None
    n_sub = n_sub if L % (n_sub * tq) == 0 else 1
    ts = n_sub * tq
    assert tq == DIL_BACK and L % ts == 0
    cur = lambda col: pl.BlockSpec((None, None, ts, W), lambda b, r, i: (b, r, i, col))
    prv = lambda col: pl.BlockSpec((None, None, tq, W), lambda b, r, i: (b, r, jnp.maximum(n_sub * i - 1, 0), col))
    tok = lambda width: pl.BlockSpec((None, ts, width), lambda b, r, i: (b, i, r))
    in_specs = [cur(0), prv(1), cur(1), prv(2), cur(2)]
    args = [hg] * 5
    if has_prev:
        in_specs += [tok(W), tok(HEAD_DIM)]
        args += [prev_o, prev_lse]
    if is_last:
        in_specs += [cur(3)]
        args += [hg]
    out_shape = [jax.ShapeDtypeStruct((B, L, d * W), BF16)]
    out_specs = [tok(W)]
    if not is_last:
        out_shape += [jax.ShapeDtypeStruct((B, L, d * HEAD_DIM), F32)]
        out_specs += [tok(HEAD_DIM)]
    res = pl.pallas_call(
        functools.partial(_dilated_kernel, has_prev=has_prev, is_last=is_last, n_sub=n_sub),
        out_shape=tuple(out_shape),
        grid=(B, d, L // ts),
        in_specs=in_specs,
        out_specs=tuple(out_specs),
        scratch_shapes=_dilated_scratch(tq, W, n_sub),
        compiler_params=_params("parallel", "parallel", "arbitrary"),
        name=f"dilated_d{d}",
    )(*args)
    return res[0] if is_last else (res[0], res[1])


def _regroup_kernel(*refs, d, ratio):
    q_ref, kp_ref, kc_ref, vp_ref, vc_ref = refs[:5]
    po_refs = refs[5:5 + ratio]
    pl_refs = refs[5 + ratio:5 + 2 * ratio]
    o_ref, lse_ref, o_scr, m_scr, d_scr, so_scr, sl_scr = refs[5 + 2 * ratio:]
    i = pl.program_id(1)
    r = pl.program_id(2)
    tq, W = o_scr.shape
    rows_c = tq // ratio
    n = d * tq
    shift = lambda v: v.bit_length() - 1
    dst = lax.broadcasted_iota(jnp.int32, (tq, tq), 0)
    src = lax.broadcasted_iota(jnp.int32, (tq, tq), 1)
    gather = jnp.where(src == (dst & (ratio - 1)) * rows_c + (dst >> shift(ratio)), 1.0, 0.0).astype(BF16)
    prev_o = jnp.dot(gather, jnp.concatenate([p[...] for p in po_refs], axis=0), preferred_element_type=F32)
    prev_l = sum(jnp.dot(gather, part, preferred_element_type=F32)
                 for part in _split3(jnp.concatenate([p[...] for p in pl_refs], axis=0)))
    outs, lse = _dilated_finish(i, _dilated_scores(q_ref, kp_ref, kc_ref), vp_ref, vc_ref, (prev_o, prev_l), None,
                                o_scr, m_scr, d_scr)
    so_scr[r] = jnp.concatenate(outs, axis=1).astype(BF16)
    sl_scr[r] = lse

    @pl.when(r == d - 1)
    def _():
        dst = lax.broadcasted_iota(jnp.int32, (n, n), 0)
        src = lax.broadcasted_iota(jnp.int32, (n, n), 1)
        scatter = jnp.where(src == (dst & (d - 1)) * tq + (dst >> shift(d)), 1.0, 0.0).astype(BF16)
        o_ref[...] = jnp.dot(scatter, so_scr[...].reshape(n, W), preferred_element_type=F32).astype(o_ref.dtype)
        lse_ref[...] = sum(jnp.dot(scatter, part, preferred_element_type=F32)
                           for part in _split3(sl_scr[...].reshape(n, HEAD_DIM)))


def _dilated_regroup(hg, prev_o, prev_lse, *, fine, tq):
    B, d, L, W3 = hg.shape
    W = W3 // 3
    S = L * d
    ratio = fine // d
    assert tq == DIL_BACK and L % tq == 0 and fine % d == 0 and tq % ratio == 0
    assert d & (d - 1) == 0 and ratio & (ratio - 1) == 0
    rows_c = tq // ratio
    cur = lambda col: pl.BlockSpec((None, None, tq, W), lambda b, i, r: (b, r, i, col))
    prv = lambda col: pl.BlockSpec((None, None, tq, W), lambda b, i, r: (b, r, jnp.maximum(i - 1, 0), col))
    piece = lambda width: [pl.BlockSpec((None, rows_c, width), lambda b, i, r, j=j: (b, i, j * d + r))
                           for j in range(ratio)]
    return pl.pallas_call(
        functools.partial(_regroup_kernel, d=d, ratio=ratio),
        out_shape=(jax.ShapeDtypeStruct((B, S, W), BF16), jax.ShapeDtypeStruct((B, S, HEAD_DIM), F32)),
        grid=(B, L // tq, d),
        in_specs=[cur(0), prv(1), cur(1), prv(2), cur(2)] + piece(W) + piece(HEAD_DIM),
        out_specs=(pl.BlockSpec((None, d * tq, W), lambda b, i, r: (b, i, 0)),
                   pl.BlockSpec((None, d * tq, HEAD_DIM), lambda b, i, r: (b, i, 0))),
        scratch_shapes=_dilated_scratch(tq, W) + [pltpu.VMEM((d, tq, W), BF16), pltpu.VMEM((d, tq, HEAD_DIM), F32)],
        compiler_params=_params("parallel", "arbitrary", "arbitrary"),
        name=f"dilated_regroup_d{d}",
    )(*([hg] * 5 + [prev_o] * ratio + [prev_lse] * ratio))


def _row_tile(n, target):
    t = min(n, target)
    assert n % t == 0
    return t


def _even_layer(xb4, w_in, pe_k, w1_k, w2_k, pe_v, w1_v, w2_v, gla_w_a2, gla_b_a, gla_norm_g, w_out):
    B, _, S, D = xb4.shape
    T = B * S
    nq = NSA_KV_GROUPS * NSA_HEADS_PER_GROUP * HEAD_DIM
    nkv = 6 * NSA_KV_GROUPS * HEAD_DIM
    ng = 3 * NSA_KV_GROUPS * NSA_HEADS_PER_GROUP
    gk = GLA_HEADS * GLA_DK
    gv = GLA_HEADS * GLA_DV
    o_g = nq + nkv
    o_a = o_g + ng + nq + 2 * gk + gv
    w16 = w_in.astype(BF16)
    w_main = jnp.concatenate([w16[:, :o_g], w16[:, o_g + ng:o_a], w16[:, o_a + GLA_RANK:]], axis=1)
    w_small = jnp.concatenate([w16[:, o_g:o_g + ng], w16[:, o_a:o_a + GLA_RANK],
                               jnp.zeros((D, HEAD_DIM - ng - GLA_RANK), BF16)], axis=1)
    tn = 512
    plain, rope = (False, 1.0), (True, 1.0)
    groups = [("nsa_q", [(True, Q_SCALE * LOG2E)] * (nq // tn)),
              ("nsa_kv", [plain, plain, rope, plain, rope, plain]),
              ("nsa_gate", [plain] * (nq // tn)), ("gla_q", [(False, GLA_Q_SCALE)] * (gk // tn)),
              ("gla_k", [plain] * (gk // tn)), ("gla_v", [plain] * (gv // tn)), ("gla_gate", [plain] * (gv // tn))]
    w_block, n_seen = {}, 0
    for name, kinds in groups:
        w_block[name] = n_seen
        n_seen += len(kinds)
    order = ["nsa_q", "gla_v", "gla_gate", "nsa_gate", "gla_q", "gla_k", "nsa_kv"]
    tiles, col = [], {}
    for name in order:
        col[name] = len(tiles) * tn
        tiles += [(w_block[name] + t,) + kind for t, kind in enumerate(dict(groups)[name])]
    pos = jnp.arange(S)[None]
    tm = _row_tile(S, 1024)
    h2 = _project(xb4, w_main, tiles, pos, tm=tm, tn=tn, out_dtype=BF16).reshape(T, -1)
    small = _project(xb4, w_small, [(0,) + plain], pos, tm=tm, tn=HEAD_DIM, out_dtype=F32).reshape(T, HEAD_DIM)

    kv_block0 = col["nsa_kv"] // HEAD_DIM
    kc, vct = _nsa_compress(h2, B, S, pe_k, w1_k, w2_k, pe_v, w1_v, w2_v, kv_block0, kv_block0 + NSA_KV_GROUPS)
    v_blocks = [kv_block0 + 4 * c + g for c in (3, 5) for g in range(NSA_KV_GROUPS)]
    vt = _transpose_values(h2, B, S, v_blocks, rows=_row_tile(S, 1024))
    o_nsa = _nsa_attention(h2, small, kc, vct, vt, B, S, q_tile0=col["nsa_q"] // tn, kv_block0=kv_block0,
                           gate_tile0=col["nsa_gate"] // tn, tq=128, tk=_row_tile(S, 512), n_sub=4)
    o_gla = _gla(h2, small, gla_w_a2, gla_b_a, gla_norm_g, B, S, q_col0=col["gla_q"], k_col0=col["gla_k"],
                 v_col0=col["gla_v"], gate_col0=col["gla_gate"], rows=_row_tile(S, 512), heads_per_step=GLA_HEADS)
    return _out_project([o_nsa, o_gla], w_out, tm=_row_tile(T, 1024), tn=512)


def _odd_layer(streams, w_in, w_out):
    B, _, S, D = streams[1].shape
    T = B * S
    n_g = len(DIL_GROUPS)
    W = DIL_HEADS * HEAD_DIM
    tn = 512
    blocks = lambda col0: [col0 // tn + t for t in range(W // tn)]

    def group_qkv(gi, d, with_gate=False):
        L = S // d
        tiles = ([(blk, True, Q_SCALE) for blk in blocks(gi * W)] + [(blk, True, 1.0) for blk in blocks((n_g + gi) * W)]
                 + [(blk, False, 1.0) for blk in blocks((2 * n_g + gi) * W)])
        if with_gate:
            tiles += [(blk, False, 1.0) for blk in blocks(3 * n_g * W)]
        tm = _row_tile(L, 1024)
        return _project(streams[d], w_in, tiles, jnp.arange(S).reshape(L, d).T, tm=tm, tn=tn, out_dtype=BF16,
                        streams_per_step=min(d, 1024 // tm) if tm == L else 1, n_split=max(1, tm // 256))

    order = sorted(range(n_g), key=lambda gi: -DIL_GROUPS[gi][1])
    assert len(order) == 3 and DIL_GROUPS[order[-1]][1] == 1
    for gi in order:
        assert DIL_GROUPS[gi][0] // DIL_GROUPS[gi][1] == DIL_BACK
    d_hi, d_mid = DIL_GROUPS[order[0]][1], DIL_GROUPS[order[1]][1]
    o, lse = _dilated_group(group_qkv(order[0], d_hi), None, None, tq=DIL_BACK, is_last=False)
    o, lse = _dilated_regroup(group_qkv(order[1], d_mid), o, lse, fine=d_hi, tq=DIL_BACK)
    o = _dilated_group(group_qkv(order[2], 1, with_gate=True), o, lse, tq=DIL_BACK, is_last=True)
    return _out_project([o.reshape(T, W)], w_out, tm=_row_tile(T, 1024), tn=512)


def kernel(x, l0_w_in, l0_nsa_pe_k, l0_nsa_w1_k, l0_nsa_w2_k, l0_nsa_pe_v, l0_nsa_w1_v, l0_nsa_w2_v, l0_gla_w_a2, l0_gla_b_a, l0_gla_norm_g, l0_w_out, l0_ln_g, l0_ln_b, l1_w_in, l1_w_out, l1_ln_g, l1_ln_b):
    B, S, D = x.shape
    tr = _row_tile(S, 256)
    dils = sorted({d for _, d in DIL_GROUPS} | {1})
    y0 = _even_layer(x.astype(BF16)[:, None], l0_w_in, l0_nsa_pe_k, l0_nsa_w1_k, l0_nsa_w2_k, l0_nsa_pe_v, l0_nsa_w1_v,
                     l0_nsa_w2_v, l0_gla_w_a2, l0_gla_b_a, l0_gla_norm_g, l0_w_out)
    x1, *x1_streams = _deepnorm_ln(x, y0, l0_ln_g, l0_ln_b, tr=tr, dilations=dils)
    y1 = _odd_layer(dict(zip(dils, x1_streams)), l1_w_in, l1_w_out)
    return _deepnorm_ln(x1, y1, l1_ln_g, l1_ln_b, tr=tr)[0]
```
